```python
import jax, jax.numpy as jnp
from jax import lax
import numpy as np

D_MODEL = 1024
BATCH = 2
SEQ = 16384
DEPTH = 4

HEAD_DIM = 64
Q_BLOCK = 128
NSA_HEADS = 8
NSA_KV = 1
NSA_GROUP = NSA_HEADS // NSA_KV
CMP_STRIDE = 16
CMP_LEN = 2 * CMP_STRIDE
CMP_HID = 256
SEL_BLOCK = 64
SEL_TOPK = 8
NSA_WINDOW = 512
SB_HEADS = 4
SWA_HEADS = 4
SWA_KV = 2
SWA_GROUP = SWA_HEADS // SWA_KV
SWA_WINDOW = 128
N_EXPERTS = 16
N_GROUPS = 4
EXPERTS_PER_GROUP = N_EXPERTS // N_GROUPS
TOP_K = 2
D_EXPERT = D_MODEL // 2
MOE_BLOCK = 512
N_BRANCH = 3
EPS = 1e-6
NEG = -1e30
BIG = 1e30

A_WIDTH = NSA_HEADS * HEAD_DIM
B_WIDTH = SB_HEADS * HEAD_DIM
C_WIDTH = SWA_HEADS * HEAD_DIM
IN_WIDTHS = (A_WIDTH, 6 * NSA_KV * HEAD_DIM, 3 * NSA_HEADS, 3 * B_WIDTH, C_WIDTH, 2 * SWA_KV * HEAD_DIM, N_BRANCH * D_MODEL)
N_IN = sum(IN_WIDTHS)

kernel_name = "hybrid_nsa_stickbreak_swa_moe"

f32 = jnp.float32


def rmsnorm(x, g):
    xf = x.astype(f32)
    y = xf * lax.rsqrt(jnp.mean(xf * xf, axis=-1, keepdims=True) + EPS)
    return (y * g.astype(f32)).astype(x.dtype)


def alibi_slopes(n):
    return 2.0 ** (-8.0 * jnp.arange(1, n + 1, dtype=f32) / n)


def masked_softmax(s, mask):
    s = jnp.where(mask, s, NEG)
    m = jnp.max(s, axis=-1, keepdims=True)
    p = jnp.where(mask, jnp.exp(s - m), 0.0)
    return p / jnp.maximum(jnp.sum(p, axis=-1, keepdims=True), 1e-30)


def split_cols(z, widths):
    return jnp.split(z, np.cumsum(widths)[:-1].tolist(), axis=-1)


def nsa_compress(k, pe, w1, w2):
    B, S, G, HD = k.shape
    ch = k.reshape(B, S // CMP_STRIDE, CMP_STRIDE, G, HD)
    blk = jnp.concatenate([ch[:, :-1], ch[:, 1:]], axis=2) + pe[None, None, :, None, :]
    flat = jnp.transpose(blk, (0, 1, 3, 2, 4)).reshape(B, S // CMP_STRIDE - 1, G, CMP_LEN * HD)
    return jax.nn.gelu(flat @ w1) @ w2


def nsa_attention(q, kv, gates, cmp_pe_k, cmp_w1_k, cmp_w2_k, cmp_pe_v, cmp_w1_v, cmp_w2_v, q_gain, k_gain, slopes):
    B, S, G, HG, HD = q.shape
    NB = S // Q_BLOCK
    NC = S // CMP_STRIDE - 1
    NSEL = S // SEL_BLOCK
    KSEL = min(SEL_TOPK, NSEL)
    R = SEL_BLOCK // CMP_STRIDE
    scale = HD ** -0.5
    q = rmsnorm(q, q_gain)
    k_cmp = rmsnorm(nsa_compress(kv[:, :, 0], cmp_pe_k, cmp_w1_k, cmp_w2_k), k_gain[0])
    v_cmp = nsa_compress(kv[:, :, 1], cmp_pe_v, cmp_w1_v, cmp_w2_v)
    k_slc = jnp.transpose(rmsnorm(kv[:, :, 2], k_gain[1]), (0, 2, 1, 3))
    v_slc = jnp.transpose(kv[:, :, 3], (0, 2, 1, 3))
    pad = ((0, 0), (NSA_WINDOW, 0), (0, 0), (0, 0))
    k_win = jnp.pad(rmsnorm(kv[:, :, 4], k_gain[2]), pad)
    v_win = jnp.pad(kv[:, :, 5], pad)
    c_end = jnp.arange(NC) * CMP_STRIDE + (CMP_LEN - 1)
    blk_ids = jnp.arange(NSEL)
    ov_w = (0.5,) + (1.0,) * (R - 1) + (0.5,)
    alibi = slopes.astype(f32)[None, :, :, None, None]
    gather = jax.vmap(jax.vmap(lambda src, idx: src[idx]))

    def one_block(args):
        n, qb, gb = args
        t = n * Q_BLOCK + jnp.arange(Q_BLOCK)
        s = jnp.einsum('bqghd,bcgd->bghqc', qb, k_cmp, preferred_element_type=f32) * scale
        dist = (t[:, None] - c_end[None, :]).astype(f32)
        p_cmp = masked_softmax(s - alibi * dist, dist >= 0)
        o_cmp = jnp.einsum('bghqc,bcgd->bqghd', p_cmp.astype(v_cmp.dtype), v_cmp)
        imp = jnp.pad(p_cmp.sum(axis=2), ((0, 0), (0, 0), (0, 0), (1, R * NSEL - NC)))
        imp = sum(w * imp[..., r: r + R * NSEL: R] for r, w in enumerate(ov_w))
        cur = (t // SEL_BLOCK)[:, None]
        forced = (blk_ids == 0) | (blk_ids == cur) | (blk_ids == cur - 1)
        score = jnp.where(blk_ids <= cur, jnp.where(forced, BIG, imp), -jnp.inf)
        _, sel = lax.top_k(score, KSEL)
        tok = (sel[..., None] * SEL_BLOCK + jnp.arange(SEL_BLOCK)).reshape(B, G, Q_BLOCK, KSEL * SEL_BLOCK)
        ks = gather(k_slc, tok)
        vs = gather(v_slc, tok)
        s = jnp.einsum('bqghd,bgqtd->bghqt', qb, ks, preferred_element_type=f32) * scale
        dist = (t[:, None] - tok).astype(f32)[:, :, None]
        p = masked_softmax(s - alibi * dist, dist >= 0)
        o_slc = jnp.einsum('bghqt,bgqtd->bqghd', p.astype(vs.dtype), vs)
        kw = lax.dynamic_slice_in_dim(k_win, n * Q_BLOCK, Q_BLOCK + NSA_WINDOW, axis=1)
        vw = lax.dynamic_slice_in_dim(v_win, n * Q_BLOCK, Q_BLOCK + NSA_WINDOW, axis=1)
        pos = n * Q_BLOCK - NSA_WINDOW + jnp.arange(Q_BLOCK + NSA_WINDOW)
        dist = (t[:, None] - pos[None, :]).astype(f32)
        mask = (dist >= 0) & (dist < NSA_WINDOW) & (pos[None, :] >= 0)
        s = jnp.einsum('bqghd,bkgd->bghqk', qb, kw, preferred_element_type=f32) * scale
        p = masked_softmax(s - alibi * dist, mask)
        o_win = jnp.einsum('bghqk,bkgd->bqghd', p.astype(vw.dtype), vw)
        return gb[..., 0:1] * o_cmp + gb[..., 1:2] * o_slc + gb[..., 2:3] * o_win

    to_blocks = lambda z: jnp.swapaxes(z.reshape(B, NB, Q_BLOCK, *z.shape[2:]), 0, 1)
    o = lax.map(one_block, (jnp.arange(NB), to_blocks(q), to_blocks(gates)))
    return jnp.swapaxes(o, 0, 1).reshape(B, S, G * HG * HD)


def stick_breaking(q, k, v):
    B, S, H, HD = q.shape
    NB = S // Q_BLOCK
    q = q * HD ** -0.5
    j = jnp.arange(Q_BLOCK)
    tri_incl = (j[:, None] >= j[None, :]).astype(f32)
    outs = []
    for n in range(NB):
        L = (n + 1) * Q_BLOCK
        z = jnp.einsum('bqhd,bshd->bhqs', q[:, n * Q_BLOCK:L], k[:, :L], preferred_element_type=f32)
        mask = jnp.arange(L)[None, :] < (n * Q_BLOCK + j)[:, None]
        lk = jnp.where(mask, -jax.nn.softplus(z), 0.0).reshape(B, H, Q_BLOCK, n + 1, Q_BLOCK)
        within = jnp.einsum('bhqcj,js->bhqcs', lk, tri_incl)
        cb = jnp.arange(n + 1)
        later = jnp.einsum('bhqc,cd->bhqd', lk.sum(axis=-1), (cb[:, None] > cb[None, :]).astype(f32))
        r = (within + later[..., None]).reshape(B, H, Q_BLOCK, L)
        a = jnp.where(mask, jnp.exp(z + r), 0.0)
        outs.append(jnp.einsum('bhqs,bshd->bqhd', a.astype(v.dtype), v[:, :L]))
    return jnp.concatenate(outs, axis=1).reshape(B, S, H * HD)


def swa_with_sinks(q, k, v, sinks, slopes):
    B, S, G, HG, HD = q.shape
    NB = S // Q_BLOCK
    qb = q.reshape(B, NB, Q_BLOCK, G, HG, HD)

    def band(z):
        zp = jnp.pad(z, ((0, 0), (Q_BLOCK, 0), (0, 0), (0, 0))).reshape(B, NB + 1, Q_BLOCK, G, HD)
        return jnp.concatenate([zp[:, :-1], zp[:, 1:]], axis=2)

    kb, vb = band(k), band(v)
    r = jnp.arange(2 * Q_BLOCK)
    dist = Q_BLOCK + jnp.arange(Q_BLOCK)[:, None] - r[None, :]
    key_pos = (jnp.arange(NB) * Q_BLOCK - Q_BLOCK)[:, None] + r[None, :]
    mask = ((dist >= 0) & (dist < SWA_WINDOW))[None] & (key_pos >= 0)[:, None, :]
    mask = mask[None, :, None, None]
    s = jnp.einsum('bnqghd,bnkgd->bnghqk', qb, kb, preferred_element_type=f32) * HD ** -0.5
    s = jnp.where(mask, s - slopes.astype(f32)[:, :, None, None] * dist.astype(f32), NEG)
    sink = sinks.astype(f32)[:, :, None, None]
    m = jnp.maximum(jnp.max(s, axis=-1, keepdims=True), sink)
    p = jnp.where(mask, jnp.exp(s - m), 0.0)
    p = p / (jnp.sum(p, axis=-1, keepdims=True) + jnp.exp(sink - m))
    o = jnp.einsum('bnghqk,bnkgd->bnqghd', p.astype(vb.dtype), vb)
    return o.reshape(B, S, G * HG * HD)


def hybrid_mixer(h, w_in, cmp_pe_k, cmp_w1_k, cmp_w2_k, cmp_pe_v, cmp_w1_v, cmp_w2_v, nsa_q_gain, nsa_k_gain,
                 swa_q_gain, swa_k_gain, swa_sinks, w_branch_a, w_branch_b, w_branch_c, w_out, nsa_slopes, swa_slopes):
    B, S, D = h.shape
    q_a, kv_a, gate_a, qkv_b, q_c, kv_c, merge = split_cols(h @ w_in, IN_WIDTHS)
    o_a = nsa_attention(q_a.reshape(B, S, NSA_KV, NSA_GROUP, HEAD_DIM),
                        kv_a.reshape(B, S, 6, NSA_KV, HEAD_DIM),
                        jax.nn.sigmoid(gate_a).reshape(B, S, NSA_KV, NSA_GROUP, 3),
                        cmp_pe_k, cmp_w1_k, cmp_w2_k, cmp_pe_v, cmp_w1_v, cmp_w2_v,
                        nsa_q_gain, nsa_k_gain, nsa_slopes)
    qkv_b = qkv_b.reshape(B, S, 3, SB_HEADS, HEAD_DIM)
    o_b = stick_breaking(qkv_b[:, :, 0], qkv_b[:, :, 1], qkv_b[:, :, 2])
    kv_c = kv_c.reshape(B, S, 2, SWA_KV, HEAD_DIM)
    o_c = swa_with_sinks(rmsnorm(q_c.reshape(B, S, SWA_KV, SWA_GROUP, HEAD_DIM), swa_q_gain),
                         rmsnorm(kv_c[:, :, 0], swa_k_gain), kv_c[:, :, 1],
                         swa_sinks.reshape(SWA_KV, SWA_GROUP), swa_slopes)
    g = jax.nn.sigmoid(merge).reshape(B, S, N_BRANCH, D)
    y = g[:, :, 0] * (o_a @ w_branch_a) + g[:, :, 1] * (o_b @ w_branch_b) + g[:, :, 2] * (o_c @ w_branch_c)
    return y @ w_out


def grouped_moe(h, w_router, router_bias, w_gate, w_up, w_down):
    B, S, D = h.shape
    hf = h.reshape(-1, D)
    N = hf.shape[0]
    scores = jax.nn.sigmoid((hf @ w_router).astype(f32))
    sel = (scores + router_bias.astype(f32)).reshape(N, N_GROUPS, EXPERTS_PER_GROUP)
    grp_score = lax.top_k(sel, TOP_K)[0].sum(axis=-1)
    g_idx = jnp.argmax(grp_score, axis=-1)
    in_grp = jnp.take_along_axis(sel, g_idx[:, None, None], axis=1)[:, 0]
    _, e_local = lax.top_k(in_grp, TOP_K)
    e_idx = (g_idx[:, None] * EXPERTS_PER_GROUP + e_local).astype(jnp.int32)
    w = jnp.take_along_axis(scores, e_idx, axis=-1)
    w = w / jnp.sum(w, axis=-1, keepdims=True)
    NK = N * TOP_K
    n_blk = -(-NK // MOE_BLOCK) + N_EXPERTS
    flat_e = e_idx.reshape(-1)
    order = jnp.argsort(flat_e)
    e_sorted = flat_e[order]
    counts = jnp.bincount(flat_e, length=N_EXPERTS)
    padded = (counts + MOE_BLOCK - 1) // MOE_BLOCK * MOE_BLOCK
    p_end = jnp.cumsum(padded)
    p_start = p_end - padded
    start = jnp.cumsum(counts) - counts
    dest = p_start[e_sorted] + jnp.arange(NK) - start[e_sorted]
    slot_tok = jnp.full((n_blk * MOE_BLOCK,), N, jnp.int32).at[dest].set((order // TOP_K).astype(jnp.int32))
    slot_w = jnp.zeros((n_blk * MOE_BLOCK,), hf.dtype).at[dest].set(w.reshape(-1)[order].astype(hf.dtype))
    blk_e = jnp.minimum(jnp.searchsorted(p_end, jnp.arange(n_blk) * MOE_BLOCK, side='right'), N_EXPERTS - 1)
    xs = jnp.concatenate([hf, jnp.zeros((1, D), hf.dtype)], axis=0)[slot_tok].reshape(n_blk, MOE_BLOCK, D)
    a = jax.nn.silu(jnp.einsum('nrd,ndf->nrf', xs, w_gate[blk_e])) * jnp.einsum('nrd,ndf->nrf', xs, w_up[blk_e])
    y = jnp.einsum('nrf,nfd->nrd', a, w_down[blk_e]).reshape(-1, D) * slot_w[:, None]
    out = jnp.zeros((N + 1, D), hf.dtype).at[slot_tok].add(y)[:N]
    return out.reshape(B, S, D)


def setup_inputs(seed: int = 0) -> dict:
    key = jax.random.key(seed)
    ks = jax.random.split(key, 27)
    nrm = lambda k, shape, s: jax.random.normal(k, shape, f32) * s
    L, D, HD = DEPTH, D_MODEL, HEAD_DIM
    return {
        "x": nrm(ks[0], (BATCH, SEQ, D), 1.0),
        "c": nrm(ks[1], (BATCH, D), 1.0),
        "w_ada": nrm(ks[2], (L, D, 6 * D), 0.5 * D ** -0.5),
        "b_ada": nrm(ks[3], (L, 6 * D), 0.02),
        "g_norm_mix": 1.0 + nrm(ks[4], (L, D), 0.02),
        "g_norm_moe": 1.0 + nrm(ks[5], (L, D), 0.02),
        "w_in": nrm(ks[6], (L, D, N_IN), D ** -0.5),
        "cmp_pe_k": nrm(ks[7], (L, CMP_LEN, HD), 0.02),
        "cmp_w1_k": nrm(ks[8], (L, CMP_LEN * HD, CMP_HID), (CMP_LEN * HD) ** -0.5),
        "cmp_w2_k": nrm(ks[9], (L, CMP_HID, HD), CMP_HID ** -0.5),
        "cmp_pe_v": nrm(ks[10], (L, CMP_LEN, HD), 0.02),
        "cmp_w1_v": nrm(ks[11], (L, CMP_LEN * HD, CMP_HID), (CMP_LEN * HD) ** -0.5),
        "cmp_w2_v": nrm(ks[12], (L, CMP_HID, HD), CMP_HID ** -0.5),
        "nsa_q_gain": 1.0 + nrm(ks[13], (L, HD), 0.02),
        "nsa_k_gain": 1.0 + nrm(ks[14], (L, 3, HD), 0.02),
        "swa_q_gain": 1.0 + nrm(ks[15], (L, HD), 0.02),
        "swa_k_gain": 1.0 + nrm(ks[16], (L, HD), 0.02),
        "swa_sinks": nrm(ks[17], (L, SWA_HEADS), 1.0),
        "w_branch_a": nrm(ks[18], (L, A_WIDTH, D), A_WIDTH ** -0.5),
        "w_branch_b": nrm(ks[19], (L, B_WIDTH, D), B_WIDTH ** -0.5),
        "w_branch_c": nrm(ks[20], (L, C_WIDTH, D), C_WIDTH ** -0.5),
        "w_out": nrm(ks[21], (L, D, D), D ** -0.5),
        "w_router": nrm(ks[22], (D, N_EXPERTS), D ** -0.5),
        "router_bias": nrm(ks[23], (N_EXPERTS,), 0.01),
        "w_exp_gate": nrm(ks[24], (L, N_EXPERTS, D, D_EXPERT), D ** -0.5),
        "w_exp_up": nrm(ks[25], (L, N_EXPERTS, D, D_EXPERT), D ** -0.5),
        "w_exp_down": nrm(ks[26], (L, N_EXPERTS, D_EXPERT, D), D_EXPERT ** -0.5),
    }


def reference(x, c, w_ada, b_ada, g_norm_mix, g_norm_moe, w_in, cmp_pe_k, cmp_w1_k, cmp_w2_k, cmp_pe_v, cmp_w1_v,
              cmp_w2_v, nsa_q_gain, nsa_k_gain, swa_q_gain, swa_k_gain, swa_sinks, w_branch_a, w_branch_b,
              w_branch_c, w_out, w_router, router_bias, w_exp_gate, w_exp_up, w_exp_down):
    slopes = alibi_slopes(SWA_HEADS + NSA_HEADS)
    swa_slopes = slopes[:SWA_HEADS].reshape(SWA_KV, SWA_GROUP)
    nsa_slopes = slopes[SWA_HEADS:].reshape(NSA_KV, NSA_GROUP)
    cond = jax.nn.silu(c)
    for l in range(DEPTH):
        mod = cond @ w_ada[l] + b_ada[l]
        sh_a, sc_a, g_a, sh_m, sc_m, g_m = [m[:, None, :] for m in jnp.split(mod, 6, axis=-1)]
        h = rmsnorm(x, g_norm_mix[l]) * (1.0 + sc_a) + sh_a
        x = x + g_a * hybrid_mixer(h, w_in[l], cmp_pe_k[l], cmp_w1_k[l], cmp_w2_k[l], cmp_pe_v[l], cmp_w1_v[l],
                                   cmp_w2_v[l], nsa_q_gain[l], nsa_k_gain[l], swa_q_gain[l], swa_k_gain[l],
                                   swa_sinks[l], w_branch_a[l], w_branch_b[l], w_branch_c[l], w_out[l],
                                   nsa_slopes, swa_slopes)
        h = rmsnorm(x, g_norm_moe[l]) * (1.0 + sc_m) + sh_m
        x = x + g_m * grouped_moe(h, w_router, router_bias, w_exp_gate[l], w_exp_up[l], w_exp_down[l])
    return x
```

```python
import functools

import numpy as np
import jax
import jax.numpy as jnp
from jax import lax
from jax.experimental import pallas as pl
from jax.experimental.pallas import tpu as pltpu

f32 = jnp.float32
bf16 = jnp.bfloat16
i32 = jnp.int32

D_MODEL = 1024
HEAD_DIM = 64
Q_BLOCK = 128
NSA_HEADS = 8
CMP_STRIDE = 16
CMP_LEN = 32
CMP_HID = 256
SEL_BLOCK = 64
SEL_TOPK = 8
NSA_WINDOW = 512
SB_HEADS = 4
SWA_HEADS = 4
SWA_KV = 2
SWA_WINDOW = 128
N_EXPERTS = 16
N_GROUPS = 4
EXPERTS_PER_GROUP = 4
TOP_K = 2
D_EXPERT = 512
MOE_BLOCK = 512
EPS = 1e-6
NEG = -1e30
BIG = 1e30

A_WIDTH = NSA_HEADS * HEAD_DIM
B_WIDTH = SB_HEADS * HEAD_DIM
C_WIDTH = SWA_HEADS * HEAD_DIM

LANES = 128
SUBLANES = 8
VMEM_LIMIT = 56 * 1024 * 1024

_O_QA = 0
_O_KVA = _O_QA + A_WIDTH
_O_GA = _O_KVA + 6 * HEAD_DIM
_O_QB = _O_GA + 3 * NSA_HEADS
_O_KB = _O_QB + B_WIDTH
_O_VB = _O_KB + B_WIDTH
_O_QC = _O_VB + B_WIDTH
_O_KC = _O_QC + C_WIDTH
_O_VC = _O_KC + SWA_KV * HEAD_DIM
_O_MG = _O_VC + SWA_KV * HEAD_DIM
_O_END = _O_MG + 3 * D_MODEL

TOK_W = 6 * HEAD_DIM + 2 * B_WIDTH + 2 * SWA_KV * HEAD_DIM
CH_W = A_WIDTH + B_WIDTH + C_WIDTH + 32
_TOK_NORM_CHUNKS = (1, 2, 7)


def _cparams(sem):
    return pltpu.CompilerParams(dimension_semantics=sem, vmem_limit_bytes=VMEM_LIMIT)


def _split_dot(x, w):
    hi = x.astype(bf16)
    lo = (x - hi.astype(f32)).astype(bf16)
    return jnp.dot(hi, w, preferred_element_type=f32) + jnp.dot(lo, w, preferred_element_type=f32)


def _norm_mod(x, a, sh):
    rs = lax.rsqrt(jnp.mean(x * x, axis=-1, keepdims=True) + EPS)
    return x * rs * a + sh


def _group_norm_ch(y, gain_col, n, tm):
    y3 = y.reshape(n, HEAD_DIM, tm)
    ss = jnp.sum(y3 * y3, axis=1, keepdims=True)
    yn = y3 * lax.rsqrt(ss * (1.0 / HEAD_DIM) + EPS)
    return yn.reshape(n * HEAD_DIM, tm) * gain_col


def _proj_in_kernel(x_ref, a_ref, sh_ref, wtok_ref, wch_ref, gtok_ref, mtok_ref, gcha_ref, gchc_ref, bd_ref,
                    kva_ref, kvb_ref, kvc_ref, qta_ref, qtb_ref, qtc_ref, gt_ref, *, tm):
    h = _norm_mod(x_ref[...], a_ref[...], sh_ref[...]).astype(bf16)
    t = jnp.dot(h, wtok_ref[...], preferred_element_type=f32)
    cols = [t[:, c * LANES:(c + 1) * LANES] for c in range(TOK_W // LANES)]
    for c in _TOK_NORM_CHUNKS:
        y = cols[c]
        ss = _split_dot(y * y, bd_ref[...])
        yn = y * lax.rsqrt(ss * (1.0 / HEAD_DIM) + EPS) * gtok_ref[:, c * LANES:(c + 1) * LANES]
        cols[c] = jnp.where(mtok_ref[:, c * LANES:(c + 1) * LANES] > 0.0, yn, y)
    t = jnp.concatenate(cols, axis=1).astype(bf16)
    kva_ref[...] = t[:, 0:384]
    kvb_ref[...] = t[:, 384:896]
    kvc_ref[...] = t[:, 896:1152]
    c = lax.dot_general(wch_ref[...], h, (((1,), (1,)), ((), ())), preferred_element_type=f32)
    qta_ref[...] = _group_norm_ch(c[0:512], gcha_ref[...], NSA_HEADS, tm).astype(bf16)
    qtb_ref[...] = c[512:768].astype(bf16)
    qtc_ref[...] = _group_norm_ch(c[768:1024], gchc_ref[...], SWA_HEADS, tm).astype(bf16)
    gt_ref[...] = jax.nn.sigmoid(c[1024:1056])


def proj_in(x2, a_mod, sh_mod, wtok, wch, gtok, mtok, gcha, gchc, bd, *, B, S, tm=512):
    N, D = x2.shape
    tpb = S // tm
    row = lambda i: (i, 0)
    bmap = lambda i: (i // tpb, 0, 0)
    cmap = lambda i: (i // tpb, 0, i % tpb)
    full = lambda i: (0, 0)
    out_shape = (
        jax.ShapeDtypeStruct((N, 384), bf16), jax.ShapeDtypeStruct((N, 512), bf16), jax.ShapeDtypeStruct((N, 256), bf16),
        jax.ShapeDtypeStruct((B, 512, S), bf16), jax.ShapeDtypeStruct((B, 256, S), bf16),
        jax.ShapeDtypeStruct((B, 256, S), bf16), jax.ShapeDtypeStruct((B, 32, S), f32),
    )
    return pl.pallas_call(
        functools.partial(_proj_in_kernel, tm=tm),
        grid=(N // tm,),
        in_specs=[
            pl.BlockSpec((tm, D), row),
            pl.BlockSpec((None, 1, D), bmap), pl.BlockSpec((None, 1, D), bmap),
            pl.BlockSpec((D, TOK_W), full), pl.BlockSpec((CH_W, D), full),
            pl.BlockSpec((1, TOK_W), full), pl.BlockSpec((1, TOK_W), full),
            pl.BlockSpec((512, 1), full), pl.BlockSpec((256, 1), full),
            pl.BlockSpec((LANES, LANES), full),
        ],
        out_specs=(
            pl.BlockSpec((tm, 384), row), pl.BlockSpec((tm, 512), row), pl.BlockSpec((tm, 256), row),
            pl.BlockSpec((None, 512, tm), cmap), pl.BlockSpec((None, 256, tm), cmap),
            pl.BlockSpec((None, 256, tm), cmap), pl.BlockSpec((None, 32, tm), cmap),
        ),
        out_shape=out_shape,
        compiler_params=_cparams(("arbitrary",)),
        name="proj_in",
    )(x2, a_mod, sh_mod, wtok, wch, gtok, mtok, gcha, gchc, bd)


def _excl_suffix_prod8(t, rows):
    def shift_up(x, k):
        return jnp.where(rows < SUBLANES - k, pltpu.roll(x, SUBLANES - k, axis=0), 1.0)
    x = shift_up(t, 1)
    x = x * shift_up(x, 1)
    x = x * shift_up(x, 2)
    x = x * shift_up(x, 4)
    return x


def _sb_tile(kb, vb, qpad, accs, laters, mask3, TQ, TK):
    V = TK // SUBLANES
    rows = lax.broadcasted_iota(i32, (SUBLANES, TQ), 0)
    a_cols, new_laters = [], []
    for h in range(2):
        z = jnp.dot(kb, qpad[h], preferred_element_type=f32)
        e = jnp.exp(-jnp.abs(z))
        rc = 1.0 / (1.0 + e)
        erc = e * rc
        pos = z >= 0.0
        sig = jnp.where(pos, rc, erc).reshape(V, SUBLANES, TQ)
        beta = jnp.where(pos, erc, rc).reshape(V, SUBLANES, TQ)
        if mask3 is not None:
            sig = jnp.where(mask3, sig, 0.0)
            beta = jnp.where(mask3, beta, 1.0)
        run = jnp.ones((SUBLANES, TQ), f32)
        excl = [None] * V
        for v in reversed(range(V)):
            excl[v] = run
            run = run * beta[v]
        g = _excl_suffix_prod8(run, rows) * laters[h]
        a3 = jnp.stack([sig[v] * (excl[v] * g) for v in range(V)], axis=0)
        a_cols.append(a3.reshape(TK, TQ).astype(bf16))
        new_laters.append((g * run)[0:1, :])
    a2 = jnp.concatenate(a_cols, axis=1)
    r = lax.dot_general(vb, a2, (((0,), (0,)), ((), ())), preferred_element_type=f32)
    new_accs = [accs[0] + r[0:64, 0:TQ], accs[1] + r[64:128, TQ:2 * TQ]]
    return new_accs, new_laters


def _sb_kernel(k_ref, v_ref, qt_ref, o_ref, *, TQ):
    TK = TQ
    V = TK // SUBLANES
    i = pl.program_id(2)
    qt = qt_ref[...]
    zero = jnp.zeros((HEAD_DIM, TQ), bf16)
    qpad = [jnp.concatenate([qt[0:64], zero], axis=0), jnp.concatenate([zero, qt[64:128]], axis=0)]
    kidx = (lax.broadcasted_iota(i32, (V, SUBLANES, TQ), 1) * V + lax.broadcasted_iota(i32, (V, SUBLANES, TQ), 0))
    mask3 = kidx < lax.broadcasted_iota(i32, (V, SUBLANES, TQ), 2)

    def load(j):
        st = pl.multiple_of(j * TK, TK)
        return k_ref[pl.ds(st, TK), :], v_ref[pl.ds(st, TK), :]

    kb, vb = load(i)
    accs = [jnp.zeros((HEAD_DIM, TQ), f32)] * 2
    laters = [jnp.ones((1, TQ), f32)] * 2
    accs, laters = _sb_tile(kb, vb, qpad, accs, laters, mask3, TQ, TK)

    def body(jj, carry):
        a0, a1, l0, l1 = carry
        kb, vb = load(i - 1 - jj)
        (a0, a1), (l0, l1) = _sb_tile(kb, vb, qpad, [a0, a1], [l0, l1], None, TQ, TK)
        return a0, a1, l0, l1

    a0, a1, _, _ = lax.fori_loop(0, i, body, (accs[0], accs[1], laters[0], laters[1]))
    o_ref[...] = jnp.concatenate([a0.T, a1.T], axis=1).astype(bf16)


def sb_attention(kvb_perm, qtb, *, B, S, TQ=Q_BLOCK):
    NB = S // TQ
    return pl.pallas_call(
        functools.partial(_sb_kernel, TQ=TQ),
        grid=(B, SB_HEADS // 2, NB),
        in_specs=[
            pl.BlockSpec((None, S, LANES), lambda b, hp, i: (b, 0, hp)),
            pl.BlockSpec((None, S, LANES), lambda b, hp, i: (b, 0, 2 + hp)),
            pl.BlockSpec((None, LANES, TQ), lambda b, hp, i: (b, hp, i)),
        ],
        out_specs=pl.BlockSpec((None, TQ, LANES), lambda b, hp, i: (b, i, hp)),
        out_shape=jax.ShapeDtypeStruct((B, S, B_WIDTH), bf16),
        compiler_params=_cparams(("arbitrary", "arbitrary", "arbitrary")),
        name="sb_attention",
    )(kvb_perm, kvb_perm, qtb)


def _perm_rows(z, S, TK):
    B = z.shape[0]
    V = TK // SUBLANES
    return z.reshape(B, S // TK, SUBLANES, V, z.shape[-1]).swapaxes(2, 3).reshape(B, S, z.shape[-1])


def _swa_kernel(kp_ref, kc_ref, vp_ref, vc_ref, qt_ref, slope_ref, sink_ref, o_ref, *, TQ):
    i = pl.program_id(1)
    kk = jnp.concatenate([kp_ref[...], kc_ref[...]], axis=0)
    vv = jnp.concatenate([vp_ref[...], vc_ref[...]], axis=0)
    qt = qt_ref[...]
    r = lax.broadcasted_iota(i32, (2 * TQ, 2 * TQ), 0)
    ql = lax.broadcasted_iota(i32, (2 * TQ, 2 * TQ), 1) % TQ
    dist = TQ + ql - r
    key_pos = (i - 1) * TQ + r
    mask = (dist >= 0) & (dist < SWA_WINDOW) & (key_pos >= 0)
    distf = dist.astype(f32)
    zero = jnp.zeros((HEAD_DIM, 2 * TQ), bf16)
    heads = []
    for g in range(SWA_KV):
        q2 = jnp.concatenate([qt[(2 * g) * 64:(2 * g + 1) * 64], qt[(2 * g + 1) * 64:(2 * g + 2) * 64]], axis=1)
        qpad = jnp.concatenate([q2, zero], axis=0) if g == 0 else jnp.concatenate([zero, q2], axis=0)
        s = jnp.dot(kk, qpad, preferred_element_type=f32)
        s = jnp.where(mask, s - slope_ref[g:g + 1, :] * distf, NEG)
        sink = sink_ref[g:g + 1, :]
        m = jnp.maximum(jnp.max(s, axis=0, keepdims=True), sink)
        p = jnp.where(mask, jnp.exp(s - m), 0.0)
        den = jnp.sum(p, axis=0, keepdims=True) + jnp.exp(sink - m)
        p = (p / den).astype(bf16)
        o = lax.dot_general(vv, p, (((0,), (0,)), ((), ())), preferred_element_type=f32)
        og = o[64 * g:64 * g + 64]
        heads += [og[:, 0:TQ], og[:, TQ:2 * TQ]]
    o_ref[...] = jnp.concatenate(heads, axis=0).T.astype(bf16)


def swa_attention(kvc, qtc, slope_l, sink_l, *, B, S, TQ=Q_BLOCK):
    NB = S // TQ
    prev = lambda c: (lambda b, i: (b, jnp.maximum(i - 1, 0), c))
    cur = lambda c: (lambda b, i: (b, i, c))
    return pl.pallas_call(
        functools.partial(_swa_kernel, TQ=TQ),
        grid=(B, NB),
        in_specs=[
            pl.BlockSpec((None, TQ, LANES), prev(0)), pl.BlockSpec((None, TQ, LANES), cur(0)),
            pl.BlockSpec((None, TQ, LANES), prev(1)), pl.BlockSpec((None, TQ, LANES), cur(1)),
            pl.BlockSpec((None, C_WIDTH, TQ), lambda b, i: (b, 0, i)),
            pl.BlockSpec((SWA_KV, 2 * TQ), lambda b, i: (0, 0)),
            pl.BlockSpec((SWA_KV, 2 * TQ), lambda b, i: (0, 0)),
        ],
        out_specs=pl.BlockSpec((None, TQ, C_WIDTH), lambda b, i: (b, i, 0)),
        out_shape=jax.ShapeDtypeStruct((B, S, C_WIDTH), bf16),
        compiler_params=_cparams(("arbitrary", "arbitrary")),
        name="swa_attention",
    )(kvc, kvc, kvc, kvc, qtc, slope_l, sink_l)


def _nsa_compress_kernel(ch_ref, w1_ref, pe_ref, w2_ref, gain_ref, o_ref, *, NCP):
    w1 = w1_ref[...]
    p = jnp.dot(ch_ref[...], w1, preferred_element_type=f32)
    pb = jnp.dot(pe_ref[...], w1, preferred_element_type=f32)
    bias = pb[0:1, 0:512] + pb[1:2, 512:1024]
    hid = p[:, 0:512] + pltpu.roll(p[:, 512:1024], NCP - 1, axis=0) + bias
    y = jnp.dot(jax.nn.gelu(hid).astype(bf16), w2_ref[...], preferred_element_type=f32)
    is_k = lax.broadcasted_iota(i32, y.shape, 1) < HEAD_DIM
    ss = jnp.sum(jnp.where(is_k, y * y, 0.0), axis=-1, keepdims=True)
    yn = y * lax.rsqrt(ss * (1.0 / HEAD_DIM) + EPS) * gain_ref[...]
    o_ref[...] = jnp.where(is_k, yn, y).astype(bf16)


def nsa_compress(ch, w1big, pe2, w2big, gain, *, B, S):
    NCP = S // CMP_STRIDE
    W = CMP_STRIDE * LANES
    return pl.pallas_call(
        functools.partial(_nsa_compress_kernel, NCP=NCP),
        grid=(B,),
        in_specs=[
            pl.BlockSpec((None, NCP, W), lambda b: (b, 0, 0)),
            pl.BlockSpec((W, 4 * CMP_HID), lambda b: (0, 0)),
            pl.BlockSpec((SUBLANES, W), lambda b: (0, 0)),
            pl.BlockSpec((2 * CMP_HID, LANES), lambda b: (0, 0)),
            pl.BlockSpec((1, LANES), lambda b: (0, 0)),
        ],
        out_specs=pl.BlockSpec((None, NCP, LANES), lambda b: (b, 0, 0)),
        out_shape=jax.ShapeDtypeStruct((B, NCP, LANES), bf16),
        compiler_params=_cparams(("arbitrary",)),
        name="nsa_compress",
    )(ch, w1big, pe2, w2big, gain)


def _q_all_heads(qt, TQ):
    qr = jnp.concatenate([qt[h * HEAD_DIM:(h + 1) * HEAD_DIM] for h in range(NSA_HEADS)], axis=1)
    return jnp.concatenate([qr, jnp.zeros_like(qr)], axis=0)


def _nsa_select_kernel(qt_ref, kv_ref, slope_ref, ocmp_ref, sel_ref, flag_ref, s_scr, *, TQ, S):
    NSEL = S // SEL_BLOCK
    NC = S // CMP_STRIDE - 1
    R = SEL_BLOCK // CMP_STRIDE
    KSEL = min(SEL_TOPK, NSEL)
    HW = NSA_HEADS * TQ
    i = pl.program_id(1)
    qpad = _q_all_heads(qt_ref[...], TQ)
    slope = slope_ref[...]
    t = i * TQ + lax.broadcasted_iota(i32, (1, HW), 1) % TQ
    j_col = lax.broadcasted_iota(i32, (NSEL, 1), 0)
    m = jnp.full((1, HW), NEG, f32)
    for r in range(R):
        s = jnp.dot(kv_ref[r * NSEL:(r + 1) * NSEL, :], qpad, preferred_element_type=f32)
        c = R * j_col + r
        dist = t - (CMP_STRIDE * c + (CMP_LEN - 1))
        s = jnp.where((dist >= 0) & (c < NC), s - slope * dist.astype(f32), NEG)
        s_scr[r] = s
        m = jnp.maximum(m, jnp.max(s, axis=0, keepdims=True))
    l = jnp.zeros((1, HW), f32)
    for r in range(R):
        s = s_scr[r]
        p = jnp.where(s > 0.5 * NEG, jnp.exp(s - m), 0.0)
        s_scr[r] = p
        l = l + jnp.sum(p, axis=0, keepdims=True)
    inv = 1.0 / jnp.maximum(l, 1e-30)
    o = jnp.zeros((LANES, HW), f32)
    ps = []
    for r in range(R):
        pn = s_scr[r] * inv
        o = o + lax.dot_general(kv_ref[r * NSEL:(r + 1) * NSEL, :], pn.astype(bf16), (((0,), (0,)), ((), ())),
                                preferred_element_type=f32)
        acc = pn[:, 0:TQ]
        for h in range(1, NSA_HEADS):
            acc = acc + pn[:, h * TQ:(h + 1) * TQ]
        ps.append(acc)
    ocmp_ref[...] = o[HEAD_DIM:2 * HEAD_DIM].astype(ocmp_ref.dtype)
    jrow = lax.broadcasted_iota(i32, (NSEL, TQ), 0)
    prev3 = jnp.where(jrow == 0, 0.0, pltpu.roll(ps[3], 1, axis=0))
    imp = 0.5 * prev3 + ps[0] + ps[1] + ps[2] + 0.5 * ps[3]
    tq = i * TQ + lax.broadcasted_iota(i32, (1, TQ), 1)
    cur = tq // SEL_BLOCK
    forced = (jrow == 0) | (jrow == cur) | (jrow == cur - 1)
    causal = jrow <= cur
    score = jnp.where(causal, jnp.where(forced, BIG, imp), -1e38)
    jf = jrow.astype(f32)
    taken = jnp.zeros((NSEL, TQ), jnp.bool_)
    picks = []
    for _ in range(KSEL):
        mx = jnp.max(score, axis=0, keepdims=True)
        idx = jnp.min(jnp.where(score == mx, jf, float(NSEL)), axis=0, keepdims=True)
        hit = jf == idx
        taken = taken | hit
        score = jnp.where(hit, -jnp.inf, score)
        picks.append(idx)
    picks += [picks[-1]] * (SUBLANES - KSEL)
    sel_ref[...] = jnp.concatenate(picks, axis=0).astype(i32)
    used = jnp.where(taken & causal, 1.0, 0.0).astype(bf16)
    flag_ref[...] = lax.dot_general(jnp.ones((SUBLANES, TQ), bf16), used, (((1,), (1,)), ((), ())),
                                    preferred_element_type=f32)


def nsa_select(qta, kvcmp, slope_lane, *, B, S, TQ=Q_BLOCK):
    NB = S // TQ
    NCP = S // CMP_STRIDE
    NSEL = S // SEL_BLOCK
    HW = NSA_HEADS * TQ
    return pl.pallas_call(
        functools.partial(_nsa_select_kernel, TQ=TQ, S=S),
        grid=(B, NB),
        in_specs=[
            pl.BlockSpec((None, A_WIDTH, TQ), lambda b, i: (b, 0, i)),
            pl.BlockSpec((None, NCP, LANES), lambda b, i: (b, 0, 0)),
            pl.BlockSpec((1, HW), lambda b, i: (0, 0)),
        ],
        out_specs=(
            pl.BlockSpec((None, None, HEAD_DIM, HW), lambda b, i: (b, i, 0, 0)),
            pl.BlockSpec((None, SUBLANES, TQ), lambda b, i: (b, 0, i)),
            pl.BlockSpec((None, None, SUBLANES, NSEL), lambda b, i: (b, i, 0, 0)),
        ),
        out_shape=(
            jax.ShapeDtypeStruct((B, NB, HEAD_DIM, HW), bf16),
            jax.ShapeDtypeStruct((B, SUBLANES, S), i32),
            jax.ShapeDtypeStruct((B, NB, SUBLANES, NSEL), f32),
        ),
        scratch_shapes=[pltpu.VMEM((SEL_BLOCK // CMP_STRIDE, NSEL, HW), f32)],
        compiler_params=_cparams(("arbitrary", "arbitrary")),
        name="nsa_select",
    )(qta, kvcmp, slope_lane)


_SLC_CHUNK = 4


def _nsa_attend_kernel(cnt_ref, lst_ref, qt_ref, kslc_ref, kwin_ref, sel_ref, ocmp_ref, gt_ref, slope_ref,
                       o_ref, acc_scr, m_scr, l_scr, *, TQ, S):
    NB = S // TQ
    NSEL = S // SEL_BLOCK
    HW = NSA_HEADS * TQ
    b = pl.program_id(0)
    i = pl.program_id(1)
    count = cnt_ref[b * NB + i]
    qpad = _q_all_heads(qt_ref[...], TQ)
    slope = slope_ref[...]
    tq = i * TQ + lax.broadcasted_iota(i32, (1, HW), 1) % TQ
    sel = sel_ref[...]
    tn = (((0,), (0,)), ((), ()))

    acc_scr[...] = jnp.zeros_like(acc_scr)
    m_scr[...] = jnp.full_like(m_scr, NEG)
    l_scr[...] = jnp.zeros_like(l_scr)

    def chunk(c, carry):
        blks, masks, biases = [], [], []
        for b4 in range(_SLC_CHUNK):
            idx = _SLC_CHUNK * c + b4
            ok = idx < count
            j = jnp.where(ok, lst_ref[0, 0, jnp.minimum(idx, NSEL - 1)], 0)
            blks.append(kslc_ref[pl.ds(pl.multiple_of(j * SEL_BLOCK, SEL_BLOCK), SEL_BLOCK), :])
            picked = jnp.max(jnp.where(sel == j, 1.0, 0.0), axis=0, keepdims=True) > 0.0
            picked = jnp.concatenate([picked & ok] * NSA_HEADS, axis=1)
            dist = tq - (j * SEL_BLOCK + lax.broadcasted_iota(i32, (SEL_BLOCK, 1), 0))
            masks.append((dist >= 0) & picked)
            biases.append(slope * dist.astype(f32))
        kc = jnp.concatenate(blks, axis=0)
        s = jnp.dot(kc, qpad, preferred_element_type=f32)
        parts = [jnp.where(masks[b4], s[b4 * SEL_BLOCK:(b4 + 1) * SEL_BLOCK] - biases[b4], NEG)
                 for b4 in range(_SLC_CHUNK)]
        m_old = m_scr[...]
        m_new = m_old
        for part in parts:
            m_new = jnp.maximum(m_new, jnp.max(part, axis=0, keepdims=True))
        alpha = jnp.exp(m_old - m_new)
        ps = [jnp.where(masks[b4], jnp.exp(parts[b4] - m_new), 0.0) for b4 in range(_SLC_CHUNK)]
        lsum = ps[0].sum(axis=0, keepdims=True)
        for part in ps[1:]:
            lsum = lsum + part.sum(axis=0, keepdims=True)
        l_scr[...] = l_scr[...] * alpha + lsum
        pv = lax.dot_general(kc, jnp.concatenate(ps, axis=0).astype(bf16), tn, preferred_element_type=f32)
        acc_scr[...] = acc_scr[...] * alpha + pv[HEAD_DIM:2 * HEAD_DIM]
        m_scr[...] = m_new
        return carry

    lax.fori_loop(0, (count + _SLC_CHUNK - 1) // _SLC_CHUNK, chunk, 0)
    o_slc = acc_scr[...] / jnp.maximum(l_scr[...], 1e-30)

    NWB = NSA_WINDOW // TQ + 1
    blks, masks, biases = [], [], []
    for w in range(NWB):
        jb = i - (NWB - 1) + w
        blks.append(kwin_ref[pl.ds(pl.multiple_of(jnp.maximum(jb, 0) * TQ, TQ), TQ), :])
        dist = tq - (jb * TQ + lax.broadcasted_iota(i32, (TQ, 1), 0))
        masks.append((dist >= 0) & (dist < NSA_WINDOW) & (jb >= 0))
        biases.append(slope * dist.astype(f32))
    kw = jnp.concatenate(blks, axis=0)
    s = jnp.dot(kw, qpad, preferred_element_type=f32)
    parts = [jnp.where(masks[w], s[w * TQ:(w + 1) * TQ] - biases[w], NEG) for w in range(NWB)]
    m = parts[0].max(axis=0, keepdims=True)
    for part in parts[1:]:
        m = jnp.maximum(m, part.max(axis=0, keepdims=True))
    ps = [jnp.where(masks[w], jnp.exp(parts[w] - m), 0.0) for w in range(NWB)]
    l = ps[0].sum(axis=0, keepdims=True)
    for part in ps[1:]:
        l = l + part.sum(axis=0, keepdims=True)
    pn = (jnp.concatenate(ps, axis=0) * (1.0 / jnp.maximum(l, 1e-30))).astype(bf16)
    o_win = lax.dot_general(kw, pn, tn, preferred_element_type=f32)[HEAD_DIM:2 * HEAD_DIM]

    gt = gt_ref[...]
    gate = [jnp.concatenate([gt[br * NSA_HEADS + h:br * NSA_HEADS + h + 1] for h in range(NSA_HEADS)], axis=1)
            for br in range(3)]
    o = gate[0] * ocmp_ref[...].astype(f32) + gate[1] * o_slc + gate[2] * o_win
    o = jnp.concatenate([o[:, h * TQ:(h + 1) * TQ] for h in range(NSA_HEADS)], axis=0)
    o_ref[...] = o.T.astype(bf16)


def nsa_attend(counts, lists, qta, kva, selT, ocmp, gT, slope_lane, *, B, S, TQ=Q_BLOCK):
    NB = S // TQ
    NSEL = S // SEL_BLOCK
    HW = NSA_HEADS * TQ
    grid_spec = pltpu.PrefetchScalarGridSpec(
        num_scalar_prefetch=1,
        grid=(B, NB),
        in_specs=[
            pl.BlockSpec((1, 1, NSEL), lambda b, i, cnt: (b * NB + i, 0, 0), memory_space=pltpu.SMEM),
            pl.BlockSpec((None, A_WIDTH, TQ), lambda b, i, cnt: (b, 0, i)),
            pl.BlockSpec((None, S, LANES), lambda b, i, cnt: (b, 0, 1)),
            pl.BlockSpec((None, S, LANES), lambda b, i, cnt: (b, 0, 2)),
            pl.BlockSpec((None, SUBLANES, TQ), lambda b, i, cnt: (b, 0, i)),
            pl.BlockSpec((None, None, HEAD_DIM, HW), lambda b, i, cnt: (b, i, 0, 0)),
            pl.BlockSpec((None, 32, TQ), lambda b, i, cnt: (b, 0, i)),
            pl.BlockSpec((1, HW), lambda b, i, cnt: (0, 0)),
        ],
        out_specs=pl.BlockSpec((None, TQ, A_WIDTH), lambda b, i, cnt: (b, i, 0)),
        scratch_shapes=[pltpu.VMEM((HEAD_DIM, HW), f32), pltpu.VMEM((1, HW), f32), pltpu.VMEM((1, HW), f32)],
    )
    return pl.pallas_call(
        functools.partial(_nsa_attend_kernel, TQ=TQ, S=S),
        grid_spec=grid_spec,
        out_shape=jax.ShapeDtypeStruct((B, S, A_WIDTH), bf16),
        compiler_params=_cparams(("arbitrary", "arbitrary")),
        name="nsa_attend",
    )(counts, lists, qta, kva, kva, selT, ocmp, gT, slope_lane)


def nsa_branch(qta, kva, gT, cmp_w, slope_lane, *, B, S):
    NB = S // Q_BLOCK
    NSEL = S // SEL_BLOCK
    NCP = S // CMP_STRIDE
    R = SEL_BLOCK // CMP_STRIDE
    kva = kva.reshape(B, S, 6 * HEAD_DIM)
    ch = kva[:, :, 0:LANES].reshape(B, NCP, CMP_STRIDE * LANES)
    kvcmp = nsa_compress(ch, *cmp_w, B=B, S=S)
    kvcmp = kvcmp.reshape(B, NSEL, R, LANES).swapaxes(1, 2).reshape(B, NCP, LANES)
    ocmp, selT, flags = nsa_select(qta, kvcmp, slope_lane, B=B, S=S)
    used = flags[:, :, 0, :] > 0.5
    ar = jnp.arange(NSEL, dtype=i32)
    lists = jnp.minimum(jnp.sort(jnp.where(used, ar, ar + NSEL), axis=-1), NSEL - 1)
    counts = jnp.sum(used, axis=-1).astype(i32)
    return nsa_attend(counts.reshape(B * NB), lists.reshape(B * NB, 1, NSEL), qta, kva, selT, ocmp, gT, slope_lane,
                      B=B, S=S)


def _route(logit, rb):
    sc = jax.nn.sigmoid(logit)
    sel = sc + rb
    srow = [sel[e:e + 1] for e in range(N_EXPERTS)]
    crow = [sc[e:e + 1] for e in range(N_EXPERTS)]
    gscore = []
    for g in range(N_GROUPS):
        a, b, c, d = srow[4 * g:4 * g + 4]
        top2 = jnp.maximum(jnp.maximum(jnp.maximum(a + b, a + c), jnp.maximum(a + d, b + c)),
                           jnp.maximum(b + d, c + d))
        gscore.append(top2)
    best, gi = gscore[0], jnp.zeros_like(gscore[0], dtype=i32)
    for g in range(1, N_GROUPS):
        better = gscore[g] > best
        gi = jnp.where(better, g, gi)
        best = jnp.where(better, gscore[g], best)

    def pick_group(rows, k):
        v = rows[k]
        for g in range(1, N_GROUPS):
            v = jnp.where(gi == g, rows[4 * g + k], v)
        return v

    iv = [pick_group(srow, k) for k in range(EXPERTS_PER_GROUP)]
    ic = [pick_group(crow, k) for k in range(EXPERTS_PER_GROUP)]
    b1, i1, w1 = iv[0], jnp.zeros_like(gi), ic[0]
    for k in range(1, EXPERTS_PER_GROUP):
        better = iv[k] > b1
        i1 = jnp.where(better, k, i1)
        w1 = jnp.where(better, ic[k], w1)
        b1 = jnp.where(better, iv[k], b1)
    b2 = jnp.full_like(b1, -jnp.inf)
    i2, w2 = jnp.zeros_like(gi), jnp.zeros_like(w1)
    for k in range(EXPERTS_PER_GROUP):
        better = (i1 != k) & (iv[k] > b2)
        i2 = jnp.where(better, k, i2)
        w2 = jnp.where(better, ic[k], w2)
        b2 = jnp.where(better, iv[k], b2)
    tot = w1 + w2
    eidx = jnp.concatenate([gi * EXPERTS_PER_GROUP + i1, gi * EXPERTS_PER_GROUP + i2], axis=0)
    ew = jnp.concatenate([w1 / tot, w2 / tot], axis=0)
    return eidx, ew


def _mix_out_kernel(x_ref, a_ref, sh_ref, ga_ref, oa_ref, ob_ref, oc_ref, wm_ref, wa_ref, wb_ref, wc_ref, wo_ref,
                    am_ref, shm_ref, wr_ref, rb_ref, xo_ref, h2_ref, eidx_ref, ew_ref):
    x = x_ref[...]
    h = _norm_mod(x, a_ref[...], sh_ref[...]).astype(bf16)
    gates = jax.nn.sigmoid(jnp.dot(h, wm_ref[...], preferred_element_type=f32))
    D = D_MODEL
    y = gates[:, 0:D] * jnp.dot(oa_ref[...], wa_ref[...], preferred_element_type=f32)
    y = y + gates[:, D:2 * D] * jnp.dot(ob_ref[...], wb_ref[...], preferred_element_type=f32)
    y = y + gates[:, 2 * D:3 * D] * jnp.dot(oc_ref[...], wc_ref[...], preferred_element_type=f32)
    xn = x + ga_ref[...] * jnp.dot(y.astype(bf16), wo_ref[...], preferred_element_type=f32)
    xo_ref[...] = xn
    h2 = _norm_mod(xn, am_ref[...], shm_ref[...])
    hi = h2.astype(bf16)
    h2_ref[...] = hi
    lo = (h2 - hi.astype(f32)).astype(bf16)
    wr = wr_ref[...]
    whi = wr.astype(bf16)
    wlo = (wr - whi.astype(f32)).astype(bf16)
    nt = lambda p, q: lax.dot_general(p, q, (((1,), (1,)), ((), ())), preferred_element_type=f32)
    logit = nt(whi, hi) + nt(whi, lo) + nt(wlo, hi)
    eidx, ew = _route(logit, rb_ref[...])
    eidx_ref[...] = eidx
    ew_ref[...] = ew


def mix_out(x2, a_mod, sh_mod, g_a, oa, ob, oc, wm, wa, wb, wc, wo, am_moe, shm_moe, wrT, rb, *, B, S, tm=256):
    N, D = x2.shape
    tpb = S // tm
    row = lambda i: (i, 0)
    bmap = lambda i: (i // tpb, 0, 0)
    full = lambda i: (0, 0)
    col = lambda i: (0, i)
    return pl.pallas_call(
        _mix_out_kernel,
        grid=(N // tm,),
        in_specs=[
            pl.BlockSpec((tm, D), row),
            pl.BlockSpec((None, 1, D), bmap), pl.BlockSpec((None, 1, D), bmap), pl.BlockSpec((None, 1, D), bmap),
            pl.BlockSpec((tm, A_WIDTH), row), pl.BlockSpec((tm, B_WIDTH), row), pl.BlockSpec((tm, C_WIDTH), row),
            pl.BlockSpec((D, 3 * D), full), pl.BlockSpec((A_WIDTH, D), full), pl.BlockSpec((B_WIDTH, D), full),
            pl.BlockSpec((C_WIDTH, D), full), pl.BlockSpec((D, D), full),
            pl.BlockSpec((None, 1, D), bmap), pl.BlockSpec((None, 1, D), bmap),
            pl.BlockSpec((N_EXPERTS, D), full), pl.BlockSpec((N_EXPERTS, 1), full),
        ],
        out_specs=(pl.BlockSpec((tm, D), row), pl.BlockSpec((tm, D), row),
                   pl.BlockSpec((TOP_K, tm), col), pl.BlockSpec((TOP_K, tm), col)),
        out_shape=(jax.ShapeDtypeStruct((N, D), f32), jax.ShapeDtypeStruct((N, D), bf16),
                   jax.ShapeDtypeStruct((TOP_K, N), i32), jax.ShapeDtypeStruct((TOP_K, N), f32)),
        compiler_params=_cparams(("arbitrary",)),
        name="mix_out",
    )(x2, a_mod, sh_mod, g_a, oa, ob, oc, wm, wa, wb, wc, wo, am_moe, shm_moe, wrT, rb)


def _moe_ffn_kernel(be_ref, xs_ref, wg_ref, wu_ref, wd_ref, sw_ref, y_ref):
    xs = xs_ref[...]
    g = jnp.dot(xs, wg_ref[...], preferred_element_type=f32)
    u = jnp.dot(xs, wu_ref[...], preferred_element_type=f32)
    a = (jax.nn.silu(g) * u).astype(bf16)
    y_ref[...] = (jnp.dot(a, wd_ref[...], preferred_element_type=f32) * sw_ref[...]).astype(y_ref.dtype)


def moe_ffn(blk_e, xs, wg, wu, wd, slot_w):
    n_rows, D = xs.shape
    n_blk = n_rows // MOE_BLOCK
    grid_spec = pltpu.PrefetchScalarGridSpec(
        num_scalar_prefetch=1,
        grid=(n_blk,),
        in_specs=[
            pl.BlockSpec((MOE_BLOCK, D), lambda i, be: (i, 0)),
            pl.BlockSpec((None, D, D_EXPERT), lambda i, be: (be[i], 0, 0)),
            pl.BlockSpec((None, D, D_EXPERT), lambda i, be: (be[i], 0, 0)),
            pl.BlockSpec((None, D_EXPERT, D), lambda i, be: (be[i], 0, 0)),
            pl.BlockSpec((MOE_BLOCK, 1), lambda i, be: (i, 0)),
        ],
        out_specs=pl.BlockSpec((MOE_BLOCK, D), lambda i, be: (i, 0)),
    )
    return pl.pallas_call(
        _moe_ffn_kernel,
        grid_spec=grid_spec,
        out_shape=jax.ShapeDtypeStruct((n_rows, D), bf16),
        compiler_params=_cparams(("arbitrary",)),
        name="moe_ffn",
    )(blk_e, xs, wg, wu, wd, slot_w)


def moe_dispatch(eidx, ew, N):
    NK = N * TOP_K
    n_blk = -(-NK // MOE_BLOCK) + N_EXPERTS
    flat_e = eidx.T.reshape(-1)
    flat_w = ew.T.reshape(-1)
    order = jnp.argsort(flat_e)
    e_sorted = flat_e[order]
    counts = jnp.bincount(flat_e, length=N_EXPERTS)
    padded = (counts + MOE_BLOCK - 1) // MOE_BLOCK * MOE_BLOCK
    p_end = jnp.cumsum(padded)
    p_start = p_end - padded
    start = jnp.cumsum(counts) - counts
    dest = (p_start[e_sorted] + jnp.arange(NK) - start[e_sorted]).astype(i32)
    slot_tok = jnp.full((n_blk * MOE_BLOCK,), N, i32).at[dest].set((order // TOP_K).astype(i32))
    slot_w = jnp.zeros((n_blk * MOE_BLOCK,), f32).at[dest].set(flat_w[order])
    blk_e = jnp.minimum(jnp.searchsorted(p_end, jnp.arange(n_blk) * MOE_BLOCK, side='right'), N_EXPERTS - 1).astype(i32)
    pos = jnp.zeros((NK,), i32).at[order].set(dest).reshape(N, TOP_K)
    return slot_tok, slot_w, blk_e, pos


def _prep_cmp_weights(pe_k, w1_k, w2_k, pe_v, w1_v, w2_v, k_gain0):
    T = CMP_STRIDE
    w1big = jnp.zeros((T, 2, HEAD_DIM, 4, CMP_HID), f32)
    w1k = w1_k.reshape(2, T, HEAD_DIM, CMP_HID)
    w1v = w1_v.reshape(2, T, HEAD_DIM, CMP_HID)
    w1big = w1big.at[:, 0, :, 0].set(w1k[0]).at[:, 1, :, 1].set(w1v[0])
    w1big = w1big.at[:, 0, :, 2].set(w1k[1]).at[:, 1, :, 3].set(w1v[1])
    w1big = w1big.reshape(T * LANES, 4 * CMP_HID).astype(bf16)
    pe = jnp.stack([pe_k.reshape(2, T, HEAD_DIM), pe_v.reshape(2, T, HEAD_DIM)], axis=2)
    pe2 = jnp.zeros((SUBLANES, T * LANES), f32).at[0:2].set(pe.reshape(2, T * LANES)).astype(bf16)
    w2big = jnp.zeros((2 * CMP_HID, LANES), f32)
    w2big = w2big.at[0:CMP_HID, 0:HEAD_DIM].set(w2_k).at[CMP_HID:, HEAD_DIM:].set(w2_v).astype(bf16)
    gain = jnp.concatenate([k_gain0, jnp.ones((HEAD_DIM,), f32)])[None, :]
    return w1big, pe2, w2big, gain


def _prep_proj_weights(w_in, nsa_q_gain, nsa_k_gain, swa_q_gain, swa_k_gain):
    w = w_in
    wtok = jnp.concatenate([w[:, _O_KVA:_O_GA], w[:, _O_KB:_O_QC], w[:, _O_KC:_O_MG]], axis=1).astype(bf16)
    gate_perm = np.array([h * 3 + br for br in range(3) for h in range(NSA_HEADS)])
    scale = HEAD_DIM ** -0.5
    wch = jnp.concatenate([w[:, _O_QA:_O_KVA], w[:, _O_QB:_O_KB] * scale, w[:, _O_QC:_O_KC],
                           w[:, _O_GA:_O_QB][:, gate_perm], jnp.zeros((D_MODEL, 8), f32)], axis=1).T.astype(bf16)
    gtok = jnp.zeros((1, TOK_W), f32)
    gtok = gtok.at[0, 128:192].set(nsa_k_gain[1]).at[0, 256:320].set(nsa_k_gain[2])
    gtok = gtok.at[0, 896:960].set(swa_k_gain).at[0, 960:1024].set(swa_k_gain)
    mtok = jnp.zeros((1, TOK_W), f32).at[0, 128:192].set(1.0).at[0, 256:320].set(1.0).at[0, 896:1024].set(1.0)
    gcha = (jnp.tile(nsa_q_gain, NSA_HEADS) * scale)[:, None]
    gchc = (jnp.tile(swa_q_gain, SWA_HEADS) * scale)[:, None]
    blk = np.arange(LANES) // HEAD_DIM
    bd = jnp.asarray(blk[:, None] == blk[None, :], bf16)
    return wtok, wch, gtok, mtok, gcha, gchc, bd


def _adaln_kernel(c_ref, w_ref, b_ref, o_ref):
    cond = jax.nn.silu(c_ref[...]).astype(bf16)
    o_ref[...] = jnp.dot(cond, w_ref[...].astype(bf16), preferred_element_type=f32) + b_ref[...]


def adaln(c_pad, w_ada, b_ada):
    L, D, D6 = w_ada.shape
    return pl.pallas_call(
        _adaln_kernel,
        grid=(L, D6 // D),
        in_specs=[
            pl.BlockSpec((SUBLANES, D), lambda l, j: (0, 0)),
            pl.BlockSpec((None, D, D), lambda l, j: (l, 0, j)),
            pl.BlockSpec((None, 1, D), lambda l, j: (l, 0, j)),
        ],
        out_specs=pl.BlockSpec((None, SUBLANES, D), lambda l, j: (l, 0, j)),
        out_shape=jax.ShapeDtypeStruct((L, SUBLANES, D6), f32),
        compiler_params=_cparams(("arbitrary", "arbitrary")),
        name="adaln",
    )(c_pad, w_ada, b_ada.reshape(L, 1, D6))


def _moe_combine_kernel(x_ref, y0_ref, y1_ref, gm_ref, o_ref):
    o_ref[...] = x_ref[...] + gm_ref[...] * (y0_ref[...].astype(f32) + y1_ref[...].astype(f32))


def moe_combine(x2, y0, y1, g_m, *, B, S, tm=512):
    N, D = x2.shape
    tpb = S // tm
    row = lambda i: (i, 0)
    return pl.pallas_call(
        _moe_combine_kernel,
        grid=(N // tm,),
        in_specs=[pl.BlockSpec((tm, D), row), pl.BlockSpec((tm, D), row), pl.BlockSpec((tm, D), row),
                  pl.BlockSpec((None, 1, D), lambda i: (i // tpb, 0, 0))],
        out_specs=pl.BlockSpec((tm, D), row),
        out_shape=jax.ShapeDtypeStruct((N, D), f32),
        compiler_params=_cparams(("arbitrary",)),
        name="moe_combine",
    )(x2, y0, y1, g_m)


def kernel(x, c, w_ada, b_ada, g_norm_mix, g_norm_moe, w_in, cmp_pe_k, cmp_w1_k, cmp_w2_k, cmp_pe_v, cmp_w1_v,
           cmp_w2_v, nsa_q_gain, nsa_k_gain, swa_q_gain, swa_k_gain, swa_sinks, w_branch_a, w_branch_b,
           w_branch_c, w_out, w_router, router_bias, w_exp_gate, w_exp_up, w_exp_down):
    B, S, D = x.shape
    N = B * S
    L = w_ada.shape[0]
    slopes = 2.0 ** (-8.0 * jnp.arange(1, SWA_HEADS + NSA_HEADS + 1, dtype=f32) / (SWA_HEADS + NSA_HEADS))
    swa_slope_l = jnp.repeat(slopes[:SWA_HEADS].reshape(SWA_KV, SWA_HEADS // SWA_KV), Q_BLOCK, axis=1)
    nsa_slope_l = jnp.repeat(slopes[SWA_HEADS:], Q_BLOCK)[None, :]
    c_pad = jnp.zeros((SUBLANES, D), f32).at[:B].set(c)
    mod_all = adaln(c_pad, w_ada, b_ada)[:, :B]
    wrT = w_router.T
    rb = router_bias[:, None]
    x2 = x.reshape(N, D)
    for l in range(L):
        sh_a, sc_a, g_a, sh_m, sc_m, g_m = [m[:, None, :] for m in jnp.split(mod_all[l], 6, axis=-1)]
        a_mix = g_norm_mix[l][None, None, :] * (1.0 + sc_a)
        a_moe = g_norm_moe[l][None, None, :] * (1.0 + sc_m)
        pw = _prep_proj_weights(w_in[l], nsa_q_gain[l], nsa_k_gain[l], swa_q_gain[l], swa_k_gain[l])
        kva, kvb, kvc, qta, qtb, qtc, gT = proj_in(x2, a_mix, sh_a, *pw, B=B, S=S)
        cw = _prep_cmp_weights(cmp_pe_k[l], cmp_w1_k[l], cmp_w2_k[l], cmp_pe_v[l], cmp_w1_v[l], cmp_w2_v[l],
                               nsa_k_gain[l][0])
        o_a = nsa_branch(qta, kva, gT, cw, nsa_slope_l, B=B, S=S)
        o_b = sb_attention(_perm_rows(kvb.reshape(B, S, 2 * B_WIDTH), S, Q_BLOCK), qtb, B=B, S=S)
        sink_l = jnp.repeat(swa_sinks[l].reshape(SWA_KV, SWA_HEADS // SWA_KV), Q_BLOCK, axis=1)
        o_c = swa_attention(kvc.reshape(B, S, 2 * SWA_KV * HEAD_DIM), qtc, swa_slope_l, sink_l, B=B, S=S)
        x2, h2, eidx, ew = mix_out(
            x2, a_mix, sh_a, g_a, o_a.reshape(N, A_WIDTH), o_b.reshape(N, B_WIDTH), o_c.reshape(N, C_WIDTH),
            w_in[l][:, _O_MG:_O_END].astype(bf16), w_branch_a[l].astype(bf16), w_branch_b[l].astype(bf16),
            w_branch_c[l].astype(bf16), w_out[l].astype(bf16), a_moe, sh_m, wrT, rb, B=B, S=S)
        slot_tok, slot_w, blk_e, pos = moe_dispatch(eidx, ew, N)
        xs = jnp.concatenate([h2, jnp.zeros((1, D), bf16)], axis=0)[slot_tok]
        y = moe_ffn(blk_e, xs, w_exp_gate[l].astype(bf16), w_exp_up[l].astype(bf16), w_exp_down[l].astype(bf16),
                    slot_w[:, None])
        x2 = moe_combine(x2, y[pos[:, 0]], y[pos[:, 1]], g_m, B=B, S=S)
    return x2.reshape(B, S, D)
```

```python
import functools

import numpy as np
import jax
import jax.numpy as jnp
from jax import lax
from jax.experimental import pallas as pl
from jax.experimental.pallas import tpu as pltpu

f32 = jnp.float32
bf16 = jnp.bfloat16
i32 = jnp.int32

D_MODEL = 1024
HEAD_DIM = 64
Q_BLOCK = 128
NSA_HEADS = 8
CMP_STRIDE = 16
CMP_LEN = 32
CMP_HID = 256
SEL_BLOCK = 64
SEL_TOPK = 8
NSA_WINDOW = 512
SB_HEADS = 4
SWA_HEADS = 4
SWA_KV = 2
SWA_WINDOW = 128
N_EXPERTS = 16
N_GROUPS = 4
EXPERTS_PER_GROUP = 4
TOP_K = 2
D_EXPERT = 512
MOE_BLOCK = 512
EPS = 1e-6
NEG = -1e30
BIG = 1e30

A_WIDTH = NSA_HEADS * HEAD_DIM
B_WIDTH = SB_HEADS * HEAD_DIM
C_WIDTH = SWA_HEADS * HEAD_DIM

LANES = 128
SUBLANES = 8
VMEM_LIMIT = 56 * 1024 * 1024

_O_QA = 0
_O_KVA = _O_QA + A_WIDTH
_O_GA = _O_KVA + 6 * HEAD_DIM
_O_QB = _O_GA + 3 * NSA_HEADS
_O_KB = _O_QB + B_WIDTH
_O_VB = _O_KB + B_WIDTH
_O_QC = _O_VB + B_WIDTH
_O_KC = _O_QC + C_WIDTH
_O_VC = _O_KC + SWA_KV * HEAD_DIM
_O_MG = _O_VC + SWA_KV * HEAD_DIM
_O_END = _O_MG + 3 * D_MODEL

TOK_W = 6 * HEAD_DIM + B_WIDTH + 2 * SWA_KV * HEAD_DIM
CH_W = A_WIDTH + B_WIDTH + C_WIDTH + B_WIDTH + 32
_TOK_NORM_CHUNKS = (1, 2, 5)


def _cparams(sem):
    return pltpu.CompilerParams(dimension_semantics=sem, vmem_limit_bytes=VMEM_LIMIT)


def _split_dot(x, w):
    hi = x.astype(bf16)
    lo = (x - hi.astype(f32)).astype(bf16)
    return jnp.dot(hi, w, preferred_element_type=f32) + jnp.dot(lo, w, preferred_element_type=f32)


def _norm_mod(x, a, sh):
    rs = lax.rsqrt(jnp.mean(x * x, axis=-1, keepdims=True) + EPS)
    return x * rs * a + sh


def _group_norm_ch(y, gain_col, n, tm):
    y3 = y.reshape(n, HEAD_DIM, tm)
    ss = jnp.sum(y3 * y3, axis=1, keepdims=True)
    yn = y3 * lax.rsqrt(ss * (1.0 / HEAD_DIM) + EPS)
    return yn.reshape(n * HEAD_DIM, tm) * gain_col


def _proj_in_kernel(x_ref, a_ref, sh_ref, wtok_ref, wch_ref, gtok_ref, mtok_ref, gcha_ref, gchc_ref, bd_ref,
                    kva_ref, kb_ref, kvc_ref, qta_ref, qtb_ref, qtc_ref, vtb_ref, gt_ref, *, tm):
    h = _norm_mod(x_ref[...], a_ref[...], sh_ref[...]).astype(bf16)
    t = jnp.dot(h, wtok_ref[...], preferred_element_type=f32)
    cols = [t[:, c * LANES:(c + 1) * LANES] for c in range(TOK_W // LANES)]
    for c in _TOK_NORM_CHUNKS:
        y = cols[c]
        ss = _split_dot(y * y, bd_ref[...])
        yn = y * lax.rsqrt(ss * (1.0 / HEAD_DIM) + EPS) * gtok_ref[:, c * LANES:(c + 1) * LANES]
        cols[c] = jnp.where(mtok_ref[:, c * LANES:(c + 1) * LANES] > 0.0, yn, y)
    t = jnp.concatenate(cols, axis=1).astype(bf16)
    kva_ref[...] = t[:, 0:384]
    kb_ref[...] = t[:, 384:640]
    kvc_ref[...] = t[:, 640:896]
    c = lax.dot_general(wch_ref[...], h, (((1,), (1,)), ((), ())), preferred_element_type=f32)
    qta_ref[...] = _group_norm_ch(c[0:512], gcha_ref[...], NSA_HEADS, tm).astype(bf16)
    qtb_ref[...] = c[512:768].astype(bf16)
    qtc_ref[...] = _group_norm_ch(c[768:1024], gchc_ref[...], SWA_HEADS, tm).astype(bf16)
    vtb_ref[...] = c[1024:1280].astype(bf16)
    gt_ref[...] = jax.nn.sigmoid(c[1280:1312])


def proj_in(x2, a_mod, sh_mod, wtok, wch, gtok, mtok, gcha, gchc, bd, *, B, S, tm=512):
    N, D = x2.shape
    tpb = S // tm
    row = lambda i: (i, 0)
    bmap = lambda i: (i // tpb, 0, 0)
    cmap = lambda i: (i // tpb, 0, i % tpb)
    full = lambda i: (0, 0)
    out_shape = (
        jax.ShapeDtypeStruct((N, 384), bf16), jax.ShapeDtypeStruct((N, 256), bf16), jax.ShapeDtypeStruct((N, 256), bf16),
        jax.ShapeDtypeStruct((B, 512, S), bf16), jax.ShapeDtypeStruct((B, 256, S), bf16),
        jax.ShapeDtypeStruct((B, 256, S), bf16), jax.ShapeDtypeStruct((B, 256, S), bf16),
        jax.ShapeDtypeStruct((B, 32, S), f32),
    )
    return pl.pallas_call(
        functools.partial(_proj_in_kernel, tm=tm),
        grid=(N // tm,),
        in_specs=[
            pl.BlockSpec((tm, D), row),
            pl.BlockSpec((None, 1, D), bmap), pl.BlockSpec((None, 1, D), bmap),
            pl.BlockSpec((D, TOK_W), full), pl.BlockSpec((CH_W, D), full),
            pl.BlockSpec((1, TOK_W), full), pl.BlockSpec((1, TOK_W), full),
            pl.BlockSpec((512, 1), full), pl.BlockSpec((256, 1), full),
            pl.BlockSpec((LANES, LANES), full),
        ],
        out_specs=(
            pl.BlockSpec((tm, 384), row), pl.BlockSpec((tm, 256), row), pl.BlockSpec((tm, 256), row),
            pl.BlockSpec((None, 512, tm), cmap), pl.BlockSpec((None, 256, tm), cmap),
            pl.BlockSpec((None, 256, tm), cmap), pl.BlockSpec((None, 256, tm), cmap),
            pl.BlockSpec((None, 32, tm), cmap),
        ),
        out_shape=out_shape,
        compiler_params=_cparams(("arbitrary",)),
        name="proj_in",
    )(x2, a_mod, sh_mod, wtok, wch, gtok, mtok, gcha, gchc, bd)


def _excl_suffix_prod8(t, rows):
    def shift_up(x, k):
        return jnp.where(rows < SUBLANES - k, pltpu.roll(x, SUBLANES - k, axis=0), 1.0)
    x = shift_up(t, 1)
    x = x * shift_up(x, 1)
    x = x * shift_up(x, 2)
    x = x * shift_up(x, 4)
    return x


SB_TK = 128
SB_TQ = 512


def _sb_scores(k_ref, qpad, z_scr, j, slot):
    kb = k_ref[pl.ds(pl.multiple_of(j * SB_TK, SB_TK), SB_TK), :]
    for h in range(2):
        z_scr[slot, h] = jnp.dot(kb, qpad[h], preferred_element_type=f32)


def _sb_weights(z_scr, p_scr, a_scr, slot, laters, mask_off, TQ):
    V = SB_TK // SUBLANES
    rows = lax.broadcasted_iota(i32, (SUBLANES, TQ), 0)
    qidx = lax.broadcasted_iota(i32, (SUBLANES, TQ), 1)
    new_laters = []
    for h in range(2):
        run = jnp.ones((SUBLANES, TQ), f32)
        for v in reversed(range(V)):
            sl = slice(v * SUBLANES, (v + 1) * SUBLANES)
            beta = 0.5 - 0.5 * jnp.tanh(z_scr[slot, h, sl, :])
            if mask_off is not None:
                beta = jnp.where(rows * V + (v + mask_off) < qidx, beta, 1.0)
            nxt = run * beta
            p_scr[h, sl, :] = run - nxt
            run = nxt
        g = _excl_suffix_prod8(run, rows) * laters[h]
        for v2 in range(V // 2):
            sl = slice(2 * v2 * SUBLANES, (2 * v2 + 2) * SUBLANES)
            a_scr[slot, h, sl, :] = (p_scr[h, sl, :] * jnp.concatenate([g, g], axis=0)).astype(bf16)
        new_laters.append((g * run)[0:1, :])
    return new_laters


def _sb_accumulate(vt_ref, a_scr, acc_scr, j, slot):
    vtb = vt_ref[:, pl.ds(pl.multiple_of(j * SB_TK, SB_TK), SB_TK)]
    for h in range(2):
        acc_scr[h] += jnp.dot(vtb[h * HEAD_DIM:(h + 1) * HEAD_DIM], a_scr[slot, h], preferred_element_type=f32)


def _sb_kernel(k_ref, vt_ref, qt_ref, o_ref, acc_scr, z_scr, p_scr, a_scr, qpad, *, TQ):
    NBQ = TQ // SB_TK
    i = pl.program_id(2)
    qpad[...] = jnp.zeros_like(qpad)
    qpad[0, 0:HEAD_DIM, :] = qt_ref[0:HEAD_DIM, :]
    qpad[1, HEAD_DIM:2 * HEAD_DIM, :] = qt_ref[HEAD_DIM:2 * HEAD_DIM, :]
    acc_scr[...] = jnp.zeros_like(acc_scr)
    a_scr[1] = jnp.zeros_like(a_scr[1])
    j0 = i * NBQ + NBQ - 1

    def step(j, laters, mask_offs):
        _sb_scores(k_ref, qpad, z_scr, j - 1, 1)
        _sb_accumulate(vt_ref, a_scr, acc_scr, jnp.minimum(j + 1, j0), 1)
        laters = _sb_weights(z_scr, p_scr, a_scr, 0, laters, mask_offs[0], TQ)
        _sb_scores(k_ref, qpad, z_scr, jnp.maximum(j - 2, 0), 0)
        _sb_accumulate(vt_ref, a_scr, acc_scr, j, 0)
        laters = _sb_weights(z_scr, p_scr, a_scr, 1, laters, mask_offs[1], TQ)
        return laters

    _sb_scores(k_ref, qpad, z_scr, j0, 0)
    laters = [jnp.ones((1, TQ), f32)] * 2
    for d in range(NBQ // 2):
        jd = NBQ - 1 - 2 * d
        laters = step(i * NBQ + jd, laters, (jd * SB_TK, (jd - 1) * SB_TK))

    def body(t, carry):
        return tuple(step(i * NBQ - 1 - 2 * t, list(carry), (None, None)))

    lax.fori_loop(0, i * (NBQ // 2), body, tuple(laters))
    _sb_accumulate(vt_ref, a_scr, acc_scr, 0, 1)
    o_ref[...] = jnp.concatenate([acc_scr[0].T, acc_scr[1].T], axis=1).astype(bf16)


def sb_attention(kb_perm, vtb_perm, qtb, *, B, S, TQ=SB_TQ):
    TQ = min(TQ, S)
    return pl.pallas_call(
        functools.partial(_sb_kernel, TQ=TQ),
        grid=(B, SB_HEADS // 2, S // TQ),
        in_specs=[
            pl.BlockSpec((None, S, LANES), lambda b, hp, i: (b, 0, hp)),
            pl.BlockSpec((None, LANES, S), lambda b, hp, i: (b, hp, 0)),
            pl.BlockSpec((None, LANES, TQ), lambda b, hp, i: (b, hp, i)),
        ],
        out_specs=pl.BlockSpec((None, TQ, LANES), lambda b, hp, i: (b, i, hp)),
        out_shape=jax.ShapeDtypeStruct((B, S, B_WIDTH), bf16),
        scratch_shapes=[pltpu.VMEM((2, HEAD_DIM, TQ), f32), pltpu.VMEM((2, 2, SB_TK, TQ), f32),
                        pltpu.VMEM((2, SB_TK, TQ), f32), pltpu.VMEM((2, 2, SB_TK, TQ), bf16),
                        pltpu.VMEM((2, 2 * HEAD_DIM, TQ), bf16)],
        compiler_params=_cparams(("arbitrary", "arbitrary", "arbitrary")),
        name="sb_attention",
    )(kb_perm, vtb_perm, qtb)


def _perm_rows(z, S):
    B = z.shape[0]
    V = SB_TK // SUBLANES
    return z.reshape(B, S // SB_TK, SUBLANES, V, z.shape[-1]).swapaxes(2, 3).reshape(B, S, z.shape[-1])


def _perm_lanes(z, S):
    B, C = z.shape[0], z.shape[1]
    V = SB_TK // SUBLANES
    return z.reshape(B, C, S // SB_TK, SUBLANES, V).swapaxes(3, 4).reshape(B, C, S)


def _swa_kernel(kp_ref, kc_ref, vp_ref, vc_ref, qt_ref, slope_ref, sink_ref, o_ref, *, TQ):
    i = pl.program_id(1)
    kk = jnp.concatenate([kp_ref[...], kc_ref[...]], axis=0)
    vv = jnp.concatenate([vp_ref[...], vc_ref[...]], axis=0)
    qt = qt_ref[...]
    r = lax.broadcasted_iota(i32, (2 * TQ, 2 * TQ), 0)
    ql = lax.broadcasted_iota(i32, (2 * TQ, 2 * TQ), 1) % TQ
    dist = TQ + ql - r
    key_pos = (i - 1) * TQ + r
    mask = (dist >= 0) & (dist < SWA_WINDOW) & (key_pos >= 0)
    distf = dist.astype(f32)
    zero = jnp.zeros((HEAD_DIM, 2 * TQ), bf16)
    heads = []
    for g in range(SWA_KV):
        q2 = jnp.concatenate([qt[(2 * g) * 64:(2 * g + 1) * 64], qt[(2 * g + 1) * 64:(2 * g + 2) * 64]], axis=1)
        qpad = jnp.concatenate([q2, zero], axis=0) if g == 0 else jnp.concatenate([zero, q2], axis=0)
        s = jnp.dot(kk, qpad, preferred_element_type=f32)
        s = jnp.where(mask, s - slope_ref[g:g + 1, :] * distf, NEG)
        sink = sink_ref[g:g + 1, :]
        m = jnp.maximum(jnp.max(s, axis=0, keepdims=True), sink)
        p = jnp.where(mask, jnp.exp(s - m), 0.0)
        den = jnp.sum(p, axis=0, keepdims=True) + jnp.exp(sink - m)
        p = (p / den).astype(bf16)
        o = lax.dot_general(vv, p, (((0,), (0,)), ((), ())), preferred_element_type=f32)
        og = o[64 * g:64 * g + 64]
        heads += [og[:, 0:TQ], og[:, TQ:2 * TQ]]
    o_ref[...] = jnp.concatenate(heads, axis=0).T.astype(bf16)


def swa_attention(kvc, qtc, slope_l, sink_l, *, B, S, TQ=Q_BLOCK):
    NB = S // TQ
    prev = lambda c: (lambda b, i: (b, jnp.maximum(i - 1, 0), c))
    cur = lambda c: (lambda b, i: (b, i, c))
    return pl.pallas_call(
        functools.partial(_swa_kernel, TQ=TQ),
        grid=(B, NB),
        in_specs=[
            pl.BlockSpec((None, TQ, LANES), prev(0)), pl.BlockSpec((None, TQ, LANES), cur(0)),
            pl.BlockSpec((None, TQ, LANES), prev(1)), pl.BlockSpec((None, TQ, LANES), cur(1)),
            pl.BlockSpec((None, C_WIDTH, TQ), lambda b, i: (b, 0, i)),
            pl.BlockSpec((SWA_KV, 2 * TQ), lambda b, i: (0, 0)),
            pl.BlockSpec((SWA_KV, 2 * TQ), lambda b, i: (0, 0)),
        ],
        out_specs=pl.BlockSpec((None, TQ, C_WIDTH), lambda b, i: (b, i, 0)),
        out_shape=jax.ShapeDtypeStruct((B, S, C_WIDTH), bf16),
        compiler_params=_cparams(("arbitrary", "arbitrary")),
        name="swa_attention",
    )(kvc, kvc, kvc, kvc, qtc, slope_l, sink_l)


def _nsa_compress_kernel(ch_ref, w1_ref, pe_ref, w2_ref, gain_ref, o_ref, *, NCP):
    w1 = w1_ref[...]
    p = jnp.dot(ch_ref[...], w1, preferred_element_type=f32)
    pb = jnp.dot(pe_ref[...], w1, preferred_element_type=f32)
    bias = pb[0:1, 0:512] + pb[1:2, 512:1024]
    hid = p[:, 0:512] + pltpu.roll(p[:, 512:1024], NCP - 1, axis=0) + bias
    y = jnp.dot(jax.nn.gelu(hid).astype(bf16), w2_ref[...], preferred_element_type=f32)
    is_k = lax.broadcasted_iota(i32, y.shape, 1) < HEAD_DIM
    ss = jnp.sum(jnp.where(is_k, y * y, 0.0), axis=-1, keepdims=True)
    yn = y * lax.rsqrt(ss * (1.0 / HEAD_DIM) + EPS) * gain_ref[...]
    o_ref[...] = jnp.where(is_k, yn, y).astype(bf16)


def nsa_compress(ch, w1big, pe2, w2big, gain, *, B, S):
    NCP = S // CMP_STRIDE
    W = CMP_STRIDE * LANES
    return pl.pallas_call(
        functools.partial(_nsa_compress_kernel, NCP=NCP),
        grid=(B,),
        in_specs=[
            pl.BlockSpec((None, NCP, W), lambda b: (b, 0, 0)),
            pl.BlockSpec((W, 4 * CMP_HID), lambda b: (0, 0)),
            pl.BlockSpec((SUBLANES, W), lambda b: (0, 0)),
            pl.BlockSpec((2 * CMP_HID, LANES), lambda b: (0, 0)),
            pl.BlockSpec((1, LANES), lambda b: (0, 0)),
        ],
        out_specs=pl.BlockSpec((None, NCP, LANES), lambda b: (b, 0, 0)),
        out_shape=jax.ShapeDtypeStruct((B, NCP, LANES), bf16),
        compiler_params=_cparams(("arbitrary",)),
        name="nsa_compress",
    )(ch, w1big, pe2, w2big, gain)


def _q_all_heads(qt, TQ):
    qr = jnp.concatenate([qt[h * HEAD_DIM:(h + 1) * HEAD_DIM] for h in range(NSA_HEADS)], axis=1)
    return jnp.concatenate([qr, jnp.zeros_like(qr)], axis=0)


def _nsa_select_kernel(qt_ref, kv_ref, slope_ref, ocmp_ref, sel_ref, flag_ref, s_scr, *, TQ, S):
    NSEL = S // SEL_BLOCK
    NC = S // CMP_STRIDE - 1
    R = SEL_BLOCK // CMP_STRIDE
    KSEL = min(SEL_TOPK, NSEL)
    HW = NSA_HEADS * TQ
    i = pl.program_id(1)
    qpad = _q_all_heads(qt_ref[...], TQ)
    slope = slope_ref[...]
    t = i * TQ + lax.broadcasted_iota(i32, (1, HW), 1) % TQ
    j_col = lax.broadcasted_iota(i32, (NSEL, 1), 0)
    m = jnp.full((1, HW), NEG, f32)
    for r in range(R):
        s = jnp.dot(kv_ref[r * NSEL:(r + 1) * NSEL, :], qpad, preferred_element_type=f32)
        c = R * j_col + r
        dist = t - (CMP_STRIDE * c + (CMP_LEN - 1))
        s = jnp.where((dist >= 0) & (c < NC), s - slope * dist.astype(f32), NEG)
        s_scr[r] = s
        m = jnp.maximum(m, jnp.max(s, axis=0, keepdims=True))
    l = jnp.zeros((1, HW), f32)
    for r in range(R):
        s = s_scr[r]
        p = jnp.where(s > 0.5 * NEG, jnp.exp(s - m), 0.0)
        s_scr[r] = p
        l = l + jnp.sum(p, axis=0, keepdims=True)
    inv = 1.0 / jnp.maximum(l, 1e-30)
    o = jnp.zeros((LANES, HW), f32)
    ps = []
    for r in range(R):
        pn = s_scr[r] * inv
        o = o + lax.dot_general(kv_ref[r * NSEL:(r + 1) * NSEL, :], pn.astype(bf16), (((0,), (0,)), ((), ())),
                                preferred_element_type=f32)
        acc = pn[:, 0:TQ]
        for h in range(1, NSA_HEADS):
            acc = acc + pn[:, h * TQ:(h + 1) * TQ]
        ps.append(acc)
    ocmp_ref[...] = o[HEAD_DIM:2 * HEAD_DIM].astype(ocmp_ref.dtype)
    jrow = lax.broadcasted_iota(i32, (NSEL, TQ), 0)
    prev3 = jnp.where(jrow == 0, 0.0, pltpu.roll(ps[3], 1, axis=0))
    imp = 0.5 * prev3 + ps[0] + ps[1] + ps[2] + 0.5 * ps[3]
    tq = i * TQ + lax.broadcasted_iota(i32, (1, TQ), 1)
    cur = tq // SEL_BLOCK
    forced = (jrow == 0) | (jrow == cur) | (jrow == cur - 1)
    causal = jrow <= cur
    score = jnp.where(causal, jnp.where(forced, BIG, imp), -1e38)
    jf = jrow.astype(f32)
    taken = jnp.zeros((NSEL, TQ), jnp.bool_)
    picks = []
    for _ in range(KSEL):
        mx = jnp.max(score, axis=0, keepdims=True)
        idx = jnp.min(jnp.where(score == mx, jf, float(NSEL)), axis=0, keepdims=True)
        hit = jf == idx
        taken = taken | hit
        score = jnp.where(hit, -jnp.inf, score)
        picks.append(idx)
    picks += [picks[-1]] * (SUBLANES - KSEL)
    sel_ref[...] = jnp.concatenate(picks, axis=0).astype(i32)
    used = jnp.where(taken & causal, 1.0, 0.0).astype(bf16)
    flag_ref[...] = lax.dot_general(jnp.ones((SUBLANES, TQ), bf16), used, (((1,), (1,)), ((), ())),
                                    preferred_element_type=f32)


def nsa_select(qta, kvcmp, slope_lane, *, B, S, TQ=Q_BLOCK):
    NB = S // TQ
    NCP = S // CMP_STRIDE
    NSEL = S // SEL_BLOCK
    HW = NSA_HEADS * TQ
    return pl.pallas_call(
        functools.partial(_nsa_select_kernel, TQ=TQ, S=S),
        grid=(B, NB),
        in_specs=[
            pl.BlockSpec((None, A_WIDTH, TQ), lambda b, i: (b, 0, i)),
            pl.BlockSpec((None, NCP, LANES), lambda b, i: (b, 0, 0)),
            pl.BlockSpec((1, HW), lambda b, i: (0, 0)),
        ],
        out_specs=(
            pl.BlockSpec((None, None, HEAD_DIM, HW), lambda b, i: (b, i, 0, 0)),
            pl.BlockSpec((None, SUBLANES, TQ), lambda b, i: (b, 0, i)),
            pl.BlockSpec((None, None, SUBLANES, NSEL), lambda b, i: (b, i, 0, 0)),
        ),
        out_shape=(
            jax.ShapeDtypeStruct((B, NB, HEAD_DIM, HW), bf16),
            jax.ShapeDtypeStruct((B, SUBLANES, S), i32),
            jax.ShapeDtypeStruct((B, NB, SUBLANES, NSEL), f32),
        ),
        scratch_shapes=[pltpu.VMEM((SEL_BLOCK // CMP_STRIDE, NSEL, HW), f32)],
        compiler_params=_cparams(("arbitrary", "arbitrary")),
        name="nsa_select",
    )(qta, kvcmp, slope_lane)


_SLC_CHUNK = 4


def _nsa_attend_kernel(cnt_ref, lst_ref, qt_ref, kslc_ref, kwin_ref, sel_ref, ocmp_ref, gt_ref, slope_ref,
                       o_ref, acc_scr, m_scr, l_scr, *, TQ, S):
    NB = S // TQ
    NSEL = S // SEL_BLOCK
    HW = NSA_HEADS * TQ
    b = pl.program_id(0)
    i = pl.program_id(1)
    count = cnt_ref[b * NB + i]
    qpad = _q_all_heads(qt_ref[...], TQ)
    slope = slope_ref[...]
    tq = i * TQ + lax.broadcasted_iota(i32, (1, HW), 1) % TQ
    sel = sel_ref[...]
    tn = (((0,), (0,)), ((), ()))

    acc_scr[...] = jnp.zeros_like(acc_scr)
    m_scr[...] = jnp.full_like(m_scr, NEG)
    l_scr[...] = jnp.zeros_like(l_scr)

    def chunk(c, carry):
        blks, masks, biases = [], [], []
        for b4 in range(_SLC_CHUNK):
            idx = _SLC_CHUNK * c + b4
            ok = idx < count
            j = jnp.where(ok, lst_ref[0, 0, jnp.minimum(idx, NSEL - 1)], 0)
            blks.append(kslc_ref[pl.ds(pl.multiple_of(j * SEL_BLOCK, SEL_BLOCK), SEL_BLOCK), :])
            picked = jnp.max(jnp.where(sel == j, 1.0, 0.0), axis=0, keepdims=True) > 0.0
            picked = jnp.concatenate([picked & ok] * NSA_HEADS, axis=1)
            dist = tq - (j * SEL_BLOCK + lax.broadcasted_iota(i32, (SEL_BLOCK, 1), 0))
            masks.append((dist >= 0) & picked)
            biases.append(slope * dist.astype(f32))
        kc = jnp.concatenate(blks, axis=0)
        s = jnp.dot(kc, qpad, preferred_element_type=f32)
        parts = [jnp.where(masks[b4], s[b4 * SEL_BLOCK:(b4 + 1) * SEL_BLOCK] - biases[b4], NEG)
                 for b4 in range(_SLC_CHUNK)]
        m_old = m_scr[...]
        m_new = m_old
        for part in parts:
            m_new = jnp.maximum(m_new, jnp.max(part, axis=0, keepdims=True))
        alpha = jnp.exp(m_old - m_new)
        ps = [jnp.where(masks[b4], jnp.exp(parts[b4] - m_new), 0.0) for b4 in range(_SLC_CHUNK)]
        lsum = ps[0].sum(axis=0, keepdims=True)
        for part in ps[1:]:
            lsum = lsum + part.sum(axis=0, keepdims=True)
        l_scr[...] = l_scr[...] * alpha + lsum
        pv = lax.dot_general(kc, jnp.concatenate(ps, axis=0).astype(bf16), tn, preferred_element_type=f32)
        acc_scr[...] = acc_scr[...] * alpha + pv[HEAD_DIM:2 * HEAD_DIM]
        m_scr[...] = m_new
        return carry

    lax.fori_loop(0, (count + _SLC_CHUNK - 1) // _SLC_CHUNK, chunk, 0)
    o_slc = acc_scr[...] / jnp.maximum(l_scr[...], 1e-30)

    NWB = NSA_WINDOW // TQ + 1
    blks, masks, biases = [], [], []
    for w in range(NWB):
        jb = i - (NWB - 1) + w
        blks.append(kwin_ref[pl.ds(pl.multiple_of(jnp.maximum(jb, 0) * TQ, TQ), TQ), :])
        dist = tq - (jb * TQ + lax.broadcasted_iota(i32, (TQ, 1), 0))
        masks.append((dist >= 0) & (dist < NSA_WINDOW) & (jb >= 0))
        biases.append(slope * dist.astype(f32))
    kw = jnp.concatenate(blks, axis=0)
    s = jnp.dot(kw, qpad, preferred_element_type=f32)
    parts = [jnp.where(masks[w], s[w * TQ:(w + 1) * TQ] - biases[w], NEG) for w in range(NWB)]
    m = parts[0].max(axis=0, keepdims=True)
    for part in parts[1:]:
        m = jnp.maximum(m, part.max(axis=0, keepdims=True))
    ps = [jnp.where(masks[w], jnp.exp(parts[w] - m), 0.0) for w in range(NWB)]
    l = ps[0].sum(axis=0, keepdims=True)
    for part in ps[1:]:
        l = l + part.sum(axis=0, keepdims=True)
    pn = (jnp.concatenate(ps, axis=0) * (1.0 / jnp.maximum(l, 1e-30))).astype(bf16)
    o_win = lax.dot_general(kw, pn, tn, preferred_element_type=f32)[HEAD_DIM:2 * HEAD_DIM]

    gt = gt_ref[...]
    gate = [jnp.concatenate([gt[br * NSA_HEADS + h:br * NSA_HEADS + h + 1] for h in range(NSA_HEADS)], axis=1)
            for br in range(3)]
    o = gate[0] * ocmp_ref[...].astype(f32) + gate[1] * o_slc + gate[2] * o_win
    o = jnp.concatenate([o[:, h * TQ:(h + 1) * TQ] for h in range(NSA_HEADS)], axis=0)
    o_ref[...] = o.T.astype(bf16)


def nsa_attend(counts, lists, qta, kva, selT, ocmp, gT, slope_lane, *, B, S, TQ=Q_BLOCK):
    NB = S // TQ
    NSEL = S // SEL_BLOCK
    HW = NSA_HEADS * TQ
    grid_spec = pltpu.PrefetchScalarGridSpec(
        num_scalar_prefetch=1,
        grid=(B, NB),
        in_specs=[
            pl.BlockSpec((1, 1, NSEL), lambda b, i, cnt: (b * NB + i, 0, 0), memory_space=pltpu.SMEM),
            pl.BlockSpec((None, A_WIDTH, TQ), lambda b, i, cnt: (b, 0, i)),
            pl.BlockSpec((None, S, LANES), lambda b, i, cnt: (b, 0, 1)),
            pl.BlockSpec((None, S, LANES), lambda b, i, cnt: (b, 0, 2)),
            pl.BlockSpec((None, SUBLANES, TQ), lambda b, i, cnt: (b, 0, i)),
            pl.BlockSpec((None, None, HEAD_DIM, HW), lambda b, i, cnt: (b, i, 0, 0)),
            pl.BlockSpec((None, 32, TQ), lambda b, i, cnt: (b, 0, i)),
            pl.BlockSpec((1, HW), lambda b, i, cnt: (0, 0)),
        ],
        out_specs=pl.BlockSpec((None, TQ, A_WIDTH), lambda b, i, cnt: (b, i, 0)),
        scratch_shapes=[pltpu.VMEM((HEAD_DIM, HW), f32), pltpu.VMEM((1, HW), f32), pltpu.VMEM((1, HW), f32)],
    )
    return pl.pallas_call(
        functools.partial(_nsa_attend_kernel, TQ=TQ, S=S),
        grid_spec=grid_spec,
        out_shape=jax.ShapeDtypeStruct((B, S, A_WIDTH), bf16),
        compiler_params=_cparams(("arbitrary", "arbitrary")),
        name="nsa_attend",
    )(counts, lists, qta, kva, kva, selT, ocmp, gT, slope_lane)


def nsa_branch(qta, kva, gT, cmp_w, slope_lane, *, B, S):
    NB = S // Q_BLOCK
    NSEL = S // SEL_BLOCK
    NCP = S // CMP_STRIDE
    R = SEL_BLOCK // CMP_STRIDE
    kva = kva.reshape(B, S, 6 * HEAD_DIM)
    ch = kva[:, :, 0:LANES].reshape(B, NCP, CMP_STRIDE * LANES)
    kvcmp = nsa_compress(ch, *cmp_w, B=B, S=S)
    kvcmp = kvcmp.reshape(B, NSEL, R, LANES).swapaxes(1, 2).reshape(B, NCP, LANES)
    ocmp, selT, flags = nsa_select(qta, kvcmp, slope_lane, B=B, S=S)
    used = flags[:, :, 0, :] > 0.5
    ar = jnp.arange(NSEL, dtype=i32)
    lists = jnp.minimum(jnp.sort(jnp.where(used, ar, ar + NSEL), axis=-1), NSEL - 1)
    counts = jnp.sum(used, axis=-1).astype(i32)
    return nsa_attend(counts.reshape(B * NB), lists.reshape(B * NB, 1, NSEL), qta, kva, selT, ocmp, gT, slope_lane,
                      B=B, S=S)


def _route(logit, rb):
    sc = jax.nn.sigmoid(logit)
    sel = sc + rb
    srow = [sel[e:e + 1] for e in range(N_EXPERTS)]
    crow = [sc[e:e + 1] for e in range(N_EXPERTS)]
    gscore = []
    for g in range(N_GROUPS):
        a, b, c, d = srow[4 * g:4 * g + 4]
        top2 = jnp.maximum(jnp.maximum(jnp.maximum(a + b, a + c), jnp.maximum(a + d, b + c)),
                           jnp.maximum(b + d, c + d))
        gscore.append(top2)
    best, gi = gscore[0], jnp.zeros_like(gscore[0], dtype=i32)
    for g in range(1, N_GROUPS):
        better = gscore[g] > best
        gi = jnp.where(better, g, gi)
        best = jnp.where(better, gscore[g], best)

    def pick_group(rows, k):
        v = rows[k]
        for g in range(1, N_GROUPS):
            v = jnp.where(gi == g, rows[4 * g + k], v)
        return v

    iv = [pick_group(srow, k) for k in range(EXPERTS_PER_GROUP)]
    ic = [pick_group(crow, k) for k in range(EXPERTS_PER_GROUP)]
    b1, i1, w1 = iv[0], jnp.zeros_like(gi), ic[0]
    for k in range(1, EXPERTS_PER_GROUP):
        better = iv[k] > b1
        i1 = jnp.where(better, k, i1)
        w1 = jnp.where(better, ic[k], w1)
        b1 = jnp.where(better, iv[k], b1)
    b2 = jnp.full_like(b1, -jnp.inf)
    i2, w2 = jnp.zeros_like(gi), jnp.zeros_like(w1)
    for k in range(EXPERTS_PER_GROUP):
        better = (i1 != k) & (iv[k] > b2)
        i2 = jnp.where(better, k, i2)
        w2 = jnp.where(better, ic[k], w2)
        b2 = jnp.where(better, iv[k], b2)
    tot = w1 + w2
    eidx = jnp.concatenate([gi * EXPERTS_PER_GROUP + i1, gi * EXPERTS_PER_GROUP + i2], axis=0)
    ew = jnp.concatenate([w1 / tot, w2 / tot], axis=0)
    return eidx, ew


def _mix_out_kernel(x_ref, a_ref, sh_ref, ga_ref, oa_ref, ob_ref, oc_ref, wm_ref, wa_ref, wb_ref, wc_ref, wo_ref,
                    am_ref, shm_ref, wr_ref, rb_ref, xo_ref, h2_ref, eidx_ref, ew_ref):
    x = x_ref[...]
    h = _norm_mod(x, a_ref[...], sh_ref[...]).astype(bf16)
    gates = jax.nn.sigmoid(jnp.dot(h, wm_ref[...], preferred_element_type=f32))
    D = D_MODEL
    y = gates[:, 0:D] * jnp.dot(oa_ref[...], wa_ref[...], preferred_element_type=f32)
    y = y + gates[:, D:2 * D] * jnp.dot(ob_ref[...], wb_ref[...], preferred_element_type=f32)
    y = y + gates[:, 2 * D:3 * D] * jnp.dot(oc_ref[...], wc_ref[...], preferred_element_type=f32)
    xn = x + ga_ref[...] * jnp.dot(y.astype(bf16), wo_ref[...], preferred_element_type=f32)
    xo_ref[...] = xn
    h2 = _norm_mod(xn, am_ref[...], shm_ref[...])
    hi = h2.astype(bf16)
    h2_ref[...] = hi
    lo = (h2 - hi.astype(f32)).astype(bf16)
    wr = wr_ref[...]
    whi = wr.astype(bf16)
    wlo = (wr - whi.astype(f32)).astype(bf16)
    nt = lambda p, q: lax.dot_general(p, q, (((1,), (1,)), ((), ())), preferred_element_type=f32)
    logit = nt(whi, hi) + nt(whi, lo) + nt(wlo, hi)
    eidx, ew = _route(logit, rb_ref[...])
    eidx_ref[...] = eidx
    ew_ref[...] = ew


def mix_out(x2, a_mod, sh_mod, g_a, oa, ob, oc, wm, wa, wb, wc, wo, am_moe, shm_moe, wrT, rb, *, B, S, tm=256):
    N, D = x2.shape
    tpb = S // tm
    row = lambda i: (i, 0)
    bmap = lambda i: (i // tpb, 0, 0)
    full = lambda i: (0, 0)
    col = lambda i: (0, i)
    return pl.pallas_call(
        _mix_out_kernel,
        grid=(N // tm,),
        in_specs=[
            pl.BlockSpec((tm, D), row),
            pl.BlockSpec((None, 1, D), bmap), pl.BlockSpec((None, 1, D), bmap), pl.BlockSpec((None, 1, D), bmap),
            pl.BlockSpec((tm, A_WIDTH), row), pl.BlockSpec((tm, B_WIDTH), row), pl.BlockSpec((tm, C_WIDTH), row),
            pl.BlockSpec((D, 3 * D), full), pl.BlockSpec((A_WIDTH, D), full), pl.BlockSpec((B_WIDTH, D), full),
            pl.BlockSpec((C_WIDTH, D), full), pl.BlockSpec((D, D), full),
            pl.BlockSpec((None, 1, D), bmap), pl.BlockSpec((None, 1, D), bmap),
            pl.BlockSpec((N_EXPERTS, D), full), pl.BlockSpec((N_EXPERTS, 1), full),
        ],
        out_specs=(pl.BlockSpec((tm, D), row), pl.BlockSpec((tm, D), row),
                   pl.BlockSpec((TOP_K, tm), col), pl.BlockSpec((TOP_K, tm), col)),
        out_shape=(jax.ShapeDtypeStruct((N, D), f32), jax.ShapeDtypeStruct((N, D), bf16),
                   jax.ShapeDtypeStruct((TOP_K, N), i32), jax.ShapeDtypeStruct((TOP_K, N), f32)),
        compiler_params=_cparams(("arbitrary",)),
        name="mix_out",
    )(x2, a_mod, sh_mod, g_a, oa, ob, oc, wm, wa, wb, wc, wo, am_moe, shm_moe, wrT, rb)


def _moe_ffn_kernel(be_ref, xs_ref, wg_ref, wu_ref, wd_ref, sw_ref, y_ref):
    xs = xs_ref[...]
    g = jnp.dot(xs, wg_ref[...], preferred_element_type=f32)
    u = jnp.dot(xs, wu_ref[...], preferred_element_type=f32)
    a = (jax.nn.silu(g) * u).astype(bf16)
    y_ref[...] = (jnp.dot(a, wd_ref[...], preferred_element_type=f32) * sw_ref[...]).astype(y_ref.dtype)


def moe_ffn(blk_e, xs, wg, wu, wd, slot_w):
    n_rows, D = xs.shape
    n_blk = n_rows // MOE_BLOCK
    grid_spec = pltpu.PrefetchScalarGridSpec(
        num_scalar_prefetch=1,
        grid=(n_blk,),
        in_specs=[
            pl.BlockSpec((MOE_BLOCK, D), lambda i, be: (i, 0)),
            pl.BlockSpec((None, D, D_EXPERT), lambda i, be: (be[i], 0, 0)),
            pl.BlockSpec((None, D, D_EXPERT), lambda i, be: (be[i], 0, 0)),
            pl.BlockSpec((None, D_EXPERT, D), lambda i, be: (be[i], 0, 0)),
            pl.BlockSpec((MOE_BLOCK, 1), lambda i, be: (i, 0)),
        ],
        out_specs=pl.BlockSpec((MOE_BLOCK, D), lambda i, be: (i, 0)),
    )
    return pl.pallas_call(
        _moe_ffn_kernel,
        grid_spec=grid_spec,
        out_shape=jax.ShapeDtypeStruct((n_rows, D), bf16),
        compiler_params=_cparams(("arbitrary",)),
        name="moe_ffn",
    )(blk_e, xs, wg, wu, wd, slot_w)


def moe_dispatch(eidx, ew, N):
    NK = N * TOP_K
    n_blk = -(-NK // MOE_BLOCK) + N_EXPERTS
    flat_e = eidx.T.reshape(-1)
    flat_w = ew.T.reshape(-1)
    order = jnp.argsort(flat_e)
    e_sorted = flat_e[order]
    counts = jnp.bincount(flat_e, length=N_EXPERTS)
    padded = (counts + MOE_BLOCK - 1) // MOE_BLOCK * MOE_BLOCK
    p_end = jnp.cumsum(padded)
    p_start = p_end - padded
    start = jnp.cumsum(counts) - counts
    dest = (p_start[e_sorted] + jnp.arange(NK) - start[e_sorted]).astype(i32)
    slot_tok = jnp.full((n_blk * MOE_BLOCK,), N, i32).at[dest].set((order // TOP_K).astype(i32))
    slot_w = jnp.zeros((n_blk * MOE_BLOCK,), f32).at[dest].set(flat_w[order])
    blk_e = jnp.minimum(jnp.searchsorted(p_end, jnp.arange(n_blk) * MOE_BLOCK, side='right'), N_EXPERTS - 1).astype(i32)
    pos = jnp.zeros((NK,), i32).at[order].set(dest).reshape(N, TOP_K)
    return slot_tok, slot_w, blk_e, pos


def _prep_cmp_weights(pe_k, w1_k, w2_k, pe_v, w1_v, w2_v, k_gain0):
    T = CMP_STRIDE
    w1big = jnp.zeros((T, 2, HEAD_DIM, 4, CMP_HID), f32)
    w1k = w1_k.reshape(2, T, HEAD_DIM, CMP_HID)
    w1v = w1_v.reshape(2, T, HEAD_DIM, CMP_HID)
    w1big = w1big.at[:, 0, :, 0].set(w1k[0]).at[:, 1, :, 1].set(w1v[0])
    w1big = w1big.at[:, 0, :, 2].set(w1k[1]).at[:, 1, :, 3].set(w1v[1])
    w1big = w1big.reshape(T * LANES, 4 * CMP_HID).astype(bf16)
    pe = jnp.stack([pe_k.reshape(2, T, HEAD_DIM), pe_v.reshape(2, T, HEAD_DIM)], axis=2)
    pe2 = jnp.zeros((SUBLANES, T * LANES), f32).at[0:2].set(pe.reshape(2, T * LANES)).astype(bf16)
    w2big = jnp.zeros((2 * CMP_HID, LANES), f32)
    w2big = w2big.at[0:CMP_HID, 0:HEAD_DIM].set(w2_k).at[CMP_HID:, HEAD_DIM:].set(w2_v).astype(bf16)
    gain = jnp.concatenate([k_gain0, jnp.ones((HEAD_DIM,), f32)])[None, :]
    return w1big, pe2, w2big, gain


def _prep_proj_weights(w_in, nsa_q_gain, nsa_k_gain, swa_q_gain, swa_k_gain):
    w = w_in
    wtok = jnp.concatenate([w[:, _O_KVA:_O_GA], w[:, _O_KB:_O_VB], w[:, _O_KC:_O_MG]], axis=1).astype(bf16)
    gate_perm = np.array([h * 3 + br for br in range(3) for h in range(NSA_HEADS)])
    scale = HEAD_DIM ** -0.5
    wch = jnp.concatenate([w[:, _O_QA:_O_KVA], w[:, _O_QB:_O_KB] * (0.5 * scale), w[:, _O_QC:_O_KC],
                           w[:, _O_VB:_O_QC], w[:, _O_GA:_O_QB][:, gate_perm], jnp.zeros((D_MODEL, 8), f32)],
                          axis=1).T.astype(bf16)
    gtok = jnp.zeros((1, TOK_W), f32)
    gtok = gtok.at[0, 128:192].set(nsa_k_gain[1]).at[0, 256:320].set(nsa_k_gain[2])
    gtok = gtok.at[0, 640:704].set(swa_k_gain).at[0, 704:768].set(swa_k_gain)
    mtok = jnp.zeros((1, TOK_W), f32).at[0, 128:192].set(1.0).at[0, 256:320].set(1.0).at[0, 640:768].set(1.0)
    gcha = (jnp.tile(nsa_q_gain, NSA_HEADS) * scale)[:, None]
    gchc = (jnp.tile(swa_q_gain, SWA_HEADS) * scale)[:, None]
    blk = np.arange(LANES) // HEAD_DIM
    bd = jnp.asarray(blk[:, None] == blk[None, :], bf16)
    return wtok, wch, gtok, mtok, gcha, gchc, bd


def _adaln_kernel(c_ref, w_ref, b_ref, o_ref):
    cond = jax.nn.silu(c_ref[...]).astype(bf16)
    o_ref[...] = jnp.dot(cond, w_ref[...].astype(bf16), preferred_element_type=f32) + b_ref[...]


def adaln(c_pad, w_ada, b_ada):
    L, D, D6 = w_ada.shape
    return pl.pallas_call(
        _adaln_kernel,
        grid=(L, D6 // D),
        in_specs=[
            pl.BlockSpec((SUBLANES, D), lambda l, j: (0, 0)),
            pl.BlockSpec((None, D, D), lambda l, j: (l, 0, j)),
            pl.BlockSpec((None, 1, D), lambda l, j: (l, 0, j)),
        ],
        out_specs=pl.BlockSpec((None, SUBLANES, D), lambda l, j: (l, 0, j)),
        out_shape=jax.ShapeDtypeStruct((L, SUBLANES, D6), f32),
        compiler_params=_cparams(("arbitrary", "arbitrary")),
        name="adaln",
    )(c_pad, w_ada, b_ada.reshape(L, 1, D6))


def _moe_combine_kernel(x_ref, y0_ref, y1_ref, gm_ref, o_ref):
    o_ref[...] = x_ref[...] + gm_ref[...] * (y0_ref[...].astype(f32) + y1_ref[...].astype(f32))


def moe_combine(x2, y0, y1, g_m, *, B, S, tm=512):
    N, D = x2.shape
    tpb = S // tm
    row = lambda i: (i, 0)
    return pl.pallas_call(
        _moe_combine_kernel,
        grid=(N // tm,),
        in_specs=[pl.BlockSpec((tm, D), row), pl.BlockSpec((tm, D), row), pl.BlockSpec((tm, D), row),
                  pl.BlockSpec((None, 1, D), lambda i: (i // tpb, 0, 0))],
        out_specs=pl.BlockSpec((tm, D), row),
        out_shape=jax.ShapeDtypeStruct((N, D), f32),
        compiler_params=_cparams(("arbitrary",)),
        name="moe_combine",
    )(x2, y0, y1, g_m)


def kernel(x, c, w_ada, b_ada, g_norm_mix, g_norm_moe, w_in, cmp_pe_k, cmp_w1_k, cmp_w2_k, cmp_pe_v, cmp_w1_v,
           cmp_w2_v, nsa_q_gain, nsa_k_gain, swa_q_gain, swa_k_gain, swa_sinks, w_branch_a, w_branch_b,
           w_branch_c, w_out, w_router, router_bias, w_exp_gate, w_exp_up, w_exp_down):
    B, S, D = x.shape
    N = B * S
    L = w_ada.shape[0]
    slopes = 2.0 ** (-8.0 * jnp.arange(1, SWA_HEADS + NSA_HEADS + 1, dtype=f32) / (SWA_HEADS + NSA_HEADS))
    swa_slope_l = jnp.repeat(slopes[:SWA_HEADS].reshape(SWA_KV, SWA_HEADS // SWA_KV), Q_BLOCK, axis=1)
    nsa_slope_l = jnp.repeat(slopes[SWA_HEADS:], Q_BLOCK)[None, :]
    c_pad = jnp.zeros((SUBLANES, D), f32).at[:B].set(c)
    mod_all = adaln(c_pad, w_ada, b_ada)[:, :B]
    wrT = w_router.T
    rb = router_bias[:, None]
    x2 = x.reshape(N, D)
    for l in range(L):
        sh_a, sc_a, g_a, sh_m, sc_m, g_m = [m[:, None, :] for m in jnp.split(mod_all[l], 6, axis=-1)]
        a_mix = g_norm_mix[l][None, None, :] * (1.0 + sc_a)
        a_moe = g_norm_moe[l][None, None, :] * (1.0 + sc_m)
        pw = _prep_proj_weights(w_in[l], nsa_q_gain[l], nsa_k_gain[l], swa_q_gain[l], swa_k_gain[l])
        kva, kb, kvc, qta, qtb, qtc, vtb, gT = proj_in(x2, a_mix, sh_a, *pw, B=B, S=S)
        cw = _prep_cmp_weights(cmp_pe_k[l], cmp_w1_k[l], cmp_w2_k[l], cmp_pe_v[l], cmp_w1_v[l], cmp_w2_v[l],
                               nsa_k_gain[l][0])
        o_a = nsa_branch(qta, kva, gT, cw, nsa_slope_l, B=B, S=S)
        o_b = sb_attention(_perm_rows(kb.reshape(B, S, B_WIDTH), S), _perm_lanes(vtb, S), qtb, B=B, S=S)
        sink_l = jnp.repeat(swa_sinks[l].reshape(SWA_KV, SWA_HEADS // SWA_KV), Q_BLOCK, axis=1)
        o_c = swa_attention(kvc.reshape(B, S, 2 * SWA_KV * HEAD_DIM), qtc, swa_slope_l, sink_l, B=B, S=S)
        x2, h2, eidx, ew = mix_out(
            x2, a_mix, sh_a, g_a, o_a.reshape(N, A_WIDTH), o_b.reshape(N, B_WIDTH), o_c.reshape(N, C_WIDTH),
            w_in[l][:, _O_MG:_O_END].astype(bf16), w_branch_a[l].astype(bf16), w_branch_b[l].astype(bf16),
            w_branch_c[l].astype(bf16), w_out[l].astype(bf16), a_moe, sh_m, wrT, rb, B=B, S=S)
        slot_tok, slot_w, blk_e, pos = moe_dispatch(eidx, ew, N)
        xs = jnp.concatenate([h2, jnp.zeros((1, D), bf16)], axis=0)[slot_tok]
        y = moe_ffn(blk_e, xs, w_exp_gate[l].astype(bf16), w_exp_up[l].astype(bf16), w_exp_down[l].astype(bf16),
                    slot_w[:, None])
        x2 = moe_combine(x2, y[pos[:, 0]], y[pos[:, 1]], g_m, B=B, S=S)
    return x2.reshape(B, S, D)
```

```python
import functools

import numpy as np
import jax
import jax.numpy as jnp
from jax import lax
from jax.experimental import pallas as pl
from jax.experimental.pallas import tpu as pltpu

f32 = jnp.float32
bf16 = jnp.bfloat16
i32 = jnp.int32

D_MODEL = 1024
HEAD_DIM = 64
Q_BLOCK = 128
NSA_HEADS = 8
CMP_STRIDE = 16
CMP_LEN = 32
CMP_HID = 256
SEL_BLOCK = 64
SEL_TOPK = 8
NSA_WINDOW = 512
SB_HEADS = 4
SWA_HEADS = 4
SWA_KV = 2
SWA_WINDOW = 128
N_EXPERTS = 16
N_GROUPS = 4
EXPERTS_PER_GROUP = 4
TOP_K = 2
D_EXPERT = 512
MOE_BLOCK = 512
EPS = 1e-6
NEG = -1e30
BIG = 1e30

A_WIDTH = NSA_HEADS * HEAD_DIM
B_WIDTH = SB_HEADS * HEAD_DIM
C_WIDTH = SWA_HEADS * HEAD_DIM

LANES = 128
SUBLANES = 8
VMEM_LIMIT = 56 * 1024 * 1024

_O_QA = 0
_O_KVA = _O_QA + A_WIDTH
_O_GA = _O_KVA + 6 * HEAD_DIM
_O_QB = _O_GA + 3 * NSA_HEADS
_O_KB = _O_QB + B_WIDTH
_O_VB = _O_KB + B_WIDTH
_O_QC = _O_VB + B_WIDTH
_O_KC = _O_QC + C_WIDTH
_O_VC = _O_KC + SWA_KV * HEAD_DIM
_O_MG = _O_VC + SWA_KV * HEAD_DIM
_O_END = _O_MG + 3 * D_MODEL

TOK_W = 6 * HEAD_DIM + B_WIDTH + 2 * SWA_KV * HEAD_DIM
CH_W = A_WIDTH + B_WIDTH + C_WIDTH + B_WIDTH + 32
_TOK_NORM_CHUNKS = (1, 2, 5)


def _cparams(sem):
    return pltpu.CompilerParams(dimension_semantics=sem, vmem_limit_bytes=VMEM_LIMIT)


def _split_dot(x, w):
    hi = x.astype(bf16)
    lo = (x - hi.astype(f32)).astype(bf16)
    return jnp.dot(hi, w, preferred_element_type=f32) + jnp.dot(lo, w, preferred_element_type=f32)


def _norm_mod(x, a, sh):
    rs = lax.rsqrt(jnp.mean(x * x, axis=-1, keepdims=True) + EPS)
    return x * rs * a + sh


def _group_norm_ch(y, gain_col, n, tm):
    y3 = y.reshape(n, HEAD_DIM, tm)
    ss = jnp.sum(y3 * y3, axis=1, keepdims=True)
    yn = y3 * lax.rsqrt(ss * (1.0 / HEAD_DIM) + EPS)
    return yn.reshape(n * HEAD_DIM, tm) * gain_col


def _proj_in_kernel(x_ref, a_ref, sh_ref, wtok_ref, wch_ref, gtok_ref, mtok_ref, gcha_ref, gchc_ref, bd_ref,
                    kva_ref, kb_ref, kvc_ref, qta_ref, qtb_ref, qtc_ref, vtb_ref, gt_ref, *, tm):
    h = _norm_mod(x_ref[...], a_ref[...], sh_ref[...]).astype(bf16)
    t = jnp.dot(h, wtok_ref[...], preferred_element_type=f32)
    cols = [t[:, c * LANES:(c + 1) * LANES] for c in range(TOK_W // LANES)]
    for c in _TOK_NORM_CHUNKS:
        y = cols[c]
        ss = _split_dot(y * y, bd_ref[...])
        yn = y * lax.rsqrt(ss * (1.0 / HEAD_DIM) + EPS) * gtok_ref[:, c * LANES:(c + 1) * LANES]
        cols[c] = jnp.where(mtok_ref[:, c * LANES:(c + 1) * LANES] > 0.0, yn, y)
    t = jnp.concatenate(cols, axis=1).astype(bf16)
    kva_ref[...] = t[:, 0:384]
    kb_ref[...] = t[:, 384:640]
    kvc_ref[...] = t[:, 640:896]
    c = lax.dot_general(wch_ref[...], h, (((1,), (1,)), ((), ())), preferred_element_type=f32)
    qta_ref[...] = _group_norm_ch(c[0:512], gcha_ref[...], NSA_HEADS, tm).astype(bf16)
    qtb_ref[...] = c[512:768].astype(bf16)
    qtc_ref[...] = _group_norm_ch(c[768:1024], gchc_ref[...], SWA_HEADS, tm).astype(bf16)
    vtb_ref[...] = c[1024:1280].astype(bf16)
    gt_ref[...] = jax.nn.sigmoid(c[1280:1312])


def proj_in(x2, a_mod, sh_mod, wtok, wch, gtok, mtok, gcha, gchc, bd, *, B, S, tm=512):
    N, D = x2.shape
    tpb = S // tm
    row = lambda i: (i, 0)
    bmap = lambda i: (i // tpb, 0, 0)
    cmap = lambda i: (i // tpb, 0, i % tpb)
    full = lambda i: (0, 0)
    out_shape = (
        jax.ShapeDtypeStruct((N, 384), bf16), jax.ShapeDtypeStruct((N, 256), bf16), jax.ShapeDtypeStruct((N, 256), bf16),
        jax.ShapeDtypeStruct((B, 512, S), bf16), jax.ShapeDtypeStruct((B, 256, S), bf16),
        jax.ShapeDtypeStruct((B, 256, S), bf16), jax.ShapeDtypeStruct((B, 256, S), bf16),
        jax.ShapeDtypeStruct((B, 32, S), f32),
    )
    return pl.pallas_call(
        functools.partial(_proj_in_kernel, tm=tm),
        grid=(N // tm,),
        in_specs=[
            pl.BlockSpec((tm, D), row),
            pl.BlockSpec((None, 1, D), bmap), pl.BlockSpec((None, 1, D), bmap),
            pl.BlockSpec((D, TOK_W), full), pl.BlockSpec((CH_W, D), full),
            pl.BlockSpec((1, TOK_W), full), pl.BlockSpec((1, TOK_W), full),
            pl.BlockSpec((512, 1), full), pl.BlockSpec((256, 1), full),
            pl.BlockSpec((LANES, LANES), full),
        ],
        out_specs=(
            pl.BlockSpec((tm, 384), row), pl.BlockSpec((tm, 256), row), pl.BlockSpec((tm, 256), row),
            pl.BlockSpec((None, 512, tm), cmap), pl.BlockSpec((None, 256, tm), cmap),
            pl.BlockSpec((None, 256, tm), cmap), pl.BlockSpec((None, 256, tm), cmap),
            pl.BlockSpec((None, 32, tm), cmap),
        ),
        out_shape=out_shape,
        compiler_params=_cparams(("arbitrary",)),
        name="proj_in",
    )(x2, a_mod, sh_mod, wtok, wch, gtok, mtok, gcha, gchc, bd)


def _excl_suffix_prod8(t, rows):
    def shift_up(x, k):
        return jnp.where(rows < SUBLANES - k, pltpu.roll(x, SUBLANES - k, axis=0), 1.0)
    x = shift_up(t, 1)
    x = x * shift_up(x, 1)
    x = x * shift_up(x, 2)
    x = x * shift_up(x, 4)
    return x


SB_TK = 128
SB_TQ = 512


def _sb_scores(k_ref, qpad, z_scr, j, slot):
    kb = k_ref[pl.ds(pl.multiple_of(j * SB_TK, SB_TK), SB_TK), :]
    for h in range(2):
        z_scr[slot, h] = jnp.dot(kb, qpad[h], preferred_element_type=f32)


def _sb_weights(z_scr, p_scr, a_scr, slot, laters, mask_off, TQ):
    V = SB_TK // SUBLANES
    rows = lax.broadcasted_iota(i32, (SUBLANES, TQ), 0)
    qidx = lax.broadcasted_iota(i32, (SUBLANES, TQ), 1)
    new_laters = []
    for h in range(2):
        run = jnp.ones((SUBLANES, TQ), f32)
        for v in reversed(range(V)):
            sl = slice(v * SUBLANES, (v + 1) * SUBLANES)
            beta = 0.5 - 0.5 * jnp.tanh(z_scr[slot, h, sl, :])
            if mask_off is not None:
                beta = jnp.where(rows * V + (v + mask_off) < qidx, beta, 1.0)
            nxt = run * beta
            p_scr[h, sl, :] = run - nxt
            run = nxt
        g = _excl_suffix_prod8(run, rows) * laters[h]
        for v2 in range(V // 2):
            sl = slice(2 * v2 * SUBLANES, (2 * v2 + 2) * SUBLANES)
            a_scr[slot, h, sl, :] = (p_scr[h, sl, :] * jnp.concatenate([g, g], axis=0)).astype(bf16)
        new_laters.append((g * run)[0:1, :])
    return new_laters


def _sb_accumulate(vt_ref, a_scr, acc_scr, j, slot):
    vtb = vt_ref[:, pl.ds(pl.multiple_of(j * SB_TK, SB_TK), SB_TK)]
    for h in range(2):
        acc_scr[h] += jnp.dot(vtb[h * HEAD_DIM:(h + 1) * HEAD_DIM], a_scr[slot, h], preferred_element_type=f32)


def _sb_kernel(k_ref, vt_ref, qt_ref, o_ref, acc_scr, z_scr, p_scr, a_scr, qpad, *, TQ):
    NBQ = TQ // SB_TK
    i = pl.program_id(2)
    qpad[...] = jnp.zeros_like(qpad)
    qpad[0, 0:HEAD_DIM, :] = qt_ref[0:HEAD_DIM, :]
    qpad[1, HEAD_DIM:2 * HEAD_DIM, :] = qt_ref[HEAD_DIM:2 * HEAD_DIM, :]
    acc_scr[...] = jnp.zeros_like(acc_scr)
    a_scr[1] = jnp.zeros_like(a_scr[1])
    j0 = i * NBQ + NBQ - 1

    def step(j, laters, mask_offs):
        _sb_scores(k_ref, qpad, z_scr, j - 1, 1)
        _sb_accumulate(vt_ref, a_scr, acc_scr, jnp.minimum(j + 1, j0), 1)
        laters = _sb_weights(z_scr, p_scr, a_scr, 0, laters, mask_offs[0], TQ)
        _sb_scores(k_ref, qpad, z_scr, jnp.maximum(j - 2, 0), 0)
        _sb_accumulate(vt_ref, a_scr, acc_scr, j, 0)
        laters = _sb_weights(z_scr, p_scr, a_scr, 1, laters, mask_offs[1], TQ)
        return laters

    _sb_scores(k_ref, qpad, z_scr, j0, 0)
    laters = [jnp.ones((1, TQ), f32)] * 2
    for d in range(NBQ // 2):
        jd = NBQ - 1 - 2 * d
        laters = step(i * NBQ + jd, laters, (jd * SB_TK, (jd - 1) * SB_TK))

    def body(t, carry):
        return tuple(step(i * NBQ - 1 - 2 * t, list(carry), (None, None)))

    lax.fori_loop(0, i * (NBQ // 2), body, tuple(laters))
    _sb_accumulate(vt_ref, a_scr, acc_scr, 0, 1)
    o_ref[...] = jnp.concatenate([acc_scr[0].T, acc_scr[1].T], axis=1).astype(bf16)


def sb_attention(kb_perm, vtb_perm, qtb, *, B, S, TQ=SB_TQ):
    TQ = min(TQ, S)
    return pl.pallas_call(
        functools.partial(_sb_kernel, TQ=TQ),
        grid=(B, SB_HEADS // 2, S // TQ),
        in_specs=[
            pl.BlockSpec((None, S, LANES), lambda b, hp, i: (b, 0, hp)),
            pl.BlockSpec((None, LANES, S), lambda b, hp, i: (b, hp, 0)),
            pl.BlockSpec((None, LANES, TQ), lambda b, hp, i: (b, hp, i)),
        ],
        out_specs=pl.BlockSpec((None, TQ, LANES), lambda b, hp, i: (b, i, hp)),
        out_shape=jax.ShapeDtypeStruct((B, S, B_WIDTH), bf16),
        scratch_shapes=[pltpu.VMEM((2, HEAD_DIM, TQ), f32), pltpu.VMEM((2, 2, SB_TK, TQ), f32),
                        pltpu.VMEM((2, SB_TK, TQ), f32), pltpu.VMEM((2, 2, SB_TK, TQ), bf16),
                        pltpu.VMEM((2, 2 * HEAD_DIM, TQ), bf16)],
        compiler_params=_cparams(("arbitrary", "arbitrary", "arbitrary")),
        name="sb_attention",
    )(kb_perm, vtb_perm, qtb)


def _perm_rows(z, S):
    B = z.shape[0]
    V = SB_TK // SUBLANES
    return z.reshape(B, S // SB_TK, SUBLANES, V, z.shape[-1]).swapaxes(2, 3).reshape(B, S, z.shape[-1])


def _perm_lanes(z, S):
    B, C = z.shape[0], z.shape[1]
    V = SB_TK // SUBLANES
    return z.reshape(B, C, S // SB_TK, SUBLANES, V).swapaxes(3, 4).reshape(B, C, S)


def _swa_kernel(kp_ref, kc_ref, vp_ref, vc_ref, qt_ref, slope_ref, sink_ref, o_ref, *, TQ):
    i = pl.program_id(1)
    kk = jnp.concatenate([kp_ref[...], kc_ref[...]], axis=0)
    vv = jnp.concatenate([vp_ref[...], vc_ref[...]], axis=0)
    qt = qt_ref[...]
    r = lax.broadcasted_iota(i32, (2 * TQ, 2 * TQ), 0)
    ql = lax.broadcasted_iota(i32, (2 * TQ, 2 * TQ), 1) % TQ
    dist = TQ + ql - r
    key_pos = (i - 1) * TQ + r
    mask = (dist >= 0) & (dist < SWA_WINDOW) & (key_pos >= 0)
    distf = dist.astype(f32)
    zero = jnp.zeros((HEAD_DIM, 2 * TQ), bf16)
    heads = []
    for g in range(SWA_KV):
        q2 = jnp.concatenate([qt[(2 * g) * 64:(2 * g + 1) * 64], qt[(2 * g + 1) * 64:(2 * g + 2) * 64]], axis=1)
        qpad = jnp.concatenate([q2, zero], axis=0) if g == 0 else jnp.concatenate([zero, q2], axis=0)
        s = jnp.dot(kk, qpad, preferred_element_type=f32)
        s = jnp.where(mask, s - slope_ref[g:g + 1, :] * distf, NEG)
        sink = sink_ref[g:g + 1, :]
        m = jnp.maximum(jnp.max(s, axis=0, keepdims=True), sink)
        p = jnp.where(mask, jnp.exp(s - m), 0.0)
        den = jnp.sum(p, axis=0, keepdims=True) + jnp.exp(sink - m)
        p = (p / den).astype(bf16)
        o = lax.dot_general(vv, p, (((0,), (0,)), ((), ())), preferred_element_type=f32)
        og = o[64 * g:64 * g + 64]
        heads += [og[:, 0:TQ], og[:, TQ:2 * TQ]]
    o_ref[...] = jnp.concatenate(heads, axis=0).T.astype(bf16)


def swa_attention(kvc, qtc, slope_l, sink_l, *, B, S, TQ=Q_BLOCK):
    NB = S // TQ
    prev = lambda c: (lambda b, i: (b, jnp.maximum(i - 1, 0), c))
    cur = lambda c: (lambda b, i: (b, i, c))
    return pl.pallas_call(
        functools.partial(_swa_kernel, TQ=TQ),
        grid=(B, NB),
        in_specs=[
            pl.BlockSpec((None, TQ, LANES), prev(0)), pl.BlockSpec((None, TQ, LANES), cur(0)),
            pl.BlockSpec((None, TQ, LANES), prev(1)), pl.BlockSpec((None, TQ, LANES), cur(1)),
            pl.BlockSpec((None, C_WIDTH, TQ), lambda b, i: (b, 0, i)),
            pl.BlockSpec((SWA_KV, 2 * TQ), lambda b, i: (0, 0)),
            pl.BlockSpec((SWA_KV, 2 * TQ), lambda b, i: (0, 0)),
        ],
        out_specs=pl.BlockSpec((None, TQ, C_WIDTH), lambda b, i: (b, i, 0)),
        out_shape=jax.ShapeDtypeStruct((B, S, C_WIDTH), bf16),
        compiler_params=_cparams(("arbitrary", "arbitrary")),
        name="swa_attention",
    )(kvc, kvc, kvc, kvc, qtc, slope_l, sink_l)


def _nsa_compress_kernel(ch_ref, w1_ref, pe_ref, w2_ref, gain_ref, o_ref, *, NCP):
    w1 = w1_ref[...]
    p = jnp.dot(ch_ref[...], w1, preferred_element_type=f32)
    pb = jnp.dot(pe_ref[...], w1, preferred_element_type=f32)
    bias = pb[0:1, 0:512] + pb[1:2, 512:1024]
    hid = p[:, 0:512] + pltpu.roll(p[:, 512:1024], NCP - 1, axis=0) + bias
    y = jnp.dot(jax.nn.gelu(hid).astype(bf16), w2_ref[...], preferred_element_type=f32)
    is_k = lax.broadcasted_iota(i32, y.shape, 1) < HEAD_DIM
    ss = jnp.sum(jnp.where(is_k, y * y, 0.0), axis=-1, keepdims=True)
    yn = y * lax.rsqrt(ss * (1.0 / HEAD_DIM) + EPS) * gain_ref[...]
    o_ref[...] = jnp.where(is_k, yn, y).astype(bf16)


def nsa_compress(ch, w1big, pe2, w2big, gain, *, B, S):
    NCP = S // CMP_STRIDE
    W = CMP_STRIDE * LANES
    return pl.pallas_call(
        functools.partial(_nsa_compress_kernel, NCP=NCP),
        grid=(B,),
        in_specs=[
            pl.BlockSpec((None, NCP, W), lambda b: (b, 0, 0)),
            pl.BlockSpec((W, 4 * CMP_HID), lambda b: (0, 0)),
            pl.BlockSpec((SUBLANES, W), lambda b: (0, 0)),
            pl.BlockSpec((2 * CMP_HID, LANES), lambda b: (0, 0)),
            pl.BlockSpec((1, LANES), lambda b: (0, 0)),
        ],
        out_specs=pl.BlockSpec((None, NCP, LANES), lambda b: (b, 0, 0)),
        out_shape=jax.ShapeDtypeStruct((B, NCP, LANES), bf16),
        compiler_params=_cparams(("arbitrary",)),
        name="nsa_compress",
    )(ch, w1big, pe2, w2big, gain)


def _q_all_heads(qt, TQ):
    qr = jnp.concatenate([qt[h * HEAD_DIM:(h + 1) * HEAD_DIM] for h in range(NSA_HEADS)], axis=1)
    return jnp.concatenate([qr, jnp.zeros_like(qr)], axis=0)


def _nsa_select_kernel(qt_ref, kv_ref, slope_ref, ocmp_ref, sel_ref, flag_ref, s_scr, *, TQ, S):
    NSEL = S // SEL_BLOCK
    NC = S // CMP_STRIDE - 1
    R = SEL_BLOCK // CMP_STRIDE
    KSEL = min(SEL_TOPK, NSEL)
    HW = NSA_HEADS * TQ
    i = pl.program_id(1)
    qpad = _q_all_heads(qt_ref[...], TQ)
    slope = slope_ref[...]
    t = i * TQ + lax.broadcasted_iota(i32, (1, HW), 1) % TQ
    j_col = lax.broadcasted_iota(i32, (NSEL, 1), 0)
    m = jnp.full((1, HW), NEG, f32)
    for r in range(R):
        s = jnp.dot(kv_ref[r * NSEL:(r + 1) * NSEL, :], qpad, preferred_element_type=f32)
        c = R * j_col + r
        dist = t - (CMP_STRIDE * c + (CMP_LEN - 1))
        s = jnp.where((dist >= 0) & (c < NC), s - slope * dist.astype(f32), NEG)
        s_scr[r] = s
        m = jnp.maximum(m, jnp.max(s, axis=0, keepdims=True))
    l = jnp.zeros((1, HW), f32)
    for r in range(R):
        s = s_scr[r]
        p = jnp.where(s > 0.5 * NEG, jnp.exp(s - m), 0.0)
        s_scr[r] = p
        l = l + jnp.sum(p, axis=0, keepdims=True)
    inv = 1.0 / jnp.maximum(l, 1e-30)
    o = jnp.zeros((LANES, HW), f32)
    ps = []
    for r in range(R):
        pn = s_scr[r] * inv
        o = o + lax.dot_general(kv_ref[r * NSEL:(r + 1) * NSEL, :], pn.astype(bf16), (((0,), (0,)), ((), ())),
                                preferred_element_type=f32)
        acc = pn[:, 0:TQ]
        for h in range(1, NSA_HEADS):
            acc = acc + pn[:, h * TQ:(h + 1) * TQ]
        ps.append(acc)
    ocmp_ref[...] = o[HEAD_DIM:2 * HEAD_DIM].astype(ocmp_ref.dtype)
    jrow = lax.broadcasted_iota(i32, (NSEL, TQ), 0)
    prev3 = jnp.where(jrow == 0, 0.0, pltpu.roll(ps[3], 1, axis=0))
    imp = 0.5 * prev3 + ps[0] + ps[1] + ps[2] + 0.5 * ps[3]
    tq = i * TQ + lax.broadcasted_iota(i32, (1, TQ), 1)
    cur = tq // SEL_BLOCK
    forced = (jrow == 0) | (jrow == cur) | (jrow == cur - 1)
    causal = jrow <= cur
    score = jnp.where(causal, jnp.where(forced, BIG, imp), -1e38)
    jf = jrow.astype(f32)
    taken = jnp.zeros((NSEL, TQ), jnp.bool_)
    picks = []
    for _ in range(KSEL):
        mx = jnp.max(score, axis=0, keepdims=True)
        idx = jnp.min(jnp.where(score == mx, jf, float(NSEL)), axis=0, keepdims=True)
        hit = jf == idx
        taken = taken | hit
        score = jnp.where(hit, -jnp.inf, score)
        picks.append(idx)
    picks += [picks[-1]] * (SUBLANES - KSEL)
    sel_ref[...] = jnp.concatenate(picks, axis=0).astype(i32)
    used = jnp.where(taken & causal, 1.0, 0.0).astype(bf16)
    flag_ref[...] = lax.dot_general(jnp.ones((SUBLANES, TQ), bf16), used, (((1,), (1,)), ((), ())),
                                    preferred_element_type=f32)


def nsa_select(qta, kvcmp, slope_lane, *, B, S, TQ=Q_BLOCK):
    NB = S // TQ
    NCP = S // CMP_STRIDE
    NSEL = S // SEL_BLOCK
    HW = NSA_HEADS * TQ
    return pl.pallas_call(
        functools.partial(_nsa_select_kernel, TQ=TQ, S=S),
        grid=(B, NB),
        in_specs=[
            pl.BlockSpec((None, A_WIDTH, TQ), lambda b, i: (b, 0, i)),
            pl.BlockSpec((None, NCP, LANES), lambda b, i: (b, 0, 0)),
            pl.BlockSpec((1, HW), lambda b, i: (0, 0)),
        ],
        out_specs=(
            pl.BlockSpec((None, None, HEAD_DIM, HW), lambda b, i: (b, i, 0, 0)),
            pl.BlockSpec((None, SUBLANES, TQ), lambda b, i: (b, 0, i)),
            pl.BlockSpec((None, None, SUBLANES, NSEL), lambda b, i: (b, i, 0, 0)),
        ),
        out_shape=(
            jax.ShapeDtypeStruct((B, NB, HEAD_DIM, HW), bf16),
            jax.ShapeDtypeStruct((B, SUBLANES, S), i32),
            jax.ShapeDtypeStruct((B, NB, SUBLANES, NSEL), f32),
        ),
        scratch_shapes=[pltpu.VMEM((SEL_BLOCK // CMP_STRIDE, NSEL, HW), f32)],
        compiler_params=_cparams(("arbitrary", "arbitrary")),
        name="nsa_select",
    )(qta, kvcmp, slope_lane)


_SLC_CHUNK = 4


def _nsa_attend_kernel(cnt_ref, lst_ref, qt_ref, kslc_ref, kwin_ref, sel_ref, ocmp_ref, gt_ref, slope_ref,
                       o_ref, acc_scr, m_scr, l_scr, *, TQ, S):
    NB = S // TQ
    NSEL = S // SEL_BLOCK
    HW = NSA_HEADS * TQ
    b = pl.program_id(0)
    i = pl.program_id(1)
    count = cnt_ref[b * NB + i]
    qpad = _q_all_heads(qt_ref[...], TQ)
    slope = slope_ref[...]
    tq = i * TQ + lax.broadcasted_iota(i32, (1, HW), 1) % TQ
    sel = sel_ref[...]
    tn = (((0,), (0,)), ((), ()))

    acc_scr[...] = jnp.zeros_like(acc_scr)
    m_scr[...] = jnp.full_like(m_scr, NEG)
    l_scr[...] = jnp.zeros_like(l_scr)

    def chunk(c, carry):
        blks, masks, biases = [], [], []
        for b4 in range(_SLC_CHUNK):
            idx = _SLC_CHUNK * c + b4
            ok = idx < count
            j = jnp.where(ok, lst_ref[0, 0, jnp.minimum(idx, NSEL - 1)], 0)
            blks.append(kslc_ref[pl.ds(pl.multiple_of(j * SEL_BLOCK, SEL_BLOCK), SEL_BLOCK), :])
            picked = jnp.max(jnp.where(sel == j, 1.0, 0.0), axis=0, keepdims=True) > 0.0
            picked = jnp.concatenate([picked & ok] * NSA_HEADS, axis=1)
            dist = tq - (j * SEL_BLOCK + lax.broadcasted_iota(i32, (SEL_BLOCK, 1), 0))
            masks.append((dist >= 0) & picked)
            biases.append(slope * dist.astype(f32))
        kc = jnp.concatenate(blks, axis=0)
        s = jnp.dot(kc, qpad, preferred_element_type=f32)
        parts = [jnp.where(masks[b4], s[b4 * SEL_BLOCK:(b4 + 1) * SEL_BLOCK] - biases[b4], NEG)
                 for b4 in range(_SLC_CHUNK)]
        m_old = m_scr[...]
        m_new = m_old
        for part in parts:
            m_new = jnp.maximum(m_new, jnp.max(part, axis=0, keepdims=True))
        alpha = jnp.exp(m_old - m_new)
        ps = [jnp.where(masks[b4], jnp.exp(parts[b4] - m_new), 0.0) for b4 in range(_SLC_CHUNK)]
        lsum = ps[0].sum(axis=0, keepdims=True)
        for part in ps[1:]:
            lsum = lsum + part.sum(axis=0, keepdims=True)
        l_scr[...] = l_scr[...] * alpha + lsum
        pv = lax.dot_general(kc, jnp.concatenate(ps, axis=0).astype(bf16), tn, preferred_element_type=f32)
        acc_scr[...] = acc_scr[...] * alpha + pv[HEAD_DIM:2 * HEAD_DIM]
        m_scr[...] = m_new
        return carry

    lax.fori_loop(0, (count + _SLC_CHUNK - 1) // _SLC_CHUNK, chunk, 0)
    o_slc = acc_scr[...] / jnp.maximum(l_scr[...], 1e-30)

    NWB = NSA_WINDOW // TQ + 1
    blks, masks, biases = [], [], []
    for w in range(NWB):
        jb = i - (NWB - 1) + w
        blks.append(kwin_ref[pl.ds(pl.multiple_of(jnp.maximum(jb, 0) * TQ, TQ), TQ), :])
        dist = tq - (jb * TQ + lax.broadcasted_iota(i32, (TQ, 1), 0))
        masks.append((dist >= 0) & (dist < NSA_WINDOW) & (jb >= 0))
        biases.append(slope * dist.astype(f32))
    kw = jnp.concatenate(blks, axis=0)
    s = jnp.dot(kw, qpad, preferred_element_type=f32)
    parts = [jnp.where(masks[w], s[w * TQ:(w + 1) * TQ] - biases[w], NEG) for w in range(NWB)]
    m = parts[0].max(axis=0, keepdims=True)
    for part in parts[1:]:
        m = jnp.maximum(m, part.max(axis=0, keepdims=True))
    ps = [jnp.where(masks[w], jnp.exp(parts[w] - m), 0.0) for w in range(NWB)]
    l = ps[0].sum(axis=0, keepdims=True)
    for part in ps[1:]:
        l = l + part.sum(axis=0, keepdims=True)
    pn = (jnp.concatenate(ps, axis=0) * (1.0 / jnp.maximum(l, 1e-30))).astype(bf16)
    o_win = lax.dot_general(kw, pn, tn, preferred_element_type=f32)[HEAD_DIM:2 * HEAD_DIM]

    gt = gt_ref[...]
    gate = [jnp.concatenate([gt[br * NSA_HEADS + h:br * NSA_HEADS + h + 1] for h in range(NSA_HEADS)], axis=1)
            for br in range(3)]
    o = gate[0] * ocmp_ref[...].astype(f32) + gate[1] * o_slc + gate[2] * o_win
    o = jnp.concatenate([o[:, h * TQ:(h + 1) * TQ] for h in range(NSA_HEADS)], axis=0)
    o_ref[...] = o.T.astype(bf16)


def nsa_attend(counts, lists, qta, kva, selT, ocmp, gT, slope_lane, *, B, S, TQ=Q_BLOCK):
    NB = S // TQ
    NSEL = S // SEL_BLOCK
    HW = NSA_HEADS * TQ
    grid_spec = pltpu.PrefetchScalarGridSpec(
        num_scalar_prefetch=1,
        grid=(B, NB),
        in_specs=[
            pl.BlockSpec((1, 1, NSEL), lambda b, i, cnt: (b * NB + i, 0, 0), memory_space=pltpu.SMEM),
            pl.BlockSpec((None, A_WIDTH, TQ), lambda b, i, cnt: (b, 0, i)),
            pl.BlockSpec((None, S, LANES), lambda b, i, cnt: (b, 0, 1)),
            pl.BlockSpec((None, S, LANES), lambda b, i, cnt: (b, 0, 2)),
            pl.BlockSpec((None, SUBLANES, TQ), lambda b, i, cnt: (b, 0, i)),
            pl.BlockSpec((None, None, HEAD_DIM, HW), lambda b, i, cnt: (b, i, 0, 0)),
            pl.BlockSpec((None, 32, TQ), lambda b, i, cnt: (b, 0, i)),
            pl.BlockSpec((1, HW), lambda b, i, cnt: (0, 0)),
        ],
        out_specs=pl.BlockSpec((None, TQ, A_WIDTH), lambda b, i, cnt: (b, i, 0)),
        scratch_shapes=[pltpu.VMEM((HEAD_DIM, HW), f32), pltpu.VMEM((1, HW), f32), pltpu.VMEM((1, HW), f32)],
    )
    return pl.pallas_call(
        functools.partial(_nsa_attend_kernel, TQ=TQ, S=S),
        grid_spec=grid_spec,
        out_shape=jax.ShapeDtypeStruct((B, S, A_WIDTH), bf16),
        compiler_params=_cparams(("arbitrary", "arbitrary")),
        name="nsa_attend",
    )(counts, lists, qta, kva, kva, selT, ocmp, gT, slope_lane)


def nsa_branch(qta, kva, gT, cmp_w, slope_lane, *, B, S):
    NB = S // Q_BLOCK
    NSEL = S // SEL_BLOCK
    NCP = S // CMP_STRIDE
    R = SEL_BLOCK // CMP_STRIDE
    kva = kva.reshape(B, S, 6 * HEAD_DIM)
    ch = kva[:, :, 0:LANES].reshape(B, NCP, CMP_STRIDE * LANES)
    kvcmp = nsa_compress(ch, *cmp_w, B=B, S=S)
    kvcmp = kvcmp.reshape(B, NSEL, R, LANES).swapaxes(1, 2).reshape(B, NCP, LANES)
    ocmp, selT, flags = nsa_select(qta, kvcmp, slope_lane, B=B, S=S)
    used = flags[:, :, 0, :] > 0.5
    ar = jnp.arange(NSEL, dtype=i32)
    lists = jnp.minimum(jnp.sort(jnp.where(used, ar, ar + NSEL), axis=-1), NSEL - 1)
    counts = jnp.sum(used, axis=-1).astype(i32)
    return nsa_attend(counts.reshape(B * NB), lists.reshape(B * NB, 1, NSEL), qta, kva, selT, ocmp, gT, slope_lane,
                      B=B, S=S)


def _route(logit, rb):
    sc = jax.nn.sigmoid(logit)
    sel = sc + rb
    srow = [sel[e:e + 1] for e in range(N_EXPERTS)]
    crow = [sc[e:e + 1] for e in range(N_EXPERTS)]
    gscore = []
    for g in range(N_GROUPS):
        a, b, c, d = srow[4 * g:4 * g + 4]
        top2 = jnp.maximum(jnp.maximum(jnp.maximum(a + b, a + c), jnp.maximum(a + d, b + c)),
                           jnp.maximum(b + d, c + d))
        gscore.append(top2)
    best, gi = gscore[0], jnp.zeros_like(gscore[0], dtype=i32)
    for g in range(1, N_GROUPS):
        better = gscore[g] > best
        gi = jnp.where(better, g, gi)
        best = jnp.where(better, gscore[g], best)

    def pick_group(rows, k):
        v = rows[k]
        for g in range(1, N_GROUPS):
            v = jnp.where(gi == g, rows[4 * g + k], v)
        return v

    iv = [pick_group(srow, k) for k in range(EXPERTS_PER_GROUP)]
    ic = [pick_group(crow, k) for k in range(EXPERTS_PER_GROUP)]
    b1, i1, w1 = iv[0], jnp.zeros_like(gi), ic[0]
    for k in range(1, EXPERTS_PER_GROUP):
        better = iv[k] > b1
        i1 = jnp.where(better, k, i1)
        w1 = jnp.where(better, ic[k], w1)
        b1 = jnp.where(better, iv[k], b1)
    b2 = jnp.full_like(b1, -jnp.inf)
    i2, w2 = jnp.zeros_like(gi), jnp.zeros_like(w1)
    for k in range(EXPERTS_PER_GROUP):
        better = (i1 != k) & (iv[k] > b2)
        i2 = jnp.where(better, k, i2)
        w2 = jnp.where(better, ic[k], w2)
        b2 = jnp.where(better, iv[k], b2)
    tot = w1 + w2
    eidx = jnp.concatenate([gi * EXPERTS_PER_GROUP + i1, gi * EXPERTS_PER_GROUP + i2], axis=0)
    ew = jnp.concatenate([w1 / tot, w2 / tot], axis=0)
    return eidx, ew


def _mix_out_kernel(x_ref, a_ref, sh_ref, ga_ref, oa_ref, ob_ref, oc_ref, wm_ref, wa_ref, wb_ref, wc_ref, wo_ref,
                    am_ref, shm_ref, wr_ref, rb_ref, xo_ref, h2_ref, eidx_ref, ew_ref):
    x = x_ref[...]
    h = _norm_mod(x, a_ref[...], sh_ref[...]).astype(bf16)
    gates = jax.nn.sigmoid(jnp.dot(h, wm_ref[...], preferred_element_type=f32))
    D = D_MODEL
    y = gates[:, 0:D] * jnp.dot(oa_ref[...], wa_ref[...], preferred_element_type=f32)
    y = y + gates[:, D:2 * D] * jnp.dot(ob_ref[...], wb_ref[...], preferred_element_type=f32)
    y = y + gates[:, 2 * D:3 * D] * jnp.dot(oc_ref[...], wc_ref[...], preferred_element_type=f32)
    xn = x + ga_ref[...] * jnp.dot(y.astype(bf16), wo_ref[...], preferred_element_type=f32)
    xo_ref[...] = xn
    h2 = _norm_mod(xn, am_ref[...], shm_ref[...])
    hi = h2.astype(bf16)
    h2_ref[...] = hi
    lo = (h2 - hi.astype(f32)).astype(bf16)
    wr = wr_ref[...]
    whi = wr.astype(bf16)
    wlo = (wr - whi.astype(f32)).astype(bf16)
    nt = lambda p, q: lax.dot_general(p, q, (((1,), (1,)), ((), ())), preferred_element_type=f32)
    logit = nt(whi, hi) + nt(whi, lo) + nt(wlo, hi)
    eidx, ew = _route(logit, rb_ref[...])
    eidx_ref[...] = eidx
    ew_ref[...] = ew


def mix_out(x2, a_mod, sh_mod, g_a, oa, ob, oc, wm, wa, wb, wc, wo, am_moe, shm_moe, wrT, rb, *, B, S, tm=256):
    N, D = x2.shape
    tpb = S // tm
    row = lambda i: (i, 0)
    bmap = lambda i: (i // tpb, 0, 0)
    full = lambda i: (0, 0)
    col = lambda i: (0, i)
    return pl.pallas_call(
        _mix_out_kernel,
        grid=(N // tm,),
        in_specs=[
            pl.BlockSpec((tm, D), row),
            pl.BlockSpec((None, 1, D), bmap), pl.BlockSpec((None, 1, D), bmap), pl.BlockSpec((None, 1, D), bmap),
            pl.BlockSpec((tm, A_WIDTH), row), pl.BlockSpec((tm, B_WIDTH), row), pl.BlockSpec((tm, C_WIDTH), row),
            pl.BlockSpec((D, 3 * D), full), pl.BlockSpec((A_WIDTH, D), full), pl.BlockSpec((B_WIDTH, D), full),
            pl.BlockSpec((C_WIDTH, D), full), pl.BlockSpec((D, D), full),
            pl.BlockSpec((None, 1, D), bmap), pl.BlockSpec((None, 1, D), bmap),
            pl.BlockSpec((N_EXPERTS, D), full), pl.BlockSpec((N_EXPERTS, 1), full),
        ],
        out_specs=(pl.BlockSpec((tm, D), row), pl.BlockSpec((tm, D), row),
                   pl.BlockSpec((TOP_K, tm), col), pl.BlockSpec((TOP_K, tm), col)),
        out_shape=(jax.ShapeDtypeStruct((N, D), f32), jax.ShapeDtypeStruct((N, D), bf16),
                   jax.ShapeDtypeStruct((TOP_K, N), i32), jax.ShapeDtypeStruct((TOP_K, N), f32)),
        compiler_params=_cparams(("arbitrary",)),
        name="mix_out",
    )(x2, a_mod, sh_mod, g_a, oa, ob, oc, wm, wa, wb, wc, wo, am_moe, shm_moe, wrT, rb)


def _moe_ffn_kernel(be_ref, nu_ref, xs_ref, wg_ref, wu_ref, wd_ref, y_ref, wg_s, wu_s, wd_s):
    i = pl.program_id(0)

    @pl.when((i == 0) | (be_ref[i] != be_ref[jnp.maximum(i - 1, 0)]))
    def _():
        wg_s[...] = wg_ref[...].astype(bf16)
        wu_s[...] = wu_ref[...].astype(bf16)
        wd_s[...] = wd_ref[...].astype(bf16)

    @pl.when(i < nu_ref[0])
    def _():
        xs = xs_ref[...]
        g = jnp.dot(xs, wg_s[...], preferred_element_type=f32)
        u = jnp.dot(xs, wu_s[...], preferred_element_type=f32)
        a = (jax.nn.silu(g) * u).astype(bf16)
        y_ref[...] = jnp.dot(a, wd_s[...], preferred_element_type=f32).astype(y_ref.dtype)

    @pl.when(i >= nu_ref[0])
    def _():
        y_ref[...] = jnp.zeros_like(y_ref)


def moe_ffn(blk_e, n_used, xs, wg, wu, wd):
    n_rows, D = xs.shape
    n_blk = n_rows // MOE_BLOCK
    grid_spec = pltpu.PrefetchScalarGridSpec(
        num_scalar_prefetch=2,
        grid=(n_blk,),
        in_specs=[
            pl.BlockSpec((MOE_BLOCK, D), lambda i, be, nu: (i, 0)),
            pl.BlockSpec((None, D, D_EXPERT), lambda i, be, nu: (be[i], 0, 0)),
            pl.BlockSpec((None, D, D_EXPERT), lambda i, be, nu: (be[i], 0, 0)),
            pl.BlockSpec((None, D_EXPERT, D), lambda i, be, nu: (be[i], 0, 0)),
        ],
        out_specs=pl.BlockSpec((MOE_BLOCK, D), lambda i, be, nu: (i, 0)),
        scratch_shapes=[pltpu.VMEM((D, D_EXPERT), bf16), pltpu.VMEM((D, D_EXPERT), bf16),
                        pltpu.VMEM((D_EXPERT, D), bf16)],
    )
    return pl.pallas_call(
        _moe_ffn_kernel,
        grid_spec=grid_spec,
        out_shape=jax.ShapeDtypeStruct((n_rows, D), bf16),
        compiler_params=_cparams(("arbitrary",)),
        name="moe_ffn",
    )(blk_e, n_used, xs, wg, wu, wd)


def moe_dispatch(eidx, N):
    NK = N * TOP_K
    n_blk = -(-NK // MOE_BLOCK) + N_EXPERTS
    experts = jnp.arange(N_EXPERTS, dtype=i32)
    flat_e = eidx.T.reshape(-1)
    order = jnp.argsort(flat_e).astype(i32)
    rank = jnp.argsort(order).astype(i32)
    onehot = flat_e[:, None] == experts[None, :]
    counts = jnp.sum(onehot, axis=0, dtype=i32)
    padded = (counts + MOE_BLOCK - 1) // MOE_BLOCK * MOE_BLOCK
    p_end = jnp.cumsum(padded)
    p_start = p_end - padded
    start = jnp.cumsum(counts) - counts
    blk_first = jnp.arange(n_blk, dtype=i32) * MOE_BLOCK
    blk_e = jnp.minimum(jnp.sum(p_end[None, :] <= blk_first[:, None], axis=1, dtype=i32), N_EXPERTS - 1)
    n_used = (p_end[-1] // MOE_BLOCK).astype(i32).reshape(1)
    off = (blk_first - p_start[blk_e])[:, None] + jnp.arange(MOE_BLOCK, dtype=i32)[None, :]
    valid = off < counts[blk_e][:, None]
    src = jnp.clip(start[blk_e][:, None] + off, 0, NK - 1)
    slot_tok = jnp.where(valid, order[src] // TOP_K, N - 1).reshape(-1)
    pos = jnp.sum(jnp.where(onehot, (p_start - start)[None, :], 0), axis=1, dtype=i32) + rank
    return slot_tok, blk_e, n_used, pos.reshape(N, TOP_K)


def _prep_cmp_weights(pe_k, w1_k, w2_k, pe_v, w1_v, w2_v, k_gain0):
    T = CMP_STRIDE
    w1big = jnp.zeros((T, 2, HEAD_DIM, 4, CMP_HID), f32)
    w1k = w1_k.reshape(2, T, HEAD_DIM, CMP_HID)
    w1v = w1_v.reshape(2, T, HEAD_DIM, CMP_HID)
    w1big = w1big.at[:, 0, :, 0].set(w1k[0]).at[:, 1, :, 1].set(w1v[0])
    w1big = w1big.at[:, 0, :, 2].set(w1k[1]).at[:, 1, :, 3].set(w1v[1])
    w1big = w1big.reshape(T * LANES, 4 * CMP_HID).astype(bf16)
    pe = jnp.stack([pe_k.reshape(2, T, HEAD_DIM), pe_v.reshape(2, T, HEAD_DIM)], axis=2)
    pe2 = jnp.zeros((SUBLANES, T * LANES), f32).at[0:2].set(pe.reshape(2, T * LANES)).astype(bf16)
    w2big = jnp.zeros((2 * CMP_HID, LANES), f32)
    w2big = w2big.at[0:CMP_HID, 0:HEAD_DIM].set(w2_k).at[CMP_HID:, HEAD_DIM:].set(w2_v).astype(bf16)
    gain = jnp.concatenate([k_gain0, jnp.ones((HEAD_DIM,), f32)])[None, :]
    return w1big, pe2, w2big, gain


def _prep_proj_weights(w_in, nsa_q_gain, nsa_k_gain, swa_q_gain, swa_k_gain):
    w = w_in
    wtok = jnp.concatenate([w[:, _O_KVA:_O_GA], w[:, _O_KB:_O_VB], w[:, _O_KC:_O_MG]], axis=1).astype(bf16)
    gate_perm = np.array([h * 3 + br for br in range(3) for h in range(NSA_HEADS)])
    scale = HEAD_DIM ** -0.5
    wch = jnp.concatenate([w[:, _O_QA:_O_KVA], w[:, _O_QB:_O_KB] * (0.5 * scale), w[:, _O_QC:_O_KC],
                           w[:, _O_VB:_O_QC], w[:, _O_GA:_O_QB][:, gate_perm], jnp.zeros((D_MODEL, 8), f32)],
                          axis=1).T.astype(bf16)
    gtok = jnp.zeros((1, TOK_W), f32)
    gtok = gtok.at[0, 128:192].set(nsa_k_gain[1]).at[0, 256:320].set(nsa_k_gain[2])
    gtok = gtok.at[0, 640:704].set(swa_k_gain).at[0, 704:768].set(swa_k_gain)
    mtok = jnp.zeros((1, TOK_W), f32).at[0, 128:192].set(1.0).at[0, 256:320].set(1.0).at[0, 640:768].set(1.0)
    gcha = (jnp.tile(nsa_q_gain, NSA_HEADS) * scale)[:, None]
    gchc = (jnp.tile(swa_q_gain, SWA_HEADS) * scale)[:, None]
    blk = np.arange(LANES) // HEAD_DIM
    bd = jnp.asarray(blk[:, None] == blk[None, :], bf16)
    return wtok, wch, gtok, mtok, gcha, gchc, bd


def _adaln_kernel(c_ref, w_ref, b_ref, o_ref):
    cond = jax.nn.silu(c_ref[...]).astype(bf16)
    o_ref[...] = jnp.dot(cond, w_ref[...].astype(bf16), preferred_element_type=f32) + b_ref[...]


def adaln(c_pad, w_ada, b_ada):
    L, D, D6 = w_ada.shape
    return pl.pallas_call(
        _adaln_kernel,
        grid=(L, D6 // D),
        in_specs=[
            pl.BlockSpec((SUBLANES, D), lambda l, j: (0, 0)),
            pl.BlockSpec((None, D, D), lambda l, j: (l, 0, j)),
            pl.BlockSpec((None, 1, D), lambda l, j: (l, 0, j)),
        ],
        out_specs=pl.BlockSpec((None, SUBLANES, D), lambda l, j: (l, 0, j)),
        out_shape=jax.ShapeDtypeStruct((L, SUBLANES, D6), f32),
        compiler_params=_cparams(("arbitrary", "arbitrary")),
        name="adaln",
    )(c_pad, w_ada, b_ada.reshape(L, 1, D6))


def _moe_combine_kernel(x_ref, y0_ref, y1_ref, w_ref, gm_ref, o_ref):
    w = w_ref[...]
    y = y0_ref[...].astype(f32) * w[:, 0:1] + y1_ref[...].astype(f32) * w[:, 1:2]
    o_ref[...] = x_ref[...] + gm_ref[...] * y


def moe_combine(x2, y0, y1, ew_tok, g_m, *, B, S, tm=512):
    N, D = x2.shape
    tpb = S // tm
    row = lambda i: (i, 0)
    return pl.pallas_call(
        _moe_combine_kernel,
        grid=(N // tm,),
        in_specs=[pl.BlockSpec((tm, D), row), pl.BlockSpec((tm, D), row), pl.BlockSpec((tm, D), row),
                  pl.BlockSpec((tm, TOP_K), row), pl.BlockSpec((None, 1, D), lambda i: (i // tpb, 0, 0))],
        out_specs=pl.BlockSpec((tm, D), row),
        out_shape=jax.ShapeDtypeStruct((N, D), f32),
        compiler_params=_cparams(("arbitrary",)),
        name="moe_combine",
    )(x2, y0, y1, ew_tok, g_m)


def kernel(x, c, w_ada, b_ada, g_norm_mix, g_norm_moe, w_in, cmp_pe_k, cmp_w1_k, cmp_w2_k, cmp_pe_v, cmp_w1_v,
           cmp_w2_v, nsa_q_gain, nsa_k_gain, swa_q_gain, swa_k_gain, swa_sinks, w_branch_a, w_branch_b,
           w_branch_c, w_out, w_router, router_bias, w_exp_gate, w_exp_up, w_exp_down):
    B, S, D = x.shape
    N = B * S
    L = w_ada.shape[0]
    slopes = 2.0 ** (-8.0 * jnp.arange(1, SWA_HEADS + NSA_HEADS + 1, dtype=f32) / (SWA_HEADS + NSA_HEADS))
    swa_slope_l = jnp.repeat(slopes[:SWA_HEADS].reshape(SWA_KV, SWA_HEADS // SWA_KV), Q_BLOCK, axis=1)
    nsa_slope_l = jnp.repeat(slopes[SWA_HEADS:], Q_BLOCK)[None, :]
    c_pad = jnp.zeros((SUBLANES, D), f32).at[:B].set(c)
    mod_all = adaln(c_pad, w_ada, b_ada)[:, :B]
    wrT = w_router.T
    rb = router_bias[:, None]
    x2 = x.reshape(N, D)
    for l in range(L):
        sh_a, sc_a, g_a, sh_m, sc_m, g_m = [m[:, None, :] for m in jnp.split(mod_all[l], 6, axis=-1)]
        a_mix = g_norm_mix[l][None, None, :] * (1.0 + sc_a)
        a_moe = g_norm_moe[l][None, None, :] * (1.0 + sc_m)
        pw = _prep_proj_weights(w_in[l], nsa_q_gain[l], nsa_k_gain[l], swa_q_gain[l], swa_k_gain[l])
        kva, kb, kvc, qta, qtb, qtc, vtb, gT = proj_in(x2, a_mix, sh_a, *pw, B=B, S=S)
        cw = _prep_cmp_weights(cmp_pe_k[l], cmp_w1_k[l], cmp_w2_k[l], cmp_pe_v[l], cmp_w1_v[l], cmp_w2_v[l],
                               nsa_k_gain[l][0])
        o_a = nsa_branch(qta, kva, gT, cw, nsa_slope_l, B=B, S=S)
        o_b = sb_attention(_perm_rows(kb.reshape(B, S, B_WIDTH), S), _perm_lanes(vtb, S), qtb, B=B, S=S)
        sink_l = jnp.repeat(swa_sinks[l].reshape(SWA_KV, SWA_HEADS // SWA_KV), Q_BLOCK, axis=1)
        o_c = swa_attention(kvc.reshape(B, S, 2 * SWA_KV * HEAD_DIM), qtc, swa_slope_l, sink_l, B=B, S=S)
        x2, h2, eidx, ew = mix_out(
            x2, a_mix, sh_a, g_a, o_a.reshape(N, A_WIDTH), o_b.reshape(N, B_WIDTH), o_c.reshape(N, C_WIDTH),
            w_in[l][:, _O_MG:_O_END].astype(bf16), w_branch_a[l].astype(bf16), w_branch_b[l].astype(bf16),
            w_branch_c[l].astype(bf16), w_out[l].astype(bf16), a_moe, sh_m, wrT, rb, B=B, S=S)
        slot_tok, blk_e, n_used, pos = moe_dispatch(eidx, N)
        y = moe_ffn(blk_e, n_used, h2[slot_tok], w_exp_gate[l], w_exp_up[l], w_exp_down[l])
        x2 = moe_combine(x2, y[pos[:, 0]], y[pos[:, 1]], ew.T, g_m, B=B, S=S)
    return x2.reshape(B, S, D)
```

```python
import functools

import numpy as np
import jax
import jax.numpy as jnp
from jax import lax
from jax.experimental import pallas as pl
from jax.experimental.pallas import tpu as pltpu

f32 = jnp.float32
bf16 = jnp.bfloat16
i32 = jnp.int32

D_MODEL = 1024
HEAD_DIM = 64
Q_BLOCK = 128
NSA_HEADS = 8
CMP_STRIDE = 16
CMP_LEN = 32
CMP_HID = 256
SEL_BLOCK = 64
SEL_TOPK = 8
NSA_WINDOW = 512
SB_HEADS = 4
SWA_HEADS = 4
SWA_KV = 2
SWA_WINDOW = 128
N_EXPERTS = 16
N_GROUPS = 4
EXPERTS_PER_GROUP = 4
TOP_K = 2
D_EXPERT = 512
MOE_BLOCK = 512
EPS = 1e-6
NEG = -1e30
BIG = 1e30

A_WIDTH = NSA_HEADS * HEAD_DIM
B_WIDTH = SB_HEADS * HEAD_DIM
C_WIDTH = SWA_HEADS * HEAD_DIM

LANES = 128
SUBLANES = 8
VMEM_LIMIT = 56 * 1024 * 1024

_O_QA = 0
_O_KVA = _O_QA + A_WIDTH
_O_GA = _O_KVA + 6 * HEAD_DIM
_O_QB = _O_GA + 3 * NSA_HEADS
_O_KB = _O_QB + B_WIDTH
_O_VB = _O_KB + B_WIDTH
_O_QC = _O_VB + B_WIDTH
_O_KC = _O_QC + C_WIDTH
_O_VC = _O_KC + SWA_KV * HEAD_DIM
_O_MG = _O_VC + SWA_KV * HEAD_DIM
_O_END = _O_MG + 3 * D_MODEL

TOK_W = 6 * HEAD_DIM + B_WIDTH + 2 * SWA_KV * HEAD_DIM
CH_W = A_WIDTH + B_WIDTH + C_WIDTH + B_WIDTH + 32
_TOK_NORM_CHUNKS = (1, 2, 5)


def _cparams(sem):
    return pltpu.CompilerParams(dimension_semantics=sem, vmem_limit_bytes=VMEM_LIMIT)


def _split_dot(x, w):
    hi = x.astype(bf16)
    lo = (x - hi.astype(f32)).astype(bf16)
    return jnp.dot(hi, w, preferred_element_type=f32) + jnp.dot(lo, w, preferred_element_type=f32)


def _norm_mod(x, a, sh):
    rs = lax.rsqrt(jnp.mean(x * x, axis=-1, keepdims=True) + EPS)
    return x * rs * a + sh


def _group_norm_ch(y, gain_col, n, tm):
    y3 = y.reshape(n, HEAD_DIM, tm)
    ss = jnp.sum(y3 * y3, axis=1, keepdims=True)
    yn = y3 * lax.rsqrt(ss * (1.0 / HEAD_DIM) + EPS)
    return yn.reshape(n * HEAD_DIM, tm) * gain_col


def _proj_in_kernel(x_ref, a_ref, sh_ref, wtok_ref, wch_ref, gtok_ref, mtok_ref, gcha_ref, gchc_ref, bd_ref,
                    kva_ref, kb_ref, kvc_ref, qta_ref, qtb_ref, qtc_ref, vtb_ref, gt_ref, *, tm):
    h = _norm_mod(x_ref[...], a_ref[...], sh_ref[...]).astype(bf16)
    t = jnp.dot(h, wtok_ref[...], preferred_element_type=f32)
    cols = [t[:, c * LANES:(c + 1) * LANES] for c in range(TOK_W // LANES)]
    for c in _TOK_NORM_CHUNKS:
        y = cols[c]
        ss = _split_dot(y * y, bd_ref[...])
        yn = y * lax.rsqrt(ss * (1.0 / HEAD_DIM) + EPS) * gtok_ref[:, c * LANES:(c + 1) * LANES]
        cols[c] = jnp.where(mtok_ref[:, c * LANES:(c + 1) * LANES] > 0.0, yn, y)
    t = jnp.concatenate(cols, axis=1).astype(bf16)
    kva_ref[...] = t[:, 0:384]
    kb_ref[...] = t[:, 384:640]
    kvc_ref[...] = t[:, 640:896]
    c = lax.dot_general(wch_ref[...], h, (((1,), (1,)), ((), ())), preferred_element_type=f32)
    qta_ref[...] = _group_norm_ch(c[0:512], gcha_ref[...], NSA_HEADS, tm).astype(bf16)
    qtb_ref[...] = c[512:768].astype(bf16)
    qtc_ref[...] = _group_norm_ch(c[768:1024], gchc_ref[...], SWA_HEADS, tm).astype(bf16)
    vtb_ref[...] = c[1024:1280].astype(bf16)
    gt_ref[...] = jax.nn.sigmoid(c[1280:1312])


def proj_in(x2, a_mod, sh_mod, wtok, wch, gtok, mtok, gcha, gchc, bd, *, B, S, tm=512):
    N, D = x2.shape
    tpb = S // tm
    row = lambda i: (i, 0)
    bmap = lambda i: (i // tpb, 0, 0)
    cmap = lambda i: (i // tpb, 0, i % tpb)
    full = lambda i: (0, 0)
    out_shape = (
        jax.ShapeDtypeStruct((N, 384), bf16), jax.ShapeDtypeStruct((N, 256), bf16), jax.ShapeDtypeStruct((N, 256), bf16),
        jax.ShapeDtypeStruct((B, 512, S), bf16), jax.ShapeDtypeStruct((B, 256, S), bf16),
        jax.ShapeDtypeStruct((B, 256, S), bf16), jax.ShapeDtypeStruct((B, 256, S), bf16),
        jax.ShapeDtypeStruct((B, 32, S), f32),
    )
    return pl.pallas_call(
        functools.partial(_proj_in_kernel, tm=tm),
        grid=(N // tm,),
        in_specs=[
            pl.BlockSpec((tm, D), row),
            pl.BlockSpec((None, 1, D), bmap), pl.BlockSpec((None, 1, D), bmap),
            pl.BlockSpec((D, TOK_W), full), pl.BlockSpec((CH_W, D), full),
            pl.BlockSpec((1, TOK_W), full), pl.BlockSpec((1, TOK_W), full),
            pl.BlockSpec((512, 1), full), pl.BlockSpec((256, 1), full),
            pl.BlockSpec((LANES, LANES), full),
        ],
        out_specs=(
            pl.BlockSpec((tm, 384), row), pl.BlockSpec((tm, 256), row), pl.BlockSpec((tm, 256), row),
            pl.BlockSpec((None, 512, tm), cmap), pl.BlockSpec((None, 256, tm), cmap),
            pl.BlockSpec((None, 256, tm), cmap), pl.BlockSpec((None, 256, tm), cmap),
            pl.BlockSpec((None, 32, tm), cmap),
        ),
        out_shape=out_shape,
        compiler_params=_cparams(("arbitrary",)),
        name="proj_in",
    )(x2, a_mod, sh_mod, wtok, wch, gtok, mtok, gcha, gchc, bd)


def _excl_suffix_prod8(t, rows):
    def shift_up(x, k):
        return jnp.where(rows < SUBLANES - k, pltpu.roll(x, SUBLANES - k, axis=0), 1.0)
    x = shift_up(t, 1)
    x = x * shift_up(x, 1)
    x = x * shift_up(x, 2)
    x = x * shift_up(x, 4)
    return x


SB_TK = 128
SB_TQ = 512


def _sb_scores(k_ref, qpad, z_scr, j, slot):
    kb = k_ref[pl.ds(pl.multiple_of(j * SB_TK, SB_TK), SB_TK), :]
    for h in range(2):
        z_scr[slot, h] = jnp.dot(kb, qpad[h], preferred_element_type=f32)


def _sb_weights(z_scr, p_scr, a_scr, slot, laters, mask_off, TQ):
    V = SB_TK // SUBLANES
    rows = lax.broadcasted_iota(i32, (SUBLANES, TQ), 0)
    qidx = lax.broadcasted_iota(i32, (SUBLANES, TQ), 1)
    new_laters = []
    for h in range(2):
        run = jnp.ones((SUBLANES, TQ), f32)
        for v in reversed(range(V)):
            sl = slice(v * SUBLANES, (v + 1) * SUBLANES)
            beta = 0.5 - 0.5 * jnp.tanh(z_scr[slot, h, sl, :])
            if mask_off is not None:
                beta = jnp.where(rows * V + (v + mask_off) < qidx, beta, 1.0)
            nxt = run * beta
            p_scr[h, sl, :] = run - nxt
            run = nxt
        g = _excl_suffix_prod8(run, rows) * laters[h]
        for v2 in range(V // 2):
            sl = slice(2 * v2 * SUBLANES, (2 * v2 + 2) * SUBLANES)
            a_scr[slot, h, sl, :] = (p_scr[h, sl, :] * jnp.concatenate([g, g], axis=0)).astype(bf16)
        new_laters.append((g * run)[0:1, :])
    return new_laters


def _sb_accumulate(vt_ref, a_scr, acc_scr, j, slot):
    vtb = vt_ref[:, pl.ds(pl.multiple_of(j * SB_TK, SB_TK), SB_TK)]
    for h in range(2):
        acc_scr[h] += jnp.dot(vtb[h * HEAD_DIM:(h + 1) * HEAD_DIM], a_scr[slot, h], preferred_element_type=f32)


def _sb_kernel(k_ref, vt_ref, qt_ref, o_ref, acc_scr, z_scr, p_scr, a_scr, qpad, *, TQ):
    NBQ = TQ // SB_TK
    i = pl.program_id(2)
    qpad[...] = jnp.zeros_like(qpad)
    qpad[0, 0:HEAD_DIM, :] = qt_ref[0:HEAD_DIM, :]
    qpad[1, HEAD_DIM:2 * HEAD_DIM, :] = qt_ref[HEAD_DIM:2 * HEAD_DIM, :]
    acc_scr[...] = jnp.zeros_like(acc_scr)
    a_scr[1] = jnp.zeros_like(a_scr[1])
    j0 = i * NBQ + NBQ - 1

    def step(j, laters, mask_offs):
        _sb_scores(k_ref, qpad, z_scr, j - 1, 1)
        _sb_accumulate(vt_ref, a_scr, acc_scr, jnp.minimum(j + 1, j0), 1)
        laters = _sb_weights(z_scr, p_scr, a_scr, 0, laters, mask_offs[0], TQ)
        _sb_scores(k_ref, qpad, z_scr, jnp.maximum(j - 2, 0), 0)
        _sb_accumulate(vt_ref, a_scr, acc_scr, j, 0)
        laters = _sb_weights(z_scr, p_scr, a_scr, 1, laters, mask_offs[1], TQ)
        return laters

    _sb_scores(k_ref, qpad, z_scr, j0, 0)
    laters = [jnp.ones((1, TQ), f32)] * 2
    for d in range(NBQ // 2):
        jd = NBQ - 1 - 2 * d
        laters = step(i * NBQ + jd, laters, (jd * SB_TK, (jd - 1) * SB_TK))

    def body(t, carry):
        return tuple(step(i * NBQ - 1 - 2 * t, list(carry), (None, None)))

    lax.fori_loop(0, i * (NBQ // 2), body, tuple(laters))
    _sb_accumulate(vt_ref, a_scr, acc_scr, 0, 1)
    o_ref[...] = jnp.concatenate([acc_scr[0].T, acc_scr[1].T], axis=1).astype(bf16)


def sb_attention(kb_perm, vtb_perm, qtb, *, B, S, TQ=SB_TQ):
    TQ = min(TQ, S)
    return pl.pallas_call(
        functools.partial(_sb_kernel, TQ=TQ),
        grid=(B, SB_HEADS // 2, S // TQ),
        in_specs=[
            pl.BlockSpec((None, S, LANES), lambda b, hp, i: (b, 0, hp)),
            pl.BlockSpec((None, LANES, S), lambda b, hp, i: (b, hp, 0)),
            pl.BlockSpec((None, LANES, TQ), lambda b, hp, i: (b, hp, i)),
        ],
        out_specs=pl.BlockSpec((None, TQ, LANES), lambda b, hp, i: (b, i, hp)),
        out_shape=jax.ShapeDtypeStruct((B, S, B_WIDTH), bf16),
        scratch_shapes=[pltpu.VMEM((2, HEAD_DIM, TQ), f32), pltpu.VMEM((2, 2, SB_TK, TQ), f32),
                        pltpu.VMEM((2, SB_TK, TQ), f32), pltpu.VMEM((2, 2, SB_TK, TQ), bf16),
                        pltpu.VMEM((2, 2 * HEAD_DIM, TQ), bf16)],
        compiler_params=_cparams(("arbitrary", "arbitrary", "arbitrary")),
        name="sb_attention",
    )(kb_perm, vtb_perm, qtb)


def _perm_rows(z, S):
    B = z.shape[0]
    V = SB_TK // SUBLANES
    return z.reshape(B, S // SB_TK, SUBLANES, V, z.shape[-1]).swapaxes(2, 3).reshape(B, S, z.shape[-1])


def _perm_lanes(z, S):
    B, C = z.shape[0], z.shape[1]
    V = SB_TK // SUBLANES
    return z.reshape(B, C, S // SB_TK, SUBLANES, V).swapaxes(3, 4).reshape(B, C, S)


def _swa_kernel(kp_ref, kc_ref, vp_ref, vc_ref, qt_ref, slope_ref, sink_ref, o_ref, *, TQ):
    i = pl.program_id(1)
    kk = jnp.concatenate([kp_ref[...], kc_ref[...]], axis=0)
    vv = jnp.concatenate([vp_ref[...], vc_ref[...]], axis=0)
    qt = qt_ref[...]
    r = lax.broadcasted_iota(i32, (2 * TQ, 2 * TQ), 0)
    ql = lax.broadcasted_iota(i32, (2 * TQ, 2 * TQ), 1) % TQ
    dist = TQ + ql - r
    key_pos = (i - 1) * TQ + r
    mask = (dist >= 0) & (dist < SWA_WINDOW) & (key_pos >= 0)
    distf = dist.astype(f32)
    zero = jnp.zeros((HEAD_DIM, 2 * TQ), bf16)
    heads = []
    for g in range(SWA_KV):
        q2 = jnp.concatenate([qt[(2 * g) * 64:(2 * g + 1) * 64], qt[(2 * g + 1) * 64:(2 * g + 2) * 64]], axis=1)
        qpad = jnp.concatenate([q2, zero], axis=0) if g == 0 else jnp.concatenate([zero, q2], axis=0)
        s = jnp.dot(kk, qpad, preferred_element_type=f32)
        s = jnp.where(mask, s - slope_ref[g:g + 1, :] * distf, NEG)
        sink = sink_ref[g:g + 1, :]
        m = jnp.maximum(jnp.max(s, axis=0, keepdims=True), sink)
        p = jnp.where(mask, jnp.exp(s - m), 0.0)
        den = jnp.sum(p, axis=0, keepdims=True) + jnp.exp(sink - m)
        p = (p / den).astype(bf16)
        o = lax.dot_general(vv, p, (((0,), (0,)), ((), ())), preferred_element_type=f32)
        og = o[64 * g:64 * g + 64]
        heads += [og[:, 0:TQ], og[:, TQ:2 * TQ]]
    o_ref[...] = jnp.concatenate(heads, axis=0).T.astype(bf16)


def swa_attention(kvc, qtc, slope_l, sink_l, *, B, S, TQ=Q_BLOCK):
    NB = S // TQ
    prev = lambda c: (lambda b, i: (b, jnp.maximum(i - 1, 0), c))
    cur = lambda c: (lambda b, i: (b, i, c))
    return pl.pallas_call(
        functools.partial(_swa_kernel, TQ=TQ),
        grid=(B, NB),
        in_specs=[
            pl.BlockSpec((None, TQ, LANES), prev(0)), pl.BlockSpec((None, TQ, LANES), cur(0)),
            pl.BlockSpec((None, TQ, LANES), prev(1)), pl.BlockSpec((None, TQ, LANES), cur(1)),
            pl.BlockSpec((None, C_WIDTH, TQ), lambda b, i: (b, 0, i)),
            pl.BlockSpec((SWA_KV, 2 * TQ), lambda b, i: (0, 0)),
            pl.BlockSpec((SWA_KV, 2 * TQ), lambda b, i: (0, 0)),
        ],
        out_specs=pl.BlockSpec((None, TQ, C_WIDTH), lambda b, i: (b, i, 0)),
        out_shape=jax.ShapeDtypeStruct((B, S, C_WIDTH), bf16),
        compiler_params=_cparams(("arbitrary", "arbitrary")),
        name="swa_attention",
    )(kvc, kvc, kvc, kvc, qtc, slope_l, sink_l)


def _nsa_compress_kernel(ch_ref, w1_ref, pe_ref, w2_ref, gain_ref, o_ref, *, NCP):
    w1 = w1_ref[...]
    p = jnp.dot(ch_ref[...], w1, preferred_element_type=f32)
    pb = jnp.dot(pe_ref[...], w1, preferred_element_type=f32)
    bias = pb[0:1, 0:512] + pb[1:2, 512:1024]
    hid = p[:, 0:512] + pltpu.roll(p[:, 512:1024], NCP - 1, axis=0) + bias
    y = jnp.dot(jax.nn.gelu(hid).astype(bf16), w2_ref[...], preferred_element_type=f32)
    is_k = lax.broadcasted_iota(i32, y.shape, 1) < HEAD_DIM
    ss = jnp.sum(jnp.where(is_k, y * y, 0.0), axis=-1, keepdims=True)
    yn = y * lax.rsqrt(ss * (1.0 / HEAD_DIM) + EPS) * gain_ref[...]
    o_ref[...] = jnp.where(is_k, yn, y).astype(bf16)


def nsa_compress(ch, w1big, pe2, w2big, gain, *, B, S):
    NCP = S // CMP_STRIDE
    W = CMP_STRIDE * LANES
    return pl.pallas_call(
        functools.partial(_nsa_compress_kernel, NCP=NCP),
        grid=(B,),
        in_specs=[
            pl.BlockSpec((None, NCP, W), lambda b: (b, 0, 0)),
            pl.BlockSpec((W, 4 * CMP_HID), lambda b: (0, 0)),
            pl.BlockSpec((SUBLANES, W), lambda b: (0, 0)),
            pl.BlockSpec((2 * CMP_HID, LANES), lambda b: (0, 0)),
            pl.BlockSpec((1, LANES), lambda b: (0, 0)),
        ],
        out_specs=pl.BlockSpec((None, NCP, LANES), lambda b: (b, 0, 0)),
        out_shape=jax.ShapeDtypeStruct((B, NCP, LANES), bf16),
        compiler_params=_cparams(("arbitrary",)),
        name="nsa_compress",
    )(ch, w1big, pe2, w2big, gain)


def _q_all_heads(qt, TQ):
    qr = jnp.concatenate([qt[h * HEAD_DIM:(h + 1) * HEAD_DIM] for h in range(NSA_HEADS)], axis=1)
    return jnp.concatenate([qr, jnp.zeros_like(qr)], axis=0)


def _nsa_select_kernel(qt_ref, kv_ref, slope_ref, ocmp_ref, sel_ref, flag_ref, s_scr, *, TQ, S):
    NSEL = S // SEL_BLOCK
    NC = S // CMP_STRIDE - 1
    R = SEL_BLOCK // CMP_STRIDE
    KSEL = min(SEL_TOPK, NSEL)
    HW = NSA_HEADS * TQ
    i = pl.program_id(1)
    qpad = _q_all_heads(qt_ref[...], TQ)
    slope = slope_ref[...]
    t = i * TQ + lax.broadcasted_iota(i32, (1, HW), 1) % TQ
    j_col = lax.broadcasted_iota(i32, (NSEL, 1), 0)
    m = jnp.full((1, HW), NEG, f32)
    for r in range(R):
        s = jnp.dot(kv_ref[r * NSEL:(r + 1) * NSEL, :], qpad, preferred_element_type=f32)
        c = R * j_col + r
        dist = t - (CMP_STRIDE * c + (CMP_LEN - 1))
        s = jnp.where((dist >= 0) & (c < NC), s - slope * dist.astype(f32), NEG)
        s_scr[r] = s
        m = jnp.maximum(m, jnp.max(s, axis=0, keepdims=True))
    l = jnp.zeros((1, HW), f32)
    for r in range(R):
        s = s_scr[r]
        p = jnp.where(s > 0.5 * NEG, jnp.exp(s - m), 0.0)
        s_scr[r] = p
        l = l + jnp.sum(p, axis=0, keepdims=True)
    inv = 1.0 / jnp.maximum(l, 1e-30)
    o = jnp.zeros((LANES, HW), f32)
    ps = []
    for r in range(R):
        pn = s_scr[r] * inv
        o = o + lax.dot_general(kv_ref[r * NSEL:(r + 1) * NSEL, :], pn.astype(bf16), (((0,), (0,)), ((), ())),
                                preferred_element_type=f32)
        acc = pn[:, 0:TQ]
        for h in range(1, NSA_HEADS):
            acc = acc + pn[:, h * TQ:(h + 1) * TQ]
        ps.append(acc)
    ocmp_ref[...] = o[HEAD_DIM:2 * HEAD_DIM].astype(ocmp_ref.dtype)
    jrow = lax.broadcasted_iota(i32, (NSEL, TQ), 0)
    prev3 = jnp.where(jrow == 0, 0.0, pltpu.roll(ps[3], 1, axis=0))
    imp = 0.5 * prev3 + ps[0] + ps[1] + ps[2] + 0.5 * ps[3]
    tq = i * TQ + lax.broadcasted_iota(i32, (1, TQ), 1)
    cur = tq // SEL_BLOCK
    forced = (jrow == 0) | (jrow == cur) | (jrow == cur - 1)
    causal = jrow <= cur
    score = jnp.where(causal, jnp.where(forced, BIG, imp), -1e38)
    jf = jrow.astype(f32)
    taken = jnp.zeros((NSEL, TQ), jnp.bool_)
    picks = []
    for _ in range(KSEL):
        mx = jnp.max(score, axis=0, keepdims=True)
        idx = jnp.min(jnp.where(score == mx, jf, float(NSEL)), axis=0, keepdims=True)
        hit = jf == idx
        taken = taken | hit
        score = jnp.where(hit, -jnp.inf, score)
        picks.append(idx)
    picks += [picks[-1]] * (SUBLANES - KSEL)
    sel_ref[...] = jnp.concatenate(picks, axis=0).astype(i32)
    used = jnp.where(taken & causal, 1.0, 0.0).astype(bf16)
    flag_ref[...] = lax.dot_general(jnp.ones((SUBLANES, TQ), bf16), used, (((1,), (1,)), ((), ())),
                                    preferred_element_type=f32)


def nsa_select(qta, kvcmp, slope_lane, *, B, S, TQ=Q_BLOCK):
    NB = S // TQ
    NCP = S // CMP_STRIDE
    NSEL = S // SEL_BLOCK
    HW = NSA_HEADS * TQ
    return pl.pallas_call(
        functools.partial(_nsa_select_kernel, TQ=TQ, S=S),
        grid=(B, NB),
        in_specs=[
            pl.BlockSpec((None, A_WIDTH, TQ), lambda b, i: (b, 0, i)),
            pl.BlockSpec((None, NCP, LANES), lambda b, i: (b, 0, 0)),
            pl.BlockSpec((1, HW), lambda b, i: (0, 0)),
        ],
        out_specs=(
            pl.BlockSpec((None, None, HEAD_DIM, HW), lambda b, i: (b, i, 0, 0)),
            pl.BlockSpec((None, SUBLANES, TQ), lambda b, i: (b, 0, i)),
            pl.BlockSpec((None, None, SUBLANES, NSEL), lambda b, i: (b, i, 0, 0)),
        ),
        out_shape=(
            jax.ShapeDtypeStruct((B, NB, HEAD_DIM, HW), bf16),
            jax.ShapeDtypeStruct((B, SUBLANES, S), i32),
            jax.ShapeDtypeStruct((B, NB, SUBLANES, NSEL), f32),
        ),
        scratch_shapes=[pltpu.VMEM((SEL_BLOCK // CMP_STRIDE, NSEL, HW), f32)],
        compiler_params=_cparams(("arbitrary", "arbitrary")),
        name="nsa_select",
    )(qta, kvcmp, slope_lane)


_SLC_CHUNK = 4


def _nsa_attend_kernel(cnt_ref, lst_ref, qt_ref, kslc_ref, kwin_ref, sel_ref, ocmp_ref, gt_ref, slope_ref,
                       o_ref, acc_scr, m_scr, l_scr, *, TQ, S):
    NB = S // TQ
    NSEL = S // SEL_BLOCK
    HW = NSA_HEADS * TQ
    b = pl.program_id(0)
    i = pl.program_id(1)
    count = cnt_ref[b * NB + i]
    qpad = _q_all_heads(qt_ref[...], TQ)
    slope = slope_ref[...]
    tq = i * TQ + lax.broadcasted_iota(i32, (1, HW), 1) % TQ
    sel = sel_ref[...]
    tn = (((0,), (0,)), ((), ()))

    acc_scr[...] = jnp.zeros_like(acc_scr)
    m_scr[...] = jnp.full_like(m_scr, NEG)
    l_scr[...] = jnp.zeros_like(l_scr)

    def chunk(c, carry):
        blks, masks, biases = [], [], []
        for b4 in range(_SLC_CHUNK):
            idx = _SLC_CHUNK * c + b4
            ok = idx < count
            j = jnp.where(ok, lst_ref[0, 0, jnp.minimum(idx, NSEL - 1)], 0)
            blks.append(kslc_ref[pl.ds(pl.multiple_of(j * SEL_BLOCK, SEL_BLOCK), SEL_BLOCK), :])
            picked = jnp.max(jnp.where(sel == j, 1.0, 0.0), axis=0, keepdims=True) > 0.0
            picked = jnp.concatenate([picked & ok] * NSA_HEADS, axis=1)
            dist = tq - (j * SEL_BLOCK + lax.broadcasted_iota(i32, (SEL_BLOCK, 1), 0))
            masks.append((dist >= 0) & picked)
            biases.append(slope * dist.astype(f32))
        kc = jnp.concatenate(blks, axis=0)
        s = jnp.dot(kc, qpad, preferred_element_type=f32)
        parts = [jnp.where(masks[b4], s[b4 * SEL_BLOCK:(b4 + 1) * SEL_BLOCK] - biases[b4], NEG)
                 for b4 in range(_SLC_CHUNK)]
        m_old = m_scr[...]
        m_new = m_old
        for part in parts:
            m_new = jnp.maximum(m_new, jnp.max(part, axis=0, keepdims=True))
        alpha = jnp.exp(m_old - m_new)
        ps = [jnp.where(masks[b4], jnp.exp(parts[b4] - m_new), 0.0) for b4 in range(_SLC_CHUNK)]
        lsum = ps[0].sum(axis=0, keepdims=True)
        for part in ps[1:]:
            lsum = lsum + part.sum(axis=0, keepdims=True)
        l_scr[...] = l_scr[...] * alpha + lsum
        pv = lax.dot_general(kc, jnp.concatenate(ps, axis=0).astype(bf16), tn, preferred_element_type=f32)
        acc_scr[...] = acc_scr[...] * alpha + pv[HEAD_DIM:2 * HEAD_DIM]
        m_scr[...] = m_new
        return carry

    lax.fori_loop(0, (count + _SLC_CHUNK - 1) // _SLC_CHUNK, chunk, 0)
    o_slc = acc_scr[...] / jnp.maximum(l_scr[...], 1e-30)

    NWB = NSA_WINDOW // TQ + 1
    blks, masks, biases = [], [], []
    for w in range(NWB):
        jb = i - (NWB - 1) + w
        blks.append(kwin_ref[pl.ds(pl.multiple_of(jnp.maximum(jb, 0) * TQ, TQ), TQ), :])
        dist = tq - (jb * TQ + lax.broadcasted_iota(i32, (TQ, 1), 0))
        masks.append((dist >= 0) & (dist < NSA_WINDOW) & (jb >= 0))
        biases.append(slope * dist.astype(f32))
    kw = jnp.concatenate(blks, axis=0)
    s = jnp.dot(kw, qpad, preferred_element_type=f32)
    parts = [jnp.where(masks[w], s[w * TQ:(w + 1) * TQ] - biases[w], NEG) for w in range(NWB)]
    m = parts[0].max(axis=0, keepdims=True)
    for part in parts[1:]:
        m = jnp.maximum(m, part.max(axis=0, keepdims=True))
    ps = [jnp.where(masks[w], jnp.exp(parts[w] - m), 0.0) for w in range(NWB)]
    l = ps[0].sum(axis=0, keepdims=True)
    for part in ps[1:]:
        l = l + part.sum(axis=0, keepdims=True)
    pn = (jnp.concatenate(ps, axis=0) * (1.0 / jnp.maximum(l, 1e-30))).astype(bf16)
    o_win = lax.dot_general(kw, pn, tn, preferred_element_type=f32)[HEAD_DIM:2 * HEAD_DIM]

    gt = gt_ref[...]
    gate = [jnp.concatenate([gt[br * NSA_HEADS + h:br * NSA_HEADS + h + 1] for h in range(NSA_HEADS)], axis=1)
            for br in range(3)]
    o = gate[0] * ocmp_ref[...].astype(f32) + gate[1] * o_slc + gate[2] * o_win
    o = jnp.concatenate([o[:, h * TQ:(h + 1) * TQ] for h in range(NSA_HEADS)], axis=0)
    o_ref[...] = o.T.astype(bf16)


def nsa_attend(counts, lists, qta, kva, selT, ocmp, gT, slope_lane, *, B, S, TQ=Q_BLOCK):
    NB = S // TQ
    NSEL = S // SEL_BLOCK
    HW = NSA_HEADS * TQ
    grid_spec = pltpu.PrefetchScalarGridSpec(
        num_scalar_prefetch=1,
        grid=(B, NB),
        in_specs=[
            pl.BlockSpec((1, 1, NSEL), lambda b, i, cnt: (b * NB + i, 0, 0), memory_space=pltpu.SMEM),
            pl.BlockSpec((None, A_WIDTH, TQ), lambda b, i, cnt: (b, 0, i)),
            pl.BlockSpec((None, S, LANES), lambda b, i, cnt: (b, 0, 1)),
            pl.BlockSpec((None, S, LANES), lambda b, i, cnt: (b, 0, 2)),
            pl.BlockSpec((None, SUBLANES, TQ), lambda b, i, cnt: (b, 0, i)),
            pl.BlockSpec((None, None, HEAD_DIM, HW), lambda b, i, cnt: (b, i, 0, 0)),
            pl.BlockSpec((None, 32, TQ), lambda b, i, cnt: (b, 0, i)),
            pl.BlockSpec((1, HW), lambda b, i, cnt: (0, 0)),
        ],
        out_specs=pl.BlockSpec((None, TQ, A_WIDTH), lambda b, i, cnt: (b, i, 0)),
        scratch_shapes=[pltpu.VMEM((HEAD_DIM, HW), f32), pltpu.VMEM((1, HW), f32), pltpu.VMEM((1, HW), f32)],
    )
    return pl.pallas_call(
        functools.partial(_nsa_attend_kernel, TQ=TQ, S=S),
        grid_spec=grid_spec,
        out_shape=jax.ShapeDtypeStruct((B, S, A_WIDTH), bf16),
        compiler_params=_cparams(("arbitrary", "arbitrary")),
        name="nsa_attend",
    )(counts, lists, qta, kva, kva, selT, ocmp, gT, slope_lane)


def nsa_branch(qta, kva, gT, cmp_w, slope_lane, *, B, S):
    NB = S // Q_BLOCK
    NSEL = S // SEL_BLOCK
    NCP = S // CMP_STRIDE
    R = SEL_BLOCK // CMP_STRIDE
    kva = kva.reshape(B, S, 6 * HEAD_DIM)
    ch = kva[:, :, 0:LANES].reshape(B, NCP, CMP_STRIDE * LANES)
    kvcmp = nsa_compress(ch, *cmp_w, B=B, S=S)
    kvcmp = kvcmp.reshape(B, NSEL, R, LANES).swapaxes(1, 2).reshape(B, NCP, LANES)
    ocmp, selT, flags = nsa_select(qta, kvcmp, slope_lane, B=B, S=S)
    used = flags[:, :, 0, :] > 0.5
    ar = jnp.arange(NSEL, dtype=i32)
    lists = jnp.minimum(jnp.sort(jnp.where(used, ar, ar + NSEL), axis=-1), NSEL - 1)
    counts = jnp.sum(used, axis=-1).astype(i32)
    return nsa_attend(counts.reshape(B * NB), lists.reshape(B * NB, 1, NSEL), qta, kva, selT, ocmp, gT, slope_lane,
                      B=B, S=S)


def _route(logit, rb):
    sc = jax.nn.sigmoid(logit)
    sel = sc + rb
    srow = [sel[e:e + 1] for e in range(N_EXPERTS)]
    crow = [sc[e:e + 1] for e in range(N_EXPERTS)]
    gscore = []
    for g in range(N_GROUPS):
        a, b, c, d = srow[4 * g:4 * g + 4]
        top2 = jnp.maximum(jnp.maximum(jnp.maximum(a + b, a + c), jnp.maximum(a + d, b + c)),
                           jnp.maximum(b + d, c + d))
        gscore.append(top2)
    best, gi = gscore[0], jnp.zeros_like(gscore[0], dtype=i32)
    for g in range(1, N_GROUPS):
        better = gscore[g] > best
        gi = jnp.where(better, g, gi)
        best = jnp.where(better, gscore[g], best)

    def pick_group(rows, k):
        v = rows[k]
        for g in range(1, N_GROUPS):
            v = jnp.where(gi == g, rows[4 * g + k], v)
        return v

    iv = [pick_group(srow, k) for k in range(EXPERTS_PER_GROUP)]
    ic = [pick_group(crow, k) for k in range(EXPERTS_PER_GROUP)]
    b1, i1, w1 = iv[0], jnp.zeros_like(gi), ic[0]
    for k in range(1, EXPERTS_PER_GROUP):
        better = iv[k] > b1
        i1 = jnp.where(better, k, i1)
        w1 = jnp.where(better, ic[k], w1)
        b1 = jnp.where(better, iv[k], b1)
    b2 = jnp.full_like(b1, -jnp.inf)
    i2, w2 = jnp.zeros_like(gi), jnp.zeros_like(w1)
    for k in range(EXPERTS_PER_GROUP):
        better = (i1 != k) & (iv[k] > b2)
        i2 = jnp.where(better, k, i2)
        w2 = jnp.where(better, ic[k], w2)
        b2 = jnp.where(better, iv[k], b2)
    tot = w1 + w2
    eidx = jnp.concatenate([gi * EXPERTS_PER_GROUP + i1, gi * EXPERTS_PER_GROUP + i2], axis=0)
    ew = jnp.concatenate([w1 / tot, w2 / tot], axis=0)
    return eidx, ew


def _mix_out_kernel(x_ref, a_ref, sh_ref, ga_ref, oa_ref, ob_ref, oc_ref, wm_ref, wa_ref, wb_ref, wc_ref, wo_ref,
                    am_ref, shm_ref, wr_ref, rb_ref, xo_ref, h2_ref, eidx_ref, ew_ref):
    x = x_ref[...]
    h = _norm_mod(x, a_ref[...], sh_ref[...]).astype(bf16)
    gates = jax.nn.sigmoid(jnp.dot(h, wm_ref[...], preferred_element_type=f32))
    D = D_MODEL
    y = gates[:, 0:D] * jnp.dot(oa_ref[...], wa_ref[...], preferred_element_type=f32)
    y = y + gates[:, D:2 * D] * jnp.dot(ob_ref[...], wb_ref[...], preferred_element_type=f32)
    y = y + gates[:, 2 * D:3 * D] * jnp.dot(oc_ref[...], wc_ref[...], preferred_element_type=f32)
    xn = x + ga_ref[...] * jnp.dot(y.astype(bf16), wo_ref[...], preferred_element_type=f32)
    xo_ref[...] = xn
    h2 = _norm_mod(xn, am_ref[...], shm_ref[...])
    hi = h2.astype(bf16)
    h2_ref[...] = hi
    lo = (h2 - hi.astype(f32)).astype(bf16)
    wr = wr_ref[...]
    whi = wr.astype(bf16)
    wlo = (wr - whi.astype(f32)).astype(bf16)
    nt = lambda p, q: lax.dot_general(p, q, (((1,), (1,)), ((), ())), preferred_element_type=f32)
    logit = nt(whi, hi) + nt(whi, lo) + nt(wlo, hi)
    eidx, ew = _route(logit, rb_ref[...])
    eidx_ref[...] = eidx
    ew_ref[...] = ew


def mix_out(x2, a_mod, sh_mod, g_a, oa, ob, oc, wm, wa, wb, wc, wo, am_moe, shm_moe, wrT, rb, *, B, S, tm=256):
    N, D = x2.shape
    tpb = S // tm
    row = lambda i: (i, 0)
    bmap = lambda i: (i // tpb, 0, 0)
    full = lambda i: (0, 0)
    col = lambda i: (0, i)
    return pl.pallas_call(
        _mix_out_kernel,
        grid=(N // tm,),
        in_specs=[
            pl.BlockSpec((tm, D), row),
            pl.BlockSpec((None, 1, D), bmap), pl.BlockSpec((None, 1, D), bmap), pl.BlockSpec((None, 1, D), bmap),
            pl.BlockSpec((tm, A_WIDTH), row), pl.BlockSpec((tm, B_WIDTH), row), pl.BlockSpec((tm, C_WIDTH), row),
            pl.BlockSpec((D, 3 * D), full), pl.BlockSpec((A_WIDTH, D), full), pl.BlockSpec((B_WIDTH, D), full),
            pl.BlockSpec((C_WIDTH, D), full), pl.BlockSpec((D, D), full),
            pl.BlockSpec((None, 1, D), bmap), pl.BlockSpec((None, 1, D), bmap),
            pl.BlockSpec((N_EXPERTS, D), full), pl.BlockSpec((N_EXPERTS, 1), full),
        ],
        out_specs=(pl.BlockSpec((tm, D), row), pl.BlockSpec((tm, D), row),
                   pl.BlockSpec((TOP_K, tm), col), pl.BlockSpec((TOP_K, tm), col)),
        out_shape=(jax.ShapeDtypeStruct((N, D), f32), jax.ShapeDtypeStruct((N, D), bf16),
                   jax.ShapeDtypeStruct((TOP_K, N), i32), jax.ShapeDtypeStruct((TOP_K, N), f32)),
        compiler_params=_cparams(("arbitrary",)),
        name="mix_out",
    )(x2, a_mod, sh_mod, g_a, oa, ob, oc, wm, wa, wb, wc, wo, am_moe, shm_moe, wrT, rb)


def _moe_ffn_kernel(be_ref, nu_ref, xs_ref, wg_ref, wu_ref, wd_ref, y_ref, wg_s, wu_s, wd_s):
    i = pl.program_id(0)

    @pl.when((i == 0) | (be_ref[i] != be_ref[jnp.maximum(i - 1, 0)]))
    def _():
        wg_s[...] = wg_ref[...].astype(bf16)
        wu_s[...] = wu_ref[...].astype(bf16)
        wd_s[...] = wd_ref[...].astype(bf16)

    @pl.when(i < nu_ref[0])
    def _():
        xs = xs_ref[...]
        g = jnp.dot(xs, wg_s[...], preferred_element_type=f32)
        u = jnp.dot(xs, wu_s[...], preferred_element_type=f32)
        a = (jax.nn.silu(g) * u).astype(bf16)
        y_ref[...] = jnp.dot(a, wd_s[...], preferred_element_type=f32).astype(y_ref.dtype)

    @pl.when(i >= nu_ref[0])
    def _():
        y_ref[...] = jnp.zeros_like(y_ref)


def moe_ffn(blk_e, n_used, xs, wg, wu, wd, *, layer):
    n_rows, D = xs.shape
    n_blk = n_rows // MOE_BLOCK
    wmap = lambda i, be, nu: (layer, be[i], 0, 0)
    grid_spec = pltpu.PrefetchScalarGridSpec(
        num_scalar_prefetch=2,
        grid=(n_blk,),
        in_specs=[
            pl.BlockSpec((MOE_BLOCK, D), lambda i, be, nu: (i, 0)),
            pl.BlockSpec((None, None, D, D_EXPERT), wmap),
            pl.BlockSpec((None, None, D, D_EXPERT), wmap),
            pl.BlockSpec((None, None, D_EXPERT, D), wmap),
        ],
        out_specs=pl.BlockSpec((MOE_BLOCK, D), lambda i, be, nu: (i, 0)),
        scratch_shapes=[pltpu.VMEM((D, D_EXPERT), bf16), pltpu.VMEM((D, D_EXPERT), bf16),
                        pltpu.VMEM((D_EXPERT, D), bf16)],
    )
    return pl.pallas_call(
        _moe_ffn_kernel,
        grid_spec=grid_spec,
        out_shape=jax.ShapeDtypeStruct((n_rows, D), bf16),
        compiler_params=_cparams(("arbitrary",)),
        name="moe_ffn",
    )(blk_e, n_used, xs, wg, wu, wd)


def moe_dispatch(eidx, N):
    NK = N * TOP_K
    n_blk = -(-NK // MOE_BLOCK) + N_EXPERTS
    n_slot = n_blk * MOE_BLOCK
    experts = jnp.arange(N_EXPERTS, dtype=i32)
    flat_e = eidx.T.reshape(-1)
    counts = jnp.sum(flat_e[:, None] == experts[None, :], axis=0, dtype=i32)
    padded = (counts + MOE_BLOCK - 1) // MOE_BLOCK * MOE_BLOCK
    p_end = jnp.cumsum(padded)
    blk_first = jnp.arange(n_blk, dtype=i32) * MOE_BLOCK
    blk_e = jnp.minimum(jnp.sum(p_end[None, :] <= blk_first[:, None], axis=1, dtype=i32), N_EXPERTS - 1)
    n_used = (p_end[-1] // MOE_BLOCK).astype(i32).reshape(1)
    stride = NK + MOE_BLOCK
    pair = jnp.arange(NK, dtype=i32)
    q = jnp.arange(MOE_BLOCK, dtype=i32)
    pad_key = jnp.where(q[None, :] < (padded - counts)[:, None], experts[:, None] * stride + NK + q[None, :],
                        jnp.iinfo(jnp.int32).max)
    keys = jnp.concatenate([flat_e * stride + pair, pad_key.reshape(-1)])
    pair_or_pad = jnp.concatenate([pair, jnp.full((n_slot - NK,), NK, i32)])
    _, slot_pair = lax.sort((keys, pair_or_pad), num_keys=1)
    slot_tok = jnp.minimum(slot_pair // TOP_K, N - 1)
    _, pos = lax.sort((slot_pair, jnp.arange(n_slot, dtype=i32)), num_keys=1)
    return slot_tok, blk_e, n_used, pos[:NK].reshape(N, TOP_K)


def _prep_cmp_weights(pe_k, w1_k, w2_k, pe_v, w1_v, w2_v, k_gain0):
    T = CMP_STRIDE
    w1big = jnp.zeros((T, 2, HEAD_DIM, 4, CMP_HID), f32)
    w1k = w1_k.reshape(2, T, HEAD_DIM, CMP_HID)
    w1v = w1_v.reshape(2, T, HEAD_DIM, CMP_HID)
    w1big = w1big.at[:, 0, :, 0].set(w1k[0]).at[:, 1, :, 1].set(w1v[0])
    w1big = w1big.at[:, 0, :, 2].set(w1k[1]).at[:, 1, :, 3].set(w1v[1])
    w1big = w1big.reshape(T * LANES, 4 * CMP_HID).astype(bf16)
    pe = jnp.stack([pe_k.reshape(2, T, HEAD_DIM), pe_v.reshape(2, T, HEAD_DIM)], axis=2)
    pe2 = jnp.zeros((SUBLANES, T * LANES), f32).at[0:2].set(pe.reshape(2, T * LANES)).astype(bf16)
    w2big = jnp.zeros((2 * CMP_HID, LANES), f32)
    w2big = w2big.at[0:CMP_HID, 0:HEAD_DIM].set(w2_k).at[CMP_HID:, HEAD_DIM:].set(w2_v).astype(bf16)
    gain = jnp.concatenate([k_gain0, jnp.ones((HEAD_DIM,), f32)])[None, :]
    return w1big, pe2, w2big, gain


def _prep_proj_weights(w_in, nsa_q_gain, nsa_k_gain, swa_q_gain, swa_k_gain):
    w = w_in
    wtok = jnp.concatenate([w[:, _O_KVA:_O_GA], w[:, _O_KB:_O_VB], w[:, _O_KC:_O_MG]], axis=1).astype(bf16)
    gate_perm = np.array([h * 3 + br for br in range(3) for h in range(NSA_HEADS)])
    scale = HEAD_DIM ** -0.5
    wch = jnp.concatenate([w[:, _O_QA:_O_KVA], w[:, _O_QB:_O_KB] * (0.5 * scale), w[:, _O_QC:_O_KC],
                           w[:, _O_VB:_O_QC], w[:, _O_GA:_O_QB][:, gate_perm], jnp.zeros((D_MODEL, 8), f32)],
                          axis=1).T.astype(bf16)
    gtok = jnp.zeros((1, TOK_W), f32)
    gtok = gtok.at[0, 128:192].set(nsa_k_gain[1]).at[0, 256:320].set(nsa_k_gain[2])
    gtok = gtok.at[0, 640:704].set(swa_k_gain).at[0, 704:768].set(swa_k_gain)
    mtok = jnp.zeros((1, TOK_W), f32).at[0, 128:192].set(1.0).at[0, 256:320].set(1.0).at[0, 640:768].set(1.0)
    gcha = (jnp.tile(nsa_q_gain, NSA_HEADS) * scale)[:, None]
    gchc = (jnp.tile(swa_q_gain, SWA_HEADS) * scale)[:, None]
    blk = np.arange(LANES) // HEAD_DIM
    bd = jnp.asarray(blk[:, None] == blk[None, :], bf16)
    return wtok, wch, gtok, mtok, gcha, gchc, bd


def _adaln_kernel(c_ref, w_ref, b_ref, o_ref):
    cond = jax.nn.silu(c_ref[...]).astype(bf16)
    o_ref[...] = jnp.dot(cond, w_ref[...].astype(bf16), preferred_element_type=f32) + b_ref[...]


def adaln(c_pad, w_ada, b_ada):
    L, D, D6 = w_ada.shape
    return pl.pallas_call(
        _adaln_kernel,
        grid=(L, D6 // D),
        in_specs=[
            pl.BlockSpec((SUBLANES, D), lambda l, j: (0, 0)),
            pl.BlockSpec((None, D, D), lambda l, j: (l, 0, j)),
            pl.BlockSpec((None, 1, D), lambda l, j: (l, 0, j)),
        ],
        out_specs=pl.BlockSpec((None, SUBLANES, D), lambda l, j: (l, 0, j)),
        out_shape=jax.ShapeDtypeStruct((L, SUBLANES, D6), f32),
        compiler_params=_cparams(("arbitrary", "arbitrary")),
        name="adaln",
    )(c_pad, w_ada, b_ada.reshape(L, 1, D6))


def _moe_combine_kernel(x_ref, y0_ref, y1_ref, w_ref, gm_ref, o_ref):
    w = w_ref[...]
    y = y0_ref[...].astype(f32) * w[:, 0:1] + y1_ref[...].astype(f32) * w[:, 1:2]
    o_ref[...] = x_ref[...] + gm_ref[...] * y


def moe_combine(x2, y0, y1, ew_tok, g_m, *, B, S, tm=512):
    N, D = x2.shape
    tpb = S // tm
    row = lambda i: (i, 0)
    return pl.pallas_call(
        _moe_combine_kernel,
        grid=(N // tm,),
        in_specs=[pl.BlockSpec((tm, D), row), pl.BlockSpec((tm, D), row), pl.BlockSpec((tm, D), row),
                  pl.BlockSpec((tm, TOP_K), row), pl.BlockSpec((None, 1, D), lambda i: (i // tpb, 0, 0))],
        out_specs=pl.BlockSpec((tm, D), row),
        out_shape=jax.ShapeDtypeStruct((N, D), f32),
        compiler_params=_cparams(("arbitrary",)),
        name="moe_combine",
    )(x2, y0, y1, ew_tok, g_m)


def kernel(x, c, w_ada, b_ada, g_norm_mix, g_norm_moe, w_in, cmp_pe_k, cmp_w1_k, cmp_w2_k, cmp_pe_v, cmp_w1_v,
           cmp_w2_v, nsa_q_gain, nsa_k_gain, swa_q_gain, swa_k_gain, swa_sinks, w_branch_a, w_branch_b,
           w_branch_c, w_out, w_router, router_bias, w_exp_gate, w_exp_up, w_exp_down):
    B, S, D = x.shape
    N = B * S
    L = w_ada.shape[0]
    slopes = 2.0 ** (-8.0 * jnp.arange(1, SWA_HEADS + NSA_HEADS + 1, dtype=f32) / (SWA_HEADS + NSA_HEADS))
    swa_slope_l = jnp.repeat(slopes[:SWA_HEADS].reshape(SWA_KV, SWA_HEADS // SWA_KV), Q_BLOCK, axis=1)
    nsa_slope_l = jnp.repeat(slopes[SWA_HEADS:], Q_BLOCK)[None, :]
    c_pad = jnp.zeros((SUBLANES, D), f32).at[:B].set(c)
    mod_all = adaln(c_pad, w_ada, b_ada)[:, :B]
    wrT = w_router.T
    rb = router_bias[:, None]
    x2 = x.reshape(N, D)
    for l in range(L):
        sh_a, sc_a, g_a, sh_m, sc_m, g_m = [m[:, None, :] for m in jnp.split(mod_all[l], 6, axis=-1)]
        a_mix = g_norm_mix[l][None, None, :] * (1.0 + sc_a)
        a_moe = g_norm_moe[l][None, None, :] * (1.0 + sc_m)
        pw = _prep_proj_weights(w_in[l], nsa_q_gain[l], nsa_k_gain[l], swa_q_gain[l], swa_k_gain[l])
        kva, kb, kvc, qta, qtb, qtc, vtb, gT = proj_in(x2, a_mix, sh_a, *pw, B=B, S=S)
        cw = _prep_cmp_weights(cmp_pe_k[l], cmp_w1_k[l], cmp_w2_k[l], cmp_pe_v[l], cmp_w1_v[l], cmp_w2_v[l],
                               nsa_k_gain[l][0])
        o_a = nsa_branch(qta, kva, gT, cw, nsa_slope_l, B=B, S=S)
        o_b = sb_attention(_perm_rows(kb.reshape(B, S, B_WIDTH), S), _perm_lanes(vtb, S), qtb, B=B, S=S)
        sink_l = jnp.repeat(swa_sinks[l].reshape(SWA_KV, SWA_HEADS // SWA_KV), Q_BLOCK, axis=1)
        o_c = swa_attention(kvc.reshape(B, S, 2 * SWA_KV * HEAD_DIM), qtc, swa_slope_l, sink_l, B=B, S=S)
        x2, h2, eidx, ew = mix_out(
            x2, a_mix, sh_a, g_a, o_a.reshape(N, A_WIDTH), o_b.reshape(N, B_WIDTH), o_c.reshape(N, C_WIDTH),
            w_in[l][:, _O_MG:_O_END].astype(bf16), w_branch_a[l].astype(bf16), w_branch_b[l].astype(bf16),
            w_branch_c[l].astype(bf16), w_out[l].astype(bf16), a_moe, sh_m, wrT, rb, B=B, S=S)
        slot_tok, blk_e, n_used, pos = moe_dispatch(eidx, N)
        y = moe_ffn(blk_e, n_used, h2[slot_tok], w_exp_gate, w_exp_up, w_exp_down, layer=l)
        x2 = moe_combine(x2, y[pos[:, 0]], y[pos[:, 1]], ew.T, g_m, B=B, S=S)
    return x2.reshape(B, S, D)
```

```python
import functools

import numpy as np
import jax
import jax.numpy as jnp
from jax import lax
from jax.experimental import pallas as pl
from jax.experimental.pallas import tpu as pltpu

f32 = jnp.float32
bf16 = jnp.bfloat16
i32 = jnp.int32

D_MODEL = 1024
HEAD_DIM = 64
Q_BLOCK = 128
NSA_HEADS = 8
CMP_STRIDE = 16
CMP_LEN = 32
CMP_HID = 256
SEL_BLOCK = 64
SEL_TOPK = 8
NSA_WINDOW = 512
SB_HEADS = 4
SWA_HEADS = 4
SWA_KV = 2
SWA_WINDOW = 128
N_EXPERTS = 16
N_GROUPS = 4
EXPERTS_PER_GROUP = 4
TOP_K = 2
D_EXPERT = 512
MOE_BLOCK = 512
EPS = 1e-6
NEG = -1e30
BIG = 1e30

A_WIDTH = NSA_HEADS * HEAD_DIM
B_WIDTH = SB_HEADS * HEAD_DIM
C_WIDTH = SWA_HEADS * HEAD_DIM

LANES = 128
SUBLANES = 8
VMEM_LIMIT = 56 * 1024 * 1024

_O_QA = 0
_O_KVA = _O_QA + A_WIDTH
_O_GA = _O_KVA + 6 * HEAD_DIM
_O_QB = _O_GA + 3 * NSA_HEADS
_O_KB = _O_QB + B_WIDTH
_O_VB = _O_KB + B_WIDTH
_O_QC = _O_VB + B_WIDTH
_O_KC = _O_QC + C_WIDTH
_O_VC = _O_KC + SWA_KV * HEAD_DIM
_O_MG = _O_VC + SWA_KV * HEAD_DIM
_O_END = _O_MG + 3 * D_MODEL

TOK_W = 6 * HEAD_DIM + B_WIDTH + 2 * SWA_KV * HEAD_DIM
CH_W = A_WIDTH + B_WIDTH + C_WIDTH + B_WIDTH + 32
_TOK_NORM_CHUNKS = (1, 2, 5)


def _cparams(sem):
    return pltpu.CompilerParams(dimension_semantics=sem, vmem_limit_bytes=VMEM_LIMIT)


def _split_dot(x, w):
    hi = x.astype(bf16)
    lo = (x - hi.astype(f32)).astype(bf16)
    return jnp.dot(hi, w, preferred_element_type=f32) + jnp.dot(lo, w, preferred_element_type=f32)


def _norm_mod(x, a, sh):
    rs = lax.rsqrt(jnp.mean(x * x, axis=-1, keepdims=True) + EPS)
    return x * rs * a + sh


def _group_norm_ch(y, gain_col, n, tm):
    y3 = y.reshape(n, HEAD_DIM, tm)
    ss = jnp.sum(y3 * y3, axis=1, keepdims=True)
    yn = y3 * lax.rsqrt(ss * (1.0 / HEAD_DIM) + EPS)
    return yn.reshape(n * HEAD_DIM, tm) * gain_col


def _proj_in_kernel(x_ref, a_ref, sh_ref, wtok_ref, wch_ref, gtok_ref, mtok_ref, gcha_ref, gchc_ref, bd_ref,
                    kva_ref, kb_ref, kvc_ref, qta_ref, qtb_ref, qtc_ref, vtb_ref, gt_ref, *, tm):
    h = _norm_mod(x_ref[...], a_ref[...], sh_ref[...]).astype(bf16)
    t = jnp.dot(h, wtok_ref[...], preferred_element_type=f32)
    cols = [t[:, c * LANES:(c + 1) * LANES] for c in range(TOK_W // LANES)]
    for c in _TOK_NORM_CHUNKS:
        y = cols[c]
        ss = _split_dot(y * y, bd_ref[...])
        yn = y * lax.rsqrt(ss * (1.0 / HEAD_DIM) + EPS) * gtok_ref[:, c * LANES:(c + 1) * LANES]
        cols[c] = jnp.where(mtok_ref[:, c * LANES:(c + 1) * LANES] > 0.0, yn, y)
    t = jnp.concatenate(cols, axis=1).astype(bf16)
    kva_ref[...] = t[:, 0:384]
    kb_ref[...] = t[:, 384:640]
    kvc_ref[...] = t[:, 640:896]
    c = lax.dot_general(wch_ref[...], h, (((1,), (1,)), ((), ())), preferred_element_type=f32)
    qta_ref[...] = _group_norm_ch(c[0:512], gcha_ref[...], NSA_HEADS, tm).astype(bf16)
    qtb_ref[...] = c[512:768].astype(bf16)
    qtc_ref[...] = _group_norm_ch(c[768:1024], gchc_ref[...], SWA_HEADS, tm).astype(bf16)
    vtb_ref[...] = c[1024:1280].astype(bf16)
    gt_ref[...] = jax.nn.sigmoid(c[1280:1312])


def proj_in(x2, a_mod, sh_mod, wtok, wch, gtok, mtok, gcha, gchc, bd, *, B, S, tm=512):
    N, D = x2.shape
    tpb = S // tm
    row = lambda i: (i, 0)
    bmap = lambda i: (i // tpb, 0, 0)
    cmap = lambda i: (i // tpb, 0, i % tpb)
    full = lambda i: (0, 0)
    out_shape = (
        jax.ShapeDtypeStruct((N, 384), bf16), jax.ShapeDtypeStruct((N, 256), bf16), jax.ShapeDtypeStruct((N, 256), bf16),
        jax.ShapeDtypeStruct((B, 512, S), bf16), jax.ShapeDtypeStruct((B, 256, S), bf16),
        jax.ShapeDtypeStruct((B, 256, S), bf16), jax.ShapeDtypeStruct((B, 256, S), bf16),
        jax.ShapeDtypeStruct((B, 32, S), f32),
    )
    return pl.pallas_call(
        functools.partial(_proj_in_kernel, tm=tm),
        grid=(N // tm,),
        in_specs=[
            pl.BlockSpec((tm, D), row),
            pl.BlockSpec((None, 1, D), bmap), pl.BlockSpec((None, 1, D), bmap),
            pl.BlockSpec((D, TOK_W), full), pl.BlockSpec((CH_W, D), full),
            pl.BlockSpec((1, TOK_W), full), pl.BlockSpec((1, TOK_W), full),
            pl.BlockSpec((512, 1), full), pl.BlockSpec((256, 1), full),
            pl.BlockSpec((LANES, LANES), full),
        ],
        out_specs=(
            pl.BlockSpec((tm, 384), row), pl.BlockSpec((tm, 256), row), pl.BlockSpec((tm, 256), row),
            pl.BlockSpec((None, 512, tm), cmap), pl.BlockSpec((None, 256, tm), cmap),
            pl.BlockSpec((None, 256, tm), cmap), pl.BlockSpec((None, 256, tm), cmap),
            pl.BlockSpec((None, 32, tm), cmap),
        ),
        out_shape=out_shape,
        compiler_params=_cparams(("arbitrary",)),
        name="proj_in",
    )(x2, a_mod, sh_mod, wtok, wch, gtok, mtok, gcha, gchc, bd)


def _excl_suffix_prod8(t, rows):
    def shift_up(x, k):
        return jnp.where(rows < SUBLANES - k, pltpu.roll(x, SUBLANES - k, axis=0), 1.0)
    x = shift_up(t, 1)
    x = x * shift_up(x, 1)
    x = x * shift_up(x, 2)
    x = x * shift_up(x, 4)
    return x


SB_TK = 128
SB_TQ = 512


def _sb_scores(k_ref, qpad, z_scr, j, slot):
    kb = k_ref[pl.ds(pl.multiple_of(j * SB_TK, SB_TK), SB_TK), :]
    for h in range(2):
        z_scr[slot, h] = jnp.dot(kb, qpad[h], preferred_element_type=f32)


def _sb_weights(z_scr, p_scr, a_scr, slot, laters, mask_off, TQ):
    V = SB_TK // SUBLANES
    rows = lax.broadcasted_iota(i32, (SUBLANES, TQ), 0)
    qidx = lax.broadcasted_iota(i32, (SUBLANES, TQ), 1)
    new_laters = []
    for h in range(2):
        run = jnp.ones((SUBLANES, TQ), f32)
        for v in reversed(range(V)):
            sl = slice(v * SUBLANES, (v + 1) * SUBLANES)
            beta = 0.5 - 0.5 * jnp.tanh(z_scr[slot, h, sl, :])
            if mask_off is not None:
                beta = jnp.where(rows * V + (v + mask_off) < qidx, beta, 1.0)
            nxt = run * beta
            p_scr[h, sl, :] = run - nxt
            run = nxt
        g = _excl_suffix_prod8(run, rows) * laters[h]
        for v2 in range(V // 2):
            sl = slice(2 * v2 * SUBLANES, (2 * v2 + 2) * SUBLANES)
            a_scr[slot, h, sl, :] = (p_scr[h, sl, :] * jnp.concatenate([g, g], axis=0)).astype(bf16)
        new_laters.append((g * run)[0:1, :])
    return new_laters


def _sb_accumulate(vt_ref, a_scr, acc_scr, j, slot):
    vtb = vt_ref[:, pl.ds(pl.multiple_of(j * SB_TK, SB_TK), SB_TK)]
    for h in range(2):
        acc_scr[h] += jnp.dot(vtb[h * HEAD_DIM:(h + 1) * HEAD_DIM], a_scr[slot, h], preferred_element_type=f32)


def _sb_kernel(k_ref, vt_ref, qt_ref, o_ref, acc_scr, z_scr, p_scr, a_scr, qpad, *, TQ):
    NBQ = TQ // SB_TK
    i = pl.program_id(2)
    qpad[...] = jnp.zeros_like(qpad)
    qpad[0, 0:HEAD_DIM, :] = qt_ref[0:HEAD_DIM, :]
    qpad[1, HEAD_DIM:2 * HEAD_DIM, :] = qt_ref[HEAD_DIM:2 * HEAD_DIM, :]
    acc_scr[...] = jnp.zeros_like(acc_scr)
    a_scr[1] = jnp.zeros_like(a_scr[1])
    j0 = i * NBQ + NBQ - 1

    def step(j, laters, mask_offs):
        _sb_scores(k_ref, qpad, z_scr, j - 1, 1)
        _sb_accumulate(vt_ref, a_scr, acc_scr, jnp.minimum(j + 1, j0), 1)
        laters = _sb_weights(z_scr, p_scr, a_scr, 0, laters, mask_offs[0], TQ)
        _sb_scores(k_ref, qpad, z_scr, jnp.maximum(j - 2, 0), 0)
        _sb_accumulate(vt_ref, a_scr, acc_scr, j, 0)
        laters = _sb_weights(z_scr, p_scr, a_scr, 1, laters, mask_offs[1], TQ)
        return laters

    _sb_scores(k_ref, qpad, z_scr, j0, 0)
    laters = [jnp.ones((1, TQ), f32)] * 2
    for d in range(NBQ // 2):
        jd = NBQ - 1 - 2 * d
        laters = step(i * NBQ + jd, laters, (jd * SB_TK, (jd - 1) * SB_TK))

    def body(t, carry):
        return tuple(step(i * NBQ - 1 - 2 * t, list(carry), (None, None)))

    lax.fori_loop(0, i * (NBQ // 2), body, tuple(laters))
    _sb_accumulate(vt_ref, a_scr, acc_scr, 0, 1)
    o_ref[...] = jnp.concatenate([acc_scr[0].T, acc_scr[1].T], axis=1).astype(bf16)


def sb_attention(kb_perm, vtb_perm, qtb, *, B, S, TQ=SB_TQ):
    TQ = min(TQ, S)
    return pl.pallas_call(
        functools.partial(_sb_kernel, TQ=TQ),
        grid=(B, SB_HEADS // 2, S // TQ),
        in_specs=[
            pl.BlockSpec((None, S, LANES), lambda b, hp, i: (b, 0, hp)),
            pl.BlockSpec((None, LANES, S), lambda b, hp, i: (b, hp, 0)),
            pl.BlockSpec((None, LANES, TQ), lambda b, hp, i: (b, hp, i)),
        ],
        out_specs=pl.BlockSpec((None, TQ, LANES), lambda b, hp, i: (b, i, hp)),
        out_shape=jax.ShapeDtypeStruct((B, S, B_WIDTH), bf16),
        scratch_shapes=[pltpu.VMEM((2, HEAD_DIM, TQ), f32), pltpu.VMEM((2, 2, SB_TK, TQ), f32),
                        pltpu.VMEM((2, SB_TK, TQ), f32), pltpu.VMEM((2, 2, SB_TK, TQ), bf16),
                        pltpu.VMEM((2, 2 * HEAD_DIM, TQ), bf16)],
        compiler_params=_cparams(("arbitrary", "arbitrary", "arbitrary")),
        name="sb_attention",
    )(kb_perm, vtb_perm, qtb)


def _perm_rows(z, S):
    B = z.shape[0]
    V = SB_TK // SUBLANES
    return z.reshape(B, S // SB_TK, SUBLANES, V, z.shape[-1]).swapaxes(2, 3).reshape(B, S, z.shape[-1])


def _perm_lanes(z, S):
    B, C = z.shape[0], z.shape[1]
    V = SB_TK // SUBLANES
    return z.reshape(B, C, S // SB_TK, SUBLANES, V).swapaxes(3, 4).reshape(B, C, S)


def _swa_kernel(kp_ref, kc_ref, vp_ref, vc_ref, qt_ref, slope_ref, sink_ref, o_ref, *, TQ):
    i = pl.program_id(1)
    kk = jnp.concatenate([kp_ref[...], kc_ref[...]], axis=0)
    vv = jnp.concatenate([vp_ref[...], vc_ref[...]], axis=0)
    qt = qt_ref[...]
    r = lax.broadcasted_iota(i32, (2 * TQ, 2 * TQ), 0)
    ql = lax.broadcasted_iota(i32, (2 * TQ, 2 * TQ), 1) % TQ
    dist = TQ + ql - r
    key_pos = (i - 1) * TQ + r
    mask = (dist >= 0) & (dist < SWA_WINDOW) & (key_pos >= 0)
    distf = dist.astype(f32)
    zero = jnp.zeros((HEAD_DIM, 2 * TQ), bf16)
    heads = []
    for g in range(SWA_KV):
        q2 = jnp.concatenate([qt[(2 * g) * 64:(2 * g + 1) * 64], qt[(2 * g + 1) * 64:(2 * g + 2) * 64]], axis=1)
        qpad = jnp.concatenate([q2, zero], axis=0) if g == 0 else jnp.concatenate([zero, q2], axis=0)
        s = jnp.dot(kk, qpad, preferred_element_type=f32)
        s = jnp.where(mask, s - slope_ref[g:g + 1, :] * distf, NEG)
        sink = sink_ref[g:g + 1, :]
        m = jnp.maximum(jnp.max(s, axis=0, keepdims=True), sink)
        p = jnp.where(mask, jnp.exp(s - m), 0.0)
        den = jnp.sum(p, axis=0, keepdims=True) + jnp.exp(sink - m)
        p = (p / den).astype(bf16)
        o = lax.dot_general(vv, p, (((0,), (0,)), ((), ())), preferred_element_type=f32)
        og = o[64 * g:64 * g + 64]
        heads += [og[:, 0:TQ], og[:, TQ:2 * TQ]]
    o_ref[...] = jnp.concatenate(heads, axis=0).T.astype(bf16)


def swa_attention(kvc, qtc, slope_l, sink_l, *, B, S, TQ=Q_BLOCK):
    NB = S // TQ
    prev = lambda c: (lambda b, i: (b, jnp.maximum(i - 1, 0), c))
    cur = lambda c: (lambda b, i: (b, i, c))
    return pl.pallas_call(
        functools.partial(_swa_kernel, TQ=TQ),
        grid=(B, NB),
        in_specs=[
            pl.BlockSpec((None, TQ, LANES), prev(0)), pl.BlockSpec((None, TQ, LANES), cur(0)),
            pl.BlockSpec((None, TQ, LANES), prev(1)), pl.BlockSpec((None, TQ, LANES), cur(1)),
            pl.BlockSpec((None, C_WIDTH, TQ), lambda b, i: (b, 0, i)),
            pl.BlockSpec((SWA_KV, 2 * TQ), lambda b, i: (0, 0)),
            pl.BlockSpec((SWA_KV, 2 * TQ), lambda b, i: (0, 0)),
        ],
        out_specs=pl.BlockSpec((None, TQ, C_WIDTH), lambda b, i: (b, i, 0)),
        out_shape=jax.ShapeDtypeStruct((B, S, C_WIDTH), bf16),
        compiler_params=_cparams(("arbitrary", "arbitrary")),
        name="swa_attention",
    )(kvc, kvc, kvc, kvc, qtc, slope_l, sink_l)


def _nsa_compress_kernel(ch_ref, w1_ref, pe_ref, w2_ref, gain_ref, o_ref, *, NCP):
    w1 = w1_ref[...]
    p = jnp.dot(ch_ref[...], w1, preferred_element_type=f32)
    pb = jnp.dot(pe_ref[...], w1, preferred_element_type=f32)
    bias = pb[0:1, 0:512] + pb[1:2, 512:1024]
    hid = p[:, 0:512] + pltpu.roll(p[:, 512:1024], NCP - 1, axis=0) + bias
    y = jnp.dot(jax.nn.gelu(hid).astype(bf16), w2_ref[...], preferred_element_type=f32)
    is_k = lax.broadcasted_iota(i32, y.shape, 1) < HEAD_DIM
    ss = jnp.sum(jnp.where(is_k, y * y, 0.0), axis=-1, keepdims=True)
    yn = y * lax.rsqrt(ss * (1.0 / HEAD_DIM) + EPS) * gain_ref[...]
    o_ref[...] = jnp.where(is_k, yn, y).astype(bf16)


def nsa_compress(ch, w1big, pe2, w2big, gain, *, B, S):
    NCP = S // CMP_STRIDE
    W = CMP_STRIDE * LANES
    return pl.pallas_call(
        functools.partial(_nsa_compress_kernel, NCP=NCP),
        grid=(B,),
        in_specs=[
            pl.BlockSpec((None, NCP, W), lambda b: (b, 0, 0)),
            pl.BlockSpec((W, 4 * CMP_HID), lambda b: (0, 0)),
            pl.BlockSpec((SUBLANES, W), lambda b: (0, 0)),
            pl.BlockSpec((2 * CMP_HID, LANES), lambda b: (0, 0)),
            pl.BlockSpec((1, LANES), lambda b: (0, 0)),
        ],
        out_specs=pl.BlockSpec((None, NCP, LANES), lambda b: (b, 0, 0)),
        out_shape=jax.ShapeDtypeStruct((B, NCP, LANES), bf16),
        compiler_params=_cparams(("arbitrary",)),
        name="nsa_compress",
    )(ch, w1big, pe2, w2big, gain)


def _q_all_heads(qt, TQ, extra=None):
    qr = jnp.concatenate([qt[h * HEAD_DIM:(h + 1) * HEAD_DIM] for h in range(NSA_HEADS)], axis=1)
    if extra is None:
        return jnp.concatenate([qr, jnp.zeros_like(qr)], axis=0)
    pad = jnp.zeros((HEAD_DIM - extra.shape[0], qr.shape[1]), qr.dtype)
    return jnp.concatenate([qr, extra.astype(qr.dtype), pad], axis=0)


def _nsa_select_kernel(qt_ref, kv_ref, slope_ref, ocmp_ref, sel_ref, flag_ref, s_scr, *, TQ, S):
    NSEL = S // SEL_BLOCK
    NC = S // CMP_STRIDE - 1
    R = SEL_BLOCK // CMP_STRIDE
    KSEL = min(SEL_TOPK, NSEL)
    HW = NSA_HEADS * TQ
    i = pl.program_id(1)
    qpad = _q_all_heads(qt_ref[...], TQ)
    slope = slope_ref[...]
    t = i * TQ + lax.broadcasted_iota(i32, (1, HW), 1) % TQ
    j_col = lax.broadcasted_iota(i32, (NSEL, 1), 0)
    m = jnp.full((1, HW), NEG, f32)
    for r in range(R):
        s = jnp.dot(kv_ref[r * NSEL:(r + 1) * NSEL, :], qpad, preferred_element_type=f32)
        c = R * j_col + r
        dist = t - (CMP_STRIDE * c + (CMP_LEN - 1))
        s = jnp.where((dist >= 0) & (c < NC), s - slope * dist.astype(f32), NEG)
        s_scr[r] = s
        m = jnp.maximum(m, jnp.max(s, axis=0, keepdims=True))
    l = jnp.zeros((1, HW), f32)
    for r in range(R):
        s = s_scr[r]
        p = jnp.where(s > 0.5 * NEG, jnp.exp(s - m), 0.0)
        s_scr[r] = p
        l = l + jnp.sum(p, axis=0, keepdims=True)
    inv = 1.0 / jnp.maximum(l, 1e-30)
    o = jnp.zeros((LANES, HW), f32)
    ps = []
    for r in range(R):
        pn = s_scr[r] * inv
        o = o + lax.dot_general(kv_ref[r * NSEL:(r + 1) * NSEL, :], pn.astype(bf16), (((0,), (0,)), ((), ())),
                                preferred_element_type=f32)
        acc = pn[:, 0:TQ]
        for h in range(1, NSA_HEADS):
            acc = acc + pn[:, h * TQ:(h + 1) * TQ]
        ps.append(acc)
    ocmp_ref[...] = o[HEAD_DIM:2 * HEAD_DIM].astype(ocmp_ref.dtype)
    jrow = lax.broadcasted_iota(i32, (NSEL, TQ), 0)
    prev3 = jnp.where(jrow == 0, 0.0, pltpu.roll(ps[3], 1, axis=0))
    imp = 0.5 * prev3 + ps[0] + ps[1] + ps[2] + 0.5 * ps[3]
    tq = i * TQ + lax.broadcasted_iota(i32, (1, TQ), 1)
    cur = tq // SEL_BLOCK
    forced = (jrow == 0) | (jrow == cur) | (jrow == cur - 1)
    causal = jrow <= cur
    score = jnp.where(causal, jnp.where(forced, BIG, imp), -1e38)
    jf = jrow.astype(f32)
    taken = jnp.zeros((NSEL, TQ), jnp.bool_)
    picks = []
    for _ in range(KSEL):
        mx = jnp.max(score, axis=0, keepdims=True)
        idx = jnp.min(jnp.where(score == mx, jf, float(NSEL)), axis=0, keepdims=True)
        hit = jf == idx
        taken = taken | hit
        score = jnp.where(hit, -jnp.inf, score)
        picks.append(idx)
    picks += [picks[-1]] * (SUBLANES - KSEL)
    sel_ref[...] = jnp.concatenate(picks, axis=0).astype(i32)
    used = jnp.where(taken & causal, 1.0, 0.0).astype(bf16)
    flag_ref[...] = lax.dot_general(jnp.ones((SUBLANES, TQ), bf16), used, (((1,), (1,)), ((), ())),
                                    preferred_element_type=f32)


def nsa_select(qta, kvcmp, slope_lane, *, B, S, TQ=Q_BLOCK):
    NB = S // TQ
    NCP = S // CMP_STRIDE
    NSEL = S // SEL_BLOCK
    HW = NSA_HEADS * TQ
    return pl.pallas_call(
        functools.partial(_nsa_select_kernel, TQ=TQ, S=S),
        grid=(B, NB),
        in_specs=[
            pl.BlockSpec((None, A_WIDTH, TQ), lambda b, i: (b, 0, i)),
            pl.BlockSpec((None, NCP, LANES), lambda b, i: (b, 0, 0)),
            pl.BlockSpec((1, HW), lambda b, i: (0, 0)),
        ],
        out_specs=(
            pl.BlockSpec((None, None, HEAD_DIM, HW), lambda b, i: (b, i, 0, 0)),
            pl.BlockSpec((None, SUBLANES, TQ), lambda b, i: (b, 0, i)),
            pl.BlockSpec((None, None, SUBLANES, NSEL), lambda b, i: (b, i, 0, 0)),
        ),
        out_shape=(
            jax.ShapeDtypeStruct((B, NB, HEAD_DIM, HW), bf16),
            jax.ShapeDtypeStruct((B, SUBLANES, S), i32),
            jax.ShapeDtypeStruct((B, NB, SUBLANES, NSEL), f32),
        ),
        scratch_shapes=[pltpu.VMEM((SEL_BLOCK // CMP_STRIDE, NSEL, HW), f32)],
        compiler_params=_cparams(("arbitrary", "arbitrary")),
        name="nsa_select",
    )(qta, kvcmp, slope_lane)


_SLC_CHUNK = 4


def _nsa_attend_kernel(cnt_ref, lst_ref, qt_ref, kslc_ref, kwin_ref, sel_ref, ocmp_ref, gt_ref, slope_ref,
                       o_ref, acc_scr, m_scr, l_scr, *, TQ, S):
    NB = S // TQ
    NSEL = S // SEL_BLOCK
    HW = NSA_HEADS * TQ
    b = pl.program_id(0)
    i = pl.program_id(1)
    count = cnt_ref[b * NB + i]
    qpad = _q_all_heads(qt_ref[...], TQ)
    slope = slope_ref[...]
    tq = i * TQ + lax.broadcasted_iota(i32, (1, HW), 1) % TQ
    sel = sel_ref[...]
    tn = (((0,), (0,)), ((), ()))

    acc_scr[...] = jnp.zeros_like(acc_scr)
    m_scr[...] = jnp.full_like(m_scr, NEG)
    l_scr[...] = jnp.zeros_like(l_scr)

    def chunk(c, carry):
        blks, masks, biases = [], [], []
        for b4 in range(_SLC_CHUNK):
            idx = _SLC_CHUNK * c + b4
            ok = idx < count
            j = jnp.where(ok, lst_ref[0, 0, jnp.minimum(idx, NSEL - 1)], 0)
            blks.append(kslc_ref[pl.ds(pl.multiple_of(j * SEL_BLOCK, SEL_BLOCK), SEL_BLOCK), :])
            picked = jnp.max(jnp.where(sel == j, 1.0, 0.0), axis=0, keepdims=True) > 0.0
            picked = jnp.concatenate([picked & ok] * NSA_HEADS, axis=1)
            dist = tq - (j * SEL_BLOCK + lax.broadcasted_iota(i32, (SEL_BLOCK, 1), 0))
            masks.append((dist >= 0) & picked)
            biases.append(slope * dist.astype(f32))
        kc = jnp.concatenate(blks, axis=0)
        s = jnp.dot(kc, qpad, preferred_element_type=f32)
        parts = [jnp.where(masks[b4], s[b4 * SEL_BLOCK:(b4 + 1) * SEL_BLOCK] - biases[b4], NEG)
                 for b4 in range(_SLC_CHUNK)]
        m_old = m_scr[...]
        m_new = m_old
        for part in parts:
            m_new = jnp.maximum(m_new, jnp.max(part, axis=0, keepdims=True))
        alpha = jnp.exp(m_old - m_new)
        ps = [jnp.where(masks[b4], jnp.exp(parts[b4] - m_new), 0.0) for b4 in range(_SLC_CHUNK)]
        lsum = ps[0].sum(axis=0, keepdims=True)
        for part in ps[1:]:
            lsum = lsum + part.sum(axis=0, keepdims=True)
        l_scr[...] = l_scr[...] * alpha + lsum
        pv = lax.dot_general(kc, jnp.concatenate(ps, axis=0).astype(bf16), tn, preferred_element_type=f32)
        acc_scr[...] = acc_scr[...] * alpha + pv[HEAD_DIM:2 * HEAD_DIM]
        m_scr[...] = m_new
        return carry

    lax.fori_loop(0, (count + _SLC_CHUNK - 1) // _SLC_CHUNK, chunk, 0)
    o_slc = acc_scr[...] / jnp.maximum(l_scr[...], 1e-30)

    NWB = NSA_WINDOW // TQ + 1
    blks, masks, biases = [], [], []
    for w in range(NWB):
        jb = i - (NWB - 1) + w
        blks.append(kwin_ref[pl.ds(pl.multiple_of(jnp.maximum(jb, 0) * TQ, TQ), TQ), :])
        dist = tq - (jb * TQ + lax.broadcasted_iota(i32, (TQ, 1), 0))
        masks.append((dist >= 0) & (dist < NSA_WINDOW) & (jb >= 0))
        biases.append(slope * dist.astype(f32))
    kw = jnp.concatenate(blks, axis=0)
    s = jnp.dot(kw, qpad, preferred_element_type=f32)
    parts = [jnp.where(masks[w], s[w * TQ:(w + 1) * TQ] - biases[w], NEG) for w in range(NWB)]
    m = parts[0].max(axis=0, keepdims=True)
    for part in parts[1:]:
        m = jnp.maximum(m, part.max(axis=0, keepdims=True))
    ps = [jnp.where(masks[w], jnp.exp(parts[w] - m), 0.0) for w in range(NWB)]
    l = ps[0].sum(axis=0, keepdims=True)
    for part in ps[1:]:
        l = l + part.sum(axis=0, keepdims=True)
    pn = (jnp.concatenate(ps, axis=0) * (1.0 / jnp.maximum(l, 1e-30))).astype(bf16)
    o_win = lax.dot_general(kw, pn, tn, preferred_element_type=f32)[HEAD_DIM:2 * HEAD_DIM]

    gt = gt_ref[...]
    gate = [jnp.concatenate([gt[br * NSA_HEADS + h:br * NSA_HEADS + h + 1] for h in range(NSA_HEADS)], axis=1)
            for br in range(3)]
    o = gate[0] * ocmp_ref[...].astype(f32) + gate[1] * o_slc + gate[2] * o_win
    o = jnp.concatenate([o[:, h * TQ:(h + 1) * TQ] for h in range(NSA_HEADS)], axis=0)
    o_ref[...] = o.T.astype(bf16)


def nsa_attend(counts, lists, qta, kva, selT, ocmp, gT, slope_lane, *, B, S, TQ=Q_BLOCK):
    NB = S // TQ
    NSEL = S // SEL_BLOCK
    HW = NSA_HEADS * TQ
    grid_spec = pltpu.PrefetchScalarGridSpec(
        num_scalar_prefetch=1,
        grid=(B, NB),
        in_specs=[
            pl.BlockSpec((1, 1, NSEL), lambda b, i, cnt: (b * NB + i, 0, 0), memory_space=pltpu.SMEM),
            pl.BlockSpec((None, A_WIDTH, TQ), lambda b, i, cnt: (b, 0, i)),
            pl.BlockSpec((None, S, LANES), lambda b, i, cnt: (b, 0, 1)),
            pl.BlockSpec((None, S, LANES), lambda b, i, cnt: (b, 0, 2)),
            pl.BlockSpec((None, SUBLANES, TQ), lambda b, i, cnt: (b, 0, i)),
            pl.BlockSpec((None, None, HEAD_DIM, HW), lambda b, i, cnt: (b, i, 0, 0)),
            pl.BlockSpec((None, 32, TQ), lambda b, i, cnt: (b, 0, i)),
            pl.BlockSpec((1, HW), lambda b, i, cnt: (0, 0)),
        ],
        out_specs=pl.BlockSpec((None, TQ, A_WIDTH), lambda b, i, cnt: (b, i, 0)),
        scratch_shapes=[pltpu.VMEM((HEAD_DIM, HW), f32), pltpu.VMEM((1, HW), f32), pltpu.VMEM((1, HW), f32)],
    )
    return pl.pallas_call(
        functools.partial(_nsa_attend_kernel, TQ=TQ, S=S),
        grid_spec=grid_spec,
        out_shape=jax.ShapeDtypeStruct((B, S, A_WIDTH), bf16),
        compiler_params=_cparams(("arbitrary", "arbitrary")),
        name="nsa_attend",
    )(counts, lists, qta, kva, kva, selT, ocmp, gT, slope_lane)


POS_COLS = 16
NSA_CMP_CHUNK = 64


def _pos_operand(pos, extra=None):
    n = pos.shape[0]
    col = lax.broadcasted_iota(i32, (n, LANES), 1) - HEAD_DIM
    a = jnp.right_shift(pos, 7)
    b = jnp.bitwise_and(pos, 127)
    val = jnp.where((col >= 0) & (col < 6), jnp.where(col % 2 == 0, a, b), 0)
    if extra is not None:
        val = jnp.where(col == extra, 1, val)
    return val.astype(f32).astype(bf16)


def _nsa_select_kernel(qt_ref, kv_ref, cend_ref, posr_ref, ocmp_ref, sel_ref, flag_ref,
                       s_scr, ps_scr, o_scr, *, TQ, S, CH):
    NSEL = S // SEL_BLOCK
    R = SEL_BLOCK // CMP_STRIDE
    KSEL = min(SEL_TOPK, NSEL)
    HW = NSA_HEADS * TQ
    i = pl.program_id(1)
    posr = posr_ref[...]
    qpad = _q_all_heads(qt_ref[...], TQ, extra=jnp.concatenate([posr, jnp.zeros_like(posr)], axis=0))
    t = i * TQ + lax.broadcasted_iota(i32, (1, HW), 1) % TQ
    tn = (((0,), (0,)), ((), ()))
    nq = jnp.minimum((2 * i + 1) // CH + 1, NSEL // CH)
    ps_scr[...] = jnp.zeros_like(ps_scr)
    o_scr[...] = jnp.zeros_like(o_scr)

    def rows_of(r, q):
        return pl.ds(pl.multiple_of(r * NSEL + q * CH, CH), CH)

    def scores(q, m):
        for r in range(R):
            rows = rows_of(r, q)
            kv = kv_ref[rows, :]
            dist = jnp.maximum(i * TQ + (TQ - 1) - cend_ref[rows, :], 0)
            is_k = lax.broadcasted_iota(i32, kv.shape, 1) < HEAD_DIM
            s = jnp.dot(jnp.where(is_k, kv, _pos_operand(dist)), qpad, preferred_element_type=f32)
            s = jnp.where(cend_ref[rows, :] <= t, s, NEG)
            s_scr[r, pl.ds(pl.multiple_of(q * CH, CH), CH), :] = s
            m = jnp.maximum(m, jnp.max(s, axis=0, keepdims=True))
        return m

    m = lax.fori_loop(0, nq, scores, jnp.full((1, HW), NEG, f32))
    m = jnp.maximum(m, 0.1 * NEG)

    def probs(q, l):
        for r in range(R):
            sl = pl.ds(pl.multiple_of(q * CH, CH), CH)
            p = jnp.exp2(s_scr[r, sl, :] - m)
            s_scr[r, sl, :] = p
            l = l + jnp.sum(p, axis=0, keepdims=True)
            o_scr[...] += lax.dot_general(kv_ref[rows_of(r, q), :], p.astype(bf16), tn, preferred_element_type=f32)
        return l

    l = lax.fori_loop(0, nq, probs, jnp.zeros((1, HW), f32))
    inv = 1.0 / jnp.maximum(l, 1e-30)
    ocmp_ref[...] = (o_scr[HEAD_DIM:2 * HEAD_DIM, :] * inv).astype(ocmp_ref.dtype)

    def head_sums(q, carry):
        for r in range(R):
            sl = pl.ds(pl.multiple_of(q * CH, CH), CH)
            pn = s_scr[r, sl, :] * inv
            acc = pn[:, 0:TQ]
            for h in range(1, NSA_HEADS):
                acc = acc + pn[:, h * TQ:(h + 1) * TQ]
            ps_scr[r, sl, :] = acc
        return carry

    lax.fori_loop(0, nq, head_sums, 0)
    jrow = lax.broadcasted_iota(i32, (NSEL, TQ), 0)
    ps3 = ps_scr[3]
    prev3 = jnp.where(jrow == 0, 0.0, pltpu.roll(ps3, 1, axis=0))
    imp = 0.5 * prev3 + ps_scr[0] + ps_scr[1] + ps_scr[2] + 0.5 * ps3
    tq = i * TQ + lax.broadcasted_iota(i32, (1, TQ), 1)
    cur = tq // SEL_BLOCK
    forced = (jrow == 0) | (jrow == cur) | (jrow == cur - 1)
    causal = jrow <= cur
    score = jnp.where(causal, jnp.where(forced, BIG, imp), -1e38)
    jf = jrow.astype(f32)
    taken = jnp.zeros((NSEL, TQ), jnp.bool_)
    picks = []
    for _ in range(KSEL):
        mx = jnp.max(score, axis=0, keepdims=True)
        idx = jnp.min(jnp.where(score == mx, jf, float(NSEL)), axis=0, keepdims=True)
        hit = jf == idx
        taken = taken | hit
        score = jnp.where(hit, -jnp.inf, score)
        picks.append(idx)
    picks += [picks[-1]] * (SUBLANES - KSEL)
    sel_ref[...] = jnp.concatenate(picks, axis=0).astype(i32)
    used = jnp.where(taken & causal, 1.0, 0.0).astype(bf16)
    flag_ref[...] = lax.dot_general(jnp.ones((SUBLANES, TQ), bf16), used, (((1,), (1,)), ((), ())),
                                    preferred_element_type=f32)


def nsa_select(qta, kvcmp, cend, posr, *, B, S, TQ=Q_BLOCK):
    NB = S // TQ
    NCP = S // CMP_STRIDE
    NSEL = S // SEL_BLOCK
    R = SEL_BLOCK // CMP_STRIDE
    HW = NSA_HEADS * TQ
    CH = min(NSA_CMP_CHUNK, NSEL)
    return pl.pallas_call(
        functools.partial(_nsa_select_kernel, TQ=TQ, S=S, CH=CH),
        grid=(B, NB),
        in_specs=[
            pl.BlockSpec((None, A_WIDTH, TQ), lambda b, i: (b, 0, i)),
            pl.BlockSpec((None, NCP, LANES), lambda b, i: (b, 0, 0)),
            pl.BlockSpec((NCP, 1), lambda b, i: (0, 0)),
            pl.BlockSpec((SUBLANES, HW), lambda b, i: (0, 0)),
        ],
        out_specs=(
            pl.BlockSpec((None, None, HEAD_DIM, HW), lambda b, i: (b, i, 0, 0)),
            pl.BlockSpec((None, SUBLANES, TQ), lambda b, i: (b, 0, i)),
            pl.BlockSpec((None, None, SUBLANES, NSEL), lambda b, i: (b, i, 0, 0)),
        ),
        out_shape=(
            jax.ShapeDtypeStruct((B, NB, HEAD_DIM, HW), bf16),
            jax.ShapeDtypeStruct((B, SUBLANES, S), i32),
            jax.ShapeDtypeStruct((B, NB, SUBLANES, NSEL), f32),
        ),
        scratch_shapes=[pltpu.VMEM((R, NSEL, HW), f32), pltpu.VMEM((R, NSEL, TQ), f32), pltpu.VMEM((LANES, HW), f32)],
        compiler_params=_cparams(("arbitrary", "arbitrary")),
        name="nsa_select",
    )(qta, kvcmp, cend, posr)


_SLC_CHUNK = 4


def _nsa_attend_kernel(cnt_ref, lst_ref, qt_ref, kslc_ref, kwin_ref, sel_ref, ocmp_ref, gt_ref, posr_ref,
                       winb_ref, diagb_ref, o_ref, acc_scr, m_scr, l_scr, qa_scr, *, TQ, S):
    NB = S // TQ
    NSEL = S // SEL_BLOCK
    HW = NSA_HEADS * TQ
    b = pl.program_id(0)
    i = pl.program_id(1)
    count = cnt_ref[b * NB + i]
    qpad = _q_all_heads(qt_ref[...], TQ)
    qa_scr[...] = qpad
    posr = posr_ref[...]
    sel = sel_ref[...]
    tn = (((0,), (0,)), ((), ()))

    kd = kslc_ref[pl.ds(pl.multiple_of(i * TQ, TQ), TQ), :]
    s = jnp.dot(kd, qpad, preferred_element_type=f32) + diagb_ref[...]
    m = jnp.max(s, axis=0, keepdims=True)
    p = jnp.exp2(s - m)
    m_scr[...] = m
    l_scr[...] = jnp.sum(p, axis=0, keepdims=True)
    acc_scr[...] = lax.dot_general(kd, p.astype(bf16), tn, preferred_element_type=f32)[HEAD_DIM:2 * HEAD_DIM]

    def chunk(c, carry):
        blks, pos, pens = [], [], []
        for b4 in range(_SLC_CHUNK):
            idx = _SLC_CHUNK * c + b4
            ok = idx < count
            j = jnp.where(ok, lst_ref[0, 0, jnp.minimum(idx, NSEL - 1)], 0)
            blks.append(kslc_ref[pl.ds(pl.multiple_of(j * SEL_BLOCK, SEL_BLOCK), SEL_BLOCK), :])
            pos.append(i * TQ + (TQ - 1) - j * SEL_BLOCK - lax.broadcasted_iota(i32, (SEL_BLOCK, 1), 0))
            picked = jnp.max(jnp.where(sel == j, 1.0, 0.0), axis=0, keepdims=True) > 0.0
            pen = jnp.where(picked & ok, 0.0, NEG)
            pens.append(jnp.concatenate([pen] * NSA_HEADS, axis=1))
        kc = jnp.concatenate(blks, axis=0)
        blk_id = lax.broadcasted_iota(i32, (_SLC_CHUNK * SEL_BLOCK, 1), 0) // SEL_BLOCK
        pm = _pos_operand(jnp.concatenate(pos, axis=0), extra=SUBLANES + blk_id)
        is_k = lax.broadcasted_iota(i32, kc.shape, 1) < HEAD_DIM
        pens += [jnp.zeros((1, HW), f32)] * (SUBLANES - _SLC_CHUNK)
        qa_scr[HEAD_DIM:HEAD_DIM + POS_COLS, :] = jnp.concatenate([posr, jnp.concatenate(pens, axis=0)],
                                                                  axis=0).astype(bf16)
        s = jnp.dot(jnp.where(is_k, kc, pm), qa_scr[...], preferred_element_type=f32)
        m_old = m_scr[...]
        m_new = jnp.maximum(m_old, jnp.max(s, axis=0, keepdims=True))
        alpha = jnp.exp2(m_old - m_new)
        p = jnp.exp2(s - m_new)
        l_scr[...] = l_scr[...] * alpha + jnp.sum(p, axis=0, keepdims=True)
        pv = lax.dot_general(kc, p.astype(bf16), tn, preferred_element_type=f32)
        acc_scr[...] = acc_scr[...] * alpha + pv[HEAD_DIM:2 * HEAD_DIM]
        m_scr[...] = m_new
        return carry

    lax.fori_loop(0, (count + _SLC_CHUNK - 1) // _SLC_CHUNK, chunk, 0)
    o_slc = acc_scr[...] / l_scr[...]

    NWB = NSA_WINDOW // TQ + 1
    blks, biases = [], []
    for w in range(NWB):
        jb = i - (NWB - 1) + w
        blks.append(kwin_ref[pl.ds(pl.multiple_of(jnp.maximum(jb, 0) * TQ, TQ), TQ), :])
        biases.append(winb_ref[pl.ds(pl.multiple_of(jnp.where(jb >= 0, w, NWB) * TQ, TQ), TQ), :])
    kw = jnp.concatenate(blks, axis=0)
    s = jnp.dot(kw, qpad, preferred_element_type=f32)
    parts = [s[w * TQ:(w + 1) * TQ] + biases[w] for w in range(NWB)]
    m = parts[0].max(axis=0, keepdims=True)
    for part in parts[1:]:
        m = jnp.maximum(m, part.max(axis=0, keepdims=True))
    ps = [jnp.exp2(part - m) for part in parts]
    l = ps[0].sum(axis=0, keepdims=True)
    for part in ps[1:]:
        l = l + part.sum(axis=0, keepdims=True)
    pw = jnp.concatenate(ps, axis=0).astype(bf16)
    o_win = lax.dot_general(kw, pw, tn, preferred_element_type=f32)[HEAD_DIM:2 * HEAD_DIM] * (1.0 / l)

    gt = gt_ref[...]
    gate = [jnp.concatenate([gt[br * NSA_HEADS + h:br * NSA_HEADS + h + 1] for h in range(NSA_HEADS)], axis=1)
            for br in range(3)]
    o = gate[0] * ocmp_ref[...].astype(f32) + gate[1] * o_slc + gate[2] * o_win
    o = jnp.concatenate([o[:, h * TQ:(h + 1) * TQ] for h in range(NSA_HEADS)], axis=0)
    o_ref[...] = o.T.astype(bf16)


def nsa_attend(counts, lists, qta, kva, selT, ocmp, gT, posr, winb, diagb, *, B, S, TQ=Q_BLOCK):
    NB = S // TQ
    NSEL = S // SEL_BLOCK
    HW = NSA_HEADS * TQ
    NWB = NSA_WINDOW // TQ + 1
    const = lambda b, i, cnt: (0, 0)
    grid_spec = pltpu.PrefetchScalarGridSpec(
        num_scalar_prefetch=1,
        grid=(B, NB),
        in_specs=[
            pl.BlockSpec((1, 1, NSEL), lambda b, i, cnt: (b * NB + i, 0, 0), memory_space=pltpu.SMEM),
            pl.BlockSpec((None, A_WIDTH, TQ), lambda b, i, cnt: (b, 0, i)),
            pl.BlockSpec((None, S, LANES), lambda b, i, cnt: (b, 0, 1)),
            pl.BlockSpec((None, S, LANES), lambda b, i, cnt: (b, 0, 2)),
            pl.BlockSpec((None, SUBLANES, TQ), lambda b, i, cnt: (b, 0, i)),
            pl.BlockSpec((None, None, HEAD_DIM, HW), lambda b, i, cnt: (b, i, 0, 0)),
            pl.BlockSpec((None, 32, TQ), lambda b, i, cnt: (b, 0, i)),
            pl.BlockSpec((SUBLANES, HW), const),
            pl.BlockSpec(((NWB + 1) * TQ, HW), const),
            pl.BlockSpec((TQ, HW), const),
        ],
        out_specs=pl.BlockSpec((None, TQ, A_WIDTH), lambda b, i, cnt: (b, i, 0)),
        scratch_shapes=[pltpu.VMEM((HEAD_DIM, HW), f32), pltpu.VMEM((1, HW), f32), pltpu.VMEM((1, HW), f32),
                        pltpu.VMEM((LANES, HW), bf16)],
    )
    return pl.pallas_call(
        functools.partial(_nsa_attend_kernel, TQ=TQ, S=S),
        grid_spec=grid_spec,
        out_shape=jax.ShapeDtypeStruct((B, S, A_WIDTH), bf16),
        compiler_params=_cparams(("arbitrary", "arbitrary")),
        name="nsa_attend",
    )(counts, lists, qta, kva, kva, selT, ocmp, gT, posr, winb, diagb)


def nsa_constants(slopes, S, TQ=Q_BLOCK):
    NSEL = S // SEL_BLOCK
    NCP = S // CMP_STRIDE
    R = SEL_BLOCK // CMP_STRIDE
    NWB = NSA_WINDOW // TQ + 1
    s2 = jnp.repeat(slopes * np.float32(np.log2(np.e)), TQ)
    hi = s2.astype(bf16).astype(f32)
    mid = (s2 - hi).astype(bf16).astype(f32)
    lo = (s2 - hi - mid).astype(bf16).astype(f32)
    zero = jnp.zeros_like(s2)
    posr = -jnp.stack([128.0 * hi, hi, 128.0 * mid, mid, 128.0 * lo, lo, zero, zero], axis=0)
    rho = np.arange(NCP)
    c = R * (rho % NSEL) + rho // NSEL
    cend = CMP_STRIDE * c + (CMP_LEN - 1)
    cend_mask = np.where(c < NCP - 1, cend, np.iinfo(np.int32).max)
    tl = jnp.tile(jnp.arange(TQ, dtype=i32), NSA_HEADS)[None, :]
    wpos = jnp.arange((NWB + 1) * TQ, dtype=i32)[:, None]
    dist = (NWB - 1) * TQ + tl - wpos
    winb = jnp.where((dist >= 0) & (dist < NSA_WINDOW) & (wpos < NWB * TQ),
                     -s2[None, :] * (NWB * TQ - 1 - wpos).astype(f32), NEG)
    dpos = jnp.arange(TQ, dtype=i32)[:, None]
    diagb = jnp.where(dpos <= tl, -s2[None, :] * (TQ - 1 - dpos).astype(f32), NEG)
    return (posr, jnp.asarray(cend_mask[:, None], i32), winb, diagb)


def nsa_branch(qta, kva, gT, cmp_w, nsa_c, *, B, S):
    NB = S // Q_BLOCK
    NSEL = S // SEL_BLOCK
    NCP = S // CMP_STRIDE
    R = SEL_BLOCK // CMP_STRIDE
    posr, cend, winb, diagb = nsa_c
    kva = kva.reshape(B, S, 6 * HEAD_DIM)
    ch = kva[:, :, 0:LANES].reshape(B, NCP, CMP_STRIDE * LANES)
    kvcmp = nsa_compress(ch, *cmp_w, B=B, S=S)
    kvcmp = kvcmp.reshape(B, NSEL, R, LANES).swapaxes(1, 2).reshape(B, NCP, LANES)
    ocmp, selT, flags = nsa_select(qta, kvcmp, cend, posr, B=B, S=S)
    ar = jnp.arange(NSEL, dtype=i32)
    own = (Q_BLOCK // SEL_BLOCK) * jnp.arange(NB, dtype=i32)
    used = (flags[:, :, 0, :] > 0.5) & (ar[None, None, :] < own[None, :, None])
    lists = jnp.minimum(jnp.sort(jnp.where(used, ar, ar + NSEL), axis=-1), NSEL - 1)
    counts = jnp.sum(used, axis=-1).astype(i32)
    return nsa_attend(counts.reshape(B * NB), lists.reshape(B * NB, 1, NSEL), qta, kva, selT, ocmp, gT,
                      posr, winb, diagb, B=B, S=S)


def _route(logit, rb):
    sc = jax.nn.sigmoid(logit)
    sel = sc + rb
    srow = [sel[e:e + 1] for e in range(N_EXPERTS)]
    crow = [sc[e:e + 1] for e in range(N_EXPERTS)]
    gscore = []
    for g in range(N_GROUPS):
        a, b, c, d = srow[4 * g:4 * g + 4]
        top2 = jnp.maximum(jnp.maximum(jnp.maximum(a + b, a + c), jnp.maximum(a + d, b + c)),
                           jnp.maximum(b + d, c + d))
        gscore.append(top2)
    best, gi = gscore[0], jnp.zeros_like(gscore[0], dtype=i32)
    for g in range(1, N_GROUPS):
        better = gscore[g] > best
        gi = jnp.where(better, g, gi)
        best = jnp.where(better, gscore[g], best)

    def pick_group(rows, k):
        v = rows[k]
        for g in range(1, N_GROUPS):
            v = jnp.where(gi == g, rows[4 * g + k], v)
        return v

    iv = [pick_group(srow, k) for k in range(EXPERTS_PER_GROUP)]
    ic = [pick_group(crow, k) for k in range(EXPERTS_PER_GROUP)]
    b1, i1, w1 = iv[0], jnp.zeros_like(gi), ic[0]
    for k in range(1, EXPERTS_PER_GROUP):
        better = iv[k] > b1
        i1 = jnp.where(better, k, i1)
        w1 = jnp.where(better, ic[k], w1)
        b1 = jnp.where(better, iv[k], b1)
    b2 = jnp.full_like(b1, -jnp.inf)
    i2, w2 = jnp.zeros_like(gi), jnp.zeros_like(w1)
    for k in range(EXPERTS_PER_GROUP):
        better = (i1 != k) & (iv[k] > b2)
        i2 = jnp.where(better, k, i2)
        w2 = jnp.where(better, ic[k], w2)
        b2 = jnp.where(better, iv[k], b2)
    tot = w1 + w2
    eidx = jnp.concatenate([gi * EXPERTS_PER_GROUP + i1, gi * EXPERTS_PER_GROUP + i2], axis=0)
    ew = jnp.concatenate([w1 / tot, w2 / tot], axis=0)
    return eidx, ew


def _mix_out_kernel(x_ref, a_ref, sh_ref, ga_ref, oa_ref, ob_ref, oc_ref, wm_ref, wa_ref, wb_ref, wc_ref, wo_ref,
                    am_ref, shm_ref, wr_ref, rb_ref, xo_ref, h2_ref, eidx_ref, ew_ref):
    x = x_ref[...]
    h = _norm_mod(x, a_ref[...], sh_ref[...]).astype(bf16)
    gates = jax.nn.sigmoid(jnp.dot(h, wm_ref[...], preferred_element_type=f32))
    D = D_MODEL
    y = gates[:, 0:D] * jnp.dot(oa_ref[...], wa_ref[...], preferred_element_type=f32)
    y = y + gates[:, D:2 * D] * jnp.dot(ob_ref[...], wb_ref[...], preferred_element_type=f32)
    y = y + gates[:, 2 * D:3 * D] * jnp.dot(oc_ref[...], wc_ref[...], preferred_element_type=f32)
    xn = x + ga_ref[...] * jnp.dot(y.astype(bf16), wo_ref[...], preferred_element_type=f32)
    xo_ref[...] = xn
    h2 = _norm_mod(xn, am_ref[...], shm_ref[...])
    hi = h2.astype(bf16)
    h2_ref[...] = hi
    lo = (h2 - hi.astype(f32)).astype(bf16)
    wr = wr_ref[...]
    whi = wr.astype(bf16)
    wlo = (wr - whi.astype(f32)).astype(bf16)
    nt = lambda p, q: lax.dot_general(p, q, (((1,), (1,)), ((), ())), preferred_element_type=f32)
    logit = nt(whi, hi) + nt(whi, lo) + nt(wlo, hi)
    eidx, ew = _route(logit, rb_ref[...])
    eidx_ref[...] = eidx
    ew_ref[...] = ew


def mix_out(x2, a_mod, sh_mod, g_a, oa, ob, oc, wm, wa, wb, wc, wo, am_moe, shm_moe, wrT, rb, *, B, S, tm=256):
    N, D = x2.shape
    tpb = S // tm
    row = lambda i: (i, 0)
    bmap = lambda i: (i // tpb, 0, 0)
    full = lambda i: (0, 0)
    col = lambda i: (0, i)
    return pl.pallas_call(
        _mix_out_kernel,
        grid=(N // tm,),
        in_specs=[
            pl.BlockSpec((tm, D), row),
            pl.BlockSpec((None, 1, D), bmap), pl.BlockSpec((None, 1, D), bmap), pl.BlockSpec((None, 1, D), bmap),
            pl.BlockSpec((tm, A_WIDTH), row), pl.BlockSpec((tm, B_WIDTH), row), pl.BlockSpec((tm, C_WIDTH), row),
            pl.BlockSpec((D, 3 * D), full), pl.BlockSpec((A_WIDTH, D), full), pl.BlockSpec((B_WIDTH, D), full),
            pl.BlockSpec((C_WIDTH, D), full), pl.BlockSpec((D, D), full),
            pl.BlockSpec((None, 1, D), bmap), pl.BlockSpec((None, 1, D), bmap),
            pl.BlockSpec((N_EXPERTS, D), full), pl.BlockSpec((N_EXPERTS, 1), full),
        ],
        out_specs=(pl.BlockSpec((tm, D), row), pl.BlockSpec((tm, D), row),
                   pl.BlockSpec((TOP_K, tm), col), pl.BlockSpec((TOP_K, tm), col)),
        out_shape=(jax.ShapeDtypeStruct((N, D), f32), jax.ShapeDtypeStruct((N, D), bf16),
                   jax.ShapeDtypeStruct((TOP_K, N), i32), jax.ShapeDtypeStruct((TOP_K, N), f32)),
        compiler_params=_cparams(("arbitrary",)),
        name="mix_out",
    )(x2, a_mod, sh_mod, g_a, oa, ob, oc, wm, wa, wb, wc, wo, am_moe, shm_moe, wrT, rb)


def _moe_ffn_kernel(be_ref, nu_ref, xs_ref, wg_ref, wu_ref, wd_ref, y_ref, wg_s, wu_s, wd_s):
    i = pl.program_id(0)

    @pl.when((i == 0) | (be_ref[i] != be_ref[jnp.maximum(i - 1, 0)]))
    def _():
        wg_s[...] = wg_ref[...].astype(bf16)
        wu_s[...] = wu_ref[...].astype(bf16)
        wd_s[...] = wd_ref[...].astype(bf16)

    @pl.when(i < nu_ref[0])
    def _():
        xs = xs_ref[...]
        g = jnp.dot(xs, wg_s[...], preferred_element_type=f32)
        u = jnp.dot(xs, wu_s[...], preferred_element_type=f32)
        a = (jax.nn.silu(g) * u).astype(bf16)
        y_ref[...] = jnp.dot(a, wd_s[...], preferred_element_type=f32).astype(y_ref.dtype)

    @pl.when(i >= nu_ref[0])
    def _():
        y_ref[...] = jnp.zeros_like(y_ref)


def moe_ffn(blk_e, n_used, xs, wg, wu, wd, *, layer):
    n_rows, D = xs.shape
    n_blk = n_rows // MOE_BLOCK
    wmap = lambda i, be, nu: (layer, be[i], 0, 0)
    grid_spec = pltpu.PrefetchScalarGridSpec(
        num_scalar_prefetch=2,
        grid=(n_blk,),
        in_specs=[
            pl.BlockSpec((MOE_BLOCK, D), lambda i, be, nu: (i, 0)),
            pl.BlockSpec((None, None, D, D_EXPERT), wmap),
            pl.BlockSpec((None, None, D, D_EXPERT), wmap),
            pl.BlockSpec((None, None, D_EXPERT, D), wmap),
        ],
        out_specs=pl.BlockSpec((MOE_BLOCK, D), lambda i, be, nu: (i, 0)),
        scratch_shapes=[pltpu.VMEM((D, D_EXPERT), bf16), pltpu.VMEM((D, D_EXPERT), bf16),
                        pltpu.VMEM((D_EXPERT, D), bf16)],
    )
    return pl.pallas_call(
        _moe_ffn_kernel,
        grid_spec=grid_spec,
        out_shape=jax.ShapeDtypeStruct((n_rows, D), bf16),
        compiler_params=_cparams(("arbitrary",)),
        name="moe_ffn",
    )(blk_e, n_used, xs, wg, wu, wd)


def moe_dispatch(eidx, N):
    NK = N * TOP_K
    n_blk = -(-NK // MOE_BLOCK) + N_EXPERTS
    n_slot = n_blk * MOE_BLOCK
    experts = jnp.arange(N_EXPERTS, dtype=i32)
    flat_e = eidx.T.reshape(-1)
    counts = jnp.sum(flat_e[:, None] == experts[None, :], axis=0, dtype=i32)
    padded = (counts + MOE_BLOCK - 1) // MOE_BLOCK * MOE_BLOCK
    p_end = jnp.cumsum(padded)
    blk_first = jnp.arange(n_blk, dtype=i32) * MOE_BLOCK
    blk_e = jnp.minimum(jnp.sum(p_end[None, :] <= blk_first[:, None], axis=1, dtype=i32), N_EXPERTS - 1)
    n_used = (p_end[-1] // MOE_BLOCK).astype(i32).reshape(1)
    stride = NK + MOE_BLOCK
    pair = jnp.arange(NK, dtype=i32)
    q = jnp.arange(MOE_BLOCK, dtype=i32)
    pad_key = jnp.where(q[None, :] < (padded - counts)[:, None], experts[:, None] * stride + NK + q[None, :],
                        jnp.iinfo(jnp.int32).max)
    keys = jnp.concatenate([flat_e * stride + pair, pad_key.reshape(-1)])
    pair_or_pad = jnp.concatenate([pair, jnp.full((n_slot - NK,), NK, i32)])
    _, slot_pair = lax.sort((keys, pair_or_pad), num_keys=1)
    slot_tok = jnp.minimum(slot_pair // TOP_K, N - 1)
    _, pos = lax.sort((slot_pair, jnp.arange(n_slot, dtype=i32)), num_keys=1)
    return slot_tok, blk_e, n_used, pos[:NK].reshape(N, TOP_K)


def _prep_cmp_weights(pe_k, w1_k, w2_k, pe_v, w1_v, w2_v, k_gain0):
    T = CMP_STRIDE
    w1big = jnp.zeros((T, 2, HEAD_DIM, 4, CMP_HID), f32)
    w1k = w1_k.reshape(2, T, HEAD_DIM, CMP_HID)
    w1v = w1_v.reshape(2, T, HEAD_DIM, CMP_HID)
    w1big = w1big.at[:, 0, :, 0].set(w1k[0]).at[:, 1, :, 1].set(w1v[0])
    w1big = w1big.at[:, 0, :, 2].set(w1k[1]).at[:, 1, :, 3].set(w1v[1])
    w1big = w1big.reshape(T * LANES, 4 * CMP_HID).astype(bf16)
    pe = jnp.stack([pe_k.reshape(2, T, HEAD_DIM), pe_v.reshape(2, T, HEAD_DIM)], axis=2)
    pe2 = jnp.zeros((SUBLANES, T * LANES), f32).at[0:2].set(pe.reshape(2, T * LANES)).astype(bf16)
    w2big = jnp.zeros((2 * CMP_HID, LANES), f32)
    w2big = w2big.at[0:CMP_HID, 0:HEAD_DIM].set(w2_k).at[CMP_HID:, HEAD_DIM:].set(w2_v).astype(bf16)
    gain = jnp.concatenate([k_gain0, jnp.ones((HEAD_DIM,), f32)])[None, :]
    return w1big, pe2, w2big, gain


def _prep_proj_weights(w_in, nsa_q_gain, nsa_k_gain, swa_q_gain, swa_k_gain):
    w = w_in
    wtok = jnp.concatenate([w[:, _O_KVA:_O_GA], w[:, _O_KB:_O_VB], w[:, _O_KC:_O_MG]], axis=1).astype(bf16)
    gate_perm = np.array([h * 3 + br for br in range(3) for h in range(NSA_HEADS)])
    scale = HEAD_DIM ** -0.5
    wch = jnp.concatenate([w[:, _O_QA:_O_KVA], w[:, _O_QB:_O_KB] * (0.5 * scale), w[:, _O_QC:_O_KC],
                           w[:, _O_VB:_O_QC], w[:, _O_GA:_O_QB][:, gate_perm], jnp.zeros((D_MODEL, 8), f32)],
                          axis=1).T.astype(bf16)
    gtok = jnp.zeros((1, TOK_W), f32)
    gtok = gtok.at[0, 128:192].set(nsa_k_gain[1]).at[0, 256:320].set(nsa_k_gain[2])
    gtok = gtok.at[0, 640:704].set(swa_k_gain).at[0, 704:768].set(swa_k_gain)
    mtok = jnp.zeros((1, TOK_W), f32).at[0, 128:192].set(1.0).at[0, 256:320].set(1.0).at[0, 640:768].set(1.0)
    gcha = (jnp.tile(nsa_q_gain, NSA_HEADS) * (scale * np.float32(np.log2(np.e))))[:, None]
    gchc = (jnp.tile(swa_q_gain, SWA_HEADS) * scale)[:, None]
    blk = np.arange(LANES) // HEAD_DIM
    bd = jnp.asarray(blk[:, None] == blk[None, :], bf16)
    return wtok, wch, gtok, mtok, gcha, gchc, bd


def _adaln_kernel(c_ref, w_ref, b_ref, o_ref):
    cond = jax.nn.silu(c_ref[...]).astype(bf16)
    o_ref[...] = jnp.dot(cond, w_ref[...].astype(bf16), preferred_element_type=f32) + b_ref[...]


def adaln(c_pad, w_ada, b_ada):
    L, D, D6 = w_ada.shape
    return pl.pallas_call(
        _adaln_kernel,
        grid=(L, D6 // D),
        in_specs=[
            pl.BlockSpec((SUBLANES, D), lambda l, j: (0, 0)),
            pl.BlockSpec((None, D, D), lambda l, j: (l, 0, j)),
            pl.BlockSpec((None, 1, D), lambda l, j: (l, 0, j)),
        ],
        out_specs=pl.BlockSpec((None, SUBLANES, D), lambda l, j: (l, 0, j)),
        out_shape=jax.ShapeDtypeStruct((L, SUBLANES, D6), f32),
        compiler_params=_cparams(("arbitrary", "arbitrary")),
        name="adaln",
    )(c_pad, w_ada, b_ada.reshape(L, 1, D6))


def _moe_combine_kernel(x_ref, y0_ref, y1_ref, w_ref, gm_ref, o_ref):
    w = w_ref[...]
    y = y0_ref[...].astype(f32) * w[:, 0:1] + y1_ref[...].astype(f32) * w[:, 1:2]
    o_ref[...] = x_ref[...] + gm_ref[...] * y


def moe_combine(x2, y0, y1, ew_tok, g_m, *, B, S, tm=512):
    N, D = x2.shape
    tpb = S // tm
    row = lambda i: (i, 0)
    return pl.pallas_call(
        _moe_combine_kernel,
        grid=(N // tm,),
        in_specs=[pl.BlockSpec((tm, D), row), pl.BlockSpec((tm, D), row), pl.BlockSpec((tm, D), row),
                  pl.BlockSpec((tm, TOP_K), row), pl.BlockSpec((None, 1, D), lambda i: (i // tpb, 0, 0))],
        out_specs=pl.BlockSpec((tm, D), row),
        out_shape=jax.ShapeDtypeStruct((N, D), f32),
        compiler_params=_cparams(("arbitrary",)),
        name="moe_combine",
    )(x2, y0, y1, ew_tok, g_m)


def kernel(x, c, w_ada, b_ada, g_norm_mix, g_norm_moe, w_in, cmp_pe_k, cmp_w1_k, cmp_w2_k, cmp_pe_v, cmp_w1_v,
           cmp_w2_v, nsa_q_gain, nsa_k_gain, swa_q_gain, swa_k_gain, swa_sinks, w_branch_a, w_branch_b,
           w_branch_c, w_out, w_router, router_bias, w_exp_gate, w_exp_up, w_exp_down):
    B, S, D = x.shape
    N = B * S
    L = w_ada.shape[0]
    slopes = 2.0 ** (-8.0 * jnp.arange(1, SWA_HEADS + NSA_HEADS + 1, dtype=f32) / (SWA_HEADS + NSA_HEADS))
    swa_slope_l = jnp.repeat(slopes[:SWA_HEADS].reshape(SWA_KV, SWA_HEADS // SWA_KV), Q_BLOCK, axis=1)
    nsa_c = nsa_constants(slopes[SWA_HEADS:], S)
    c_pad = jnp.zeros((SUBLANES, D), f32).at[:B].set(c)
    mod_all = adaln(c_pad, w_ada, b_ada)[:, :B]
    wrT = w_router.T
    rb = router_bias[:, None]
    x2 = x.reshape(N, D)
    for l in range(L):
        sh_a, sc_a, g_a, sh_m, sc_m, g_m = [m[:, None, :] for m in jnp.split(mod_all[l], 6, axis=-1)]
        a_mix = g_norm_mix[l][None, None, :] * (1.0 + sc_a)
        a_moe = g_norm_moe[l][None, None, :] * (1.0 + sc_m)
        pw = _prep_proj_weights(w_in[l], nsa_q_gain[l], nsa_k_gain[l], swa_q_gain[l], swa_k_gain[l])
        kva, kb, kvc, qta, qtb, qtc, vtb, gT = proj_in(x2, a_mix, sh_a, *pw, B=B, S=S)
        cw = _prep_cmp_weights(cmp_pe_k[l], cmp_w1_k[l], cmp_w2_k[l], cmp_pe_v[l], cmp_w1_v[l], cmp_w2_v[l],
                               nsa_k_gain[l][0])
        o_a = nsa_branch(qta, kva, gT, cw, nsa_c, B=B, S=S)
        o_b = sb_attention(_perm_rows(kb.reshape(B, S, B_WIDTH), S), _perm_lanes(vtb, S), qtb, B=B, S=S)
        sink_l = jnp.repeat(swa_sinks[l].reshape(SWA_KV, SWA_HEADS // SWA_KV), Q_BLOCK, axis=1)
        o_c = swa_attention(kvc.reshape(B, S, 2 * SWA_KV * HEAD_DIM), qtc, swa_slope_l, sink_l, B=B, S=S)
        x2, h2, eidx, ew = mix_out(
            x2, a_mix, sh_a, g_a, o_a.reshape(N, A_WIDTH), o_b.reshape(N, B_WIDTH), o_c.reshape(N, C_WIDTH),
            w_in[l][:, _O_MG:_O_END].astype(bf16), w_branch_a[l].astype(bf16), w_branch_b[l].astype(bf16),
            w_branch_c[l].astype(bf16), w_out[l].astype(bf16), a_moe, sh_m, wrT, rb, B=B, S=S)
        slot_tok, blk_e, n_used, pos = moe_dispatch(eidx, N)
        y = moe_ffn(blk_e, n_used, h2[slot_tok], w_exp_gate, w_exp_up, w_exp_down, layer=l)
        x2 = moe_combine(x2, y[pos[:, 0]], y[pos[:, 1]], ew.T, g_m, B=B, S=S)
    return x2.reshape(B, S, D)
```

```python
import functools

import numpy as np
import jax
import jax.numpy as jnp
from jax import lax
from jax.experimental import pallas as pl
from jax.experimental.pallas import tpu as pltpu

f32 = jnp.float32
bf16 = jnp.bfloat16
i32 = jnp.int32

D_MODEL = 1024
HEAD_DIM = 64
Q_BLOCK = 128
NSA_HEADS = 8
CMP_STRIDE = 16
CMP_LEN = 32
CMP_HID = 256
SEL_BLOCK = 64
SEL_TOPK = 8
NSA_WINDOW = 512
SB_HEADS = 4
SWA_HEADS = 4
SWA_KV = 2
SWA_WINDOW = 128
N_EXPERTS = 16
N_GROUPS = 4
EXPERTS_PER_GROUP = 4
TOP_K = 2
D_EXPERT = 512
MOE_BLOCK = 512
EPS = 1e-6
NEG = -1e30
BIG = 1e30

A_WIDTH = NSA_HEADS * HEAD_DIM
B_WIDTH = SB_HEADS * HEAD_DIM
C_WIDTH = SWA_HEADS * HEAD_DIM

LANES = 128
SUBLANES = 8
VMEM_LIMIT = 56 * 1024 * 1024

_O_QA = 0
_O_KVA = _O_QA + A_WIDTH
_O_GA = _O_KVA + 6 * HEAD_DIM
_O_QB = _O_GA + 3 * NSA_HEADS
_O_KB = _O_QB + B_WIDTH
_O_VB = _O_KB + B_WIDTH
_O_QC = _O_VB + B_WIDTH
_O_KC = _O_QC + C_WIDTH
_O_VC = _O_KC + SWA_KV * HEAD_DIM
_O_MG = _O_VC + SWA_KV * HEAD_DIM
_O_END = _O_MG + 3 * D_MODEL

TOK_W = 6 * HEAD_DIM + B_WIDTH + 2 * SWA_KV * HEAD_DIM
CH_W = A_WIDTH + B_WIDTH + C_WIDTH + B_WIDTH + 32
_TOK_NORM_CHUNKS = (1, 2, 5)


def _cparams(sem):
    return pltpu.CompilerParams(dimension_semantics=sem, vmem_limit_bytes=VMEM_LIMIT)


def _split_dot(x, w):
    hi = x.astype(bf16)
    lo = (x - hi.astype(f32)).astype(bf16)
    return jnp.dot(hi, w, preferred_element_type=f32) + jnp.dot(lo, w, preferred_element_type=f32)


def _norm_mod(x, a, sh):
    rs = lax.rsqrt(jnp.mean(x * x, axis=-1, keepdims=True) + EPS)
    return x * rs * a + sh


def _group_norm_ch(y, gain_col, n, tm):
    y3 = y.reshape(n, HEAD_DIM, tm)
    ss = jnp.sum(y3 * y3, axis=1, keepdims=True)
    yn = y3 * lax.rsqrt(ss * (1.0 / HEAD_DIM) + EPS)
    return yn.reshape(n * HEAD_DIM, tm) * gain_col


def _proj_in_kernel(x_ref, a_ref, sh_ref, wtok_ref, wch_ref, gtok_ref, mtok_ref, gcha_ref, gchc_ref, bd_ref,
                    kva_ref, kb_ref, kvc_ref, qta_ref, qtb_ref, qtc_ref, vtb_ref, gt_ref, *, tm):
    h = _norm_mod(x_ref[...], a_ref[...], sh_ref[...]).astype(bf16)
    t = jnp.dot(h, wtok_ref[...], preferred_element_type=f32)
    cols = [t[:, c * LANES:(c + 1) * LANES] for c in range(TOK_W // LANES)]
    for c in _TOK_NORM_CHUNKS:
        y = cols[c]
        ss = _split_dot(y * y, bd_ref[...])
        yn = y * lax.rsqrt(ss * (1.0 / HEAD_DIM) + EPS) * gtok_ref[:, c * LANES:(c + 1) * LANES]
        cols[c] = jnp.where(mtok_ref[:, c * LANES:(c + 1) * LANES] > 0.0, yn, y)
    t = jnp.concatenate(cols, axis=1).astype(bf16)
    kva_ref[...] = t[:, 0:384]
    kb_ref[...] = t[:, 384:640]
    kvc_ref[...] = t[:, 640:896]
    c = lax.dot_general(wch_ref[...], h, (((1,), (1,)), ((), ())), preferred_element_type=f32)
    qta_ref[...] = _group_norm_ch(c[0:512], gcha_ref[...], NSA_HEADS, tm).astype(bf16)
    qtb_ref[...] = c[512:768].astype(bf16)
    qtc_ref[...] = _group_norm_ch(c[768:1024], gchc_ref[...], SWA_HEADS, tm).astype(bf16)
    vtb_ref[...] = c[1024:1280].astype(bf16)
    gt_ref[...] = jax.nn.sigmoid(c[1280:1312])


def proj_in(x2, a_mod, sh_mod, wtok, wch, gtok, mtok, gcha, gchc, bd, *, B, S, tm=512):
    N, D = x2.shape
    tpb = S // tm
    row = lambda i: (i, 0)
    bmap = lambda i: (i // tpb, 0, 0)
    cmap = lambda i: (i // tpb, 0, i % tpb)
    full = lambda i: (0, 0)
    out_shape = (
        jax.ShapeDtypeStruct((N, 384), bf16), jax.ShapeDtypeStruct((N, 256), bf16), jax.ShapeDtypeStruct((N, 256), bf16),
        jax.ShapeDtypeStruct((B, 512, S), bf16), jax.ShapeDtypeStruct((B, 256, S), bf16),
        jax.ShapeDtypeStruct((B, 256, S), bf16), jax.ShapeDtypeStruct((B, 256, S), bf16),
        jax.ShapeDtypeStruct((B, 32, S), f32),
    )
    return pl.pallas_call(
        functools.partial(_proj_in_kernel, tm=tm),
        grid=(N // tm,),
        in_specs=[
            pl.BlockSpec((tm, D), row),
            pl.BlockSpec((None, 1, D), bmap), pl.BlockSpec((None, 1, D), bmap),
            pl.BlockSpec((D, TOK_W), full), pl.BlockSpec((CH_W, D), full),
            pl.BlockSpec((1, TOK_W), full), pl.BlockSpec((1, TOK_W), full),
            pl.BlockSpec((512, 1), full), pl.BlockSpec((256, 1), full),
            pl.BlockSpec((LANES, LANES), full),
        ],
        out_specs=(
            pl.BlockSpec((tm, 384), row), pl.BlockSpec((tm, 256), row), pl.BlockSpec((tm, 256), row),
            pl.BlockSpec((None, 512, tm), cmap), pl.BlockSpec((None, 256, tm), cmap),
            pl.BlockSpec((None, 256, tm), cmap), pl.BlockSpec((None, 256, tm), cmap),
            pl.BlockSpec((None, 32, tm), cmap),
        ),
        out_shape=out_shape,
        compiler_params=_cparams(("arbitrary",)),
        name="proj_in",
    )(x2, a_mod, sh_mod, wtok, wch, gtok, mtok, gcha, gchc, bd)


def _excl_suffix_prod8(t, rows):
    def shift_up(x, k):
        return jnp.where(rows < SUBLANES - k, pltpu.roll(x, SUBLANES - k, axis=0), 1.0)
    x = shift_up(t, 1)
    x = x * shift_up(x, 1)
    x = x * shift_up(x, 2)
    x = x * shift_up(x, 4)
    return x


SB_TK = 128
SB_TQ = 512
SB_NH = 2


def _sb_scores(k_ref, qpad, z_scr, j, slot):
    kb = k_ref[pl.ds(pl.multiple_of(j * SB_TK, SB_TK), SB_TK), :]
    for h in range(SB_NH):
        z_scr[slot, h] = jnp.dot(kb, qpad[h], preferred_element_type=f32)


def _sb_weights(z_scr, p_scr, a_scr, slot, laters, mask_off, TQ):
    V = SB_TK // SUBLANES
    rows = lax.broadcasted_iota(i32, (SUBLANES, TQ), 0)
    qidx = lax.broadcasted_iota(i32, (SUBLANES, TQ), 1)
    new_laters = []
    for h in range(SB_NH):
        run = jnp.ones((SUBLANES, TQ), f32)
        for v in reversed(range(V)):
            sl = slice(v * SUBLANES, (v + 1) * SUBLANES)
            beta = 0.5 - 0.5 * jnp.tanh(z_scr[slot, h, sl, :])
            if mask_off is not None:
                beta = jnp.where(rows * V + (v + mask_off) < qidx, beta, 1.0)
            nxt = run * beta
            p_scr[slot % 2, h, sl, :] = run - nxt
            run = nxt
        g = _excl_suffix_prod8(run, rows) * laters[h]
        for v2 in range(V // 2):
            sl = slice(2 * v2 * SUBLANES, (2 * v2 + 2) * SUBLANES)
            a_scr[slot, h, sl, :] = (p_scr[slot % 2, h, sl, :] * jnp.concatenate([g, g], axis=0)).astype(bf16)
        new_laters.append((g * run)[0:1, :])
    return new_laters


def _sb_accumulate(vt_ref, a_scr, acc_scr, j, slot):
    vtb = vt_ref[:, pl.ds(pl.multiple_of(j * SB_TK, SB_TK), SB_TK)]
    for h in range(SB_NH):
        acc_scr[h] += jnp.dot(vtb[h * HEAD_DIM:(h + 1) * HEAD_DIM], a_scr[slot, h], preferred_element_type=f32)


def _sb_kernel(k_ref, vt_ref, qt_ref, o_ref, acc_scr, z_scr, p_scr, a_scr, qpad, *, TQ):
    NBQ = TQ // SB_TK
    i = pl.program_id(2)
    qpad[...] = jnp.zeros_like(qpad)
    for h in range(SB_NH):
        qpad[h, h * HEAD_DIM:(h + 1) * HEAD_DIM, :] = qt_ref[h * HEAD_DIM:(h + 1) * HEAD_DIM, :]
    acc_scr[...] = jnp.zeros_like(acc_scr)
    a_scr[2] = jnp.zeros_like(a_scr[2])
    a_scr[3] = jnp.zeros_like(a_scr[3])
    j0 = i * NBQ + NBQ - 1
    assert NBQ % 4 == 0

    def step(j, par, laters, mask_offs):
        o = 2 * (1 - par)
        _sb_scores(k_ref, qpad, z_scr, jnp.maximum(j - 2, 0), o)
        _sb_scores(k_ref, qpad, z_scr, jnp.maximum(j - 3, 0), o + 1)
        _sb_accumulate(vt_ref, a_scr, acc_scr, jnp.minimum(j + 2, j0), o)
        _sb_accumulate(vt_ref, a_scr, acc_scr, jnp.minimum(j + 1, j0), o + 1)
        laters = _sb_weights(z_scr, p_scr, a_scr, 2 * par, laters, mask_offs[0], TQ)
        laters = _sb_weights(z_scr, p_scr, a_scr, 2 * par + 1, laters, mask_offs[1], TQ)
        return laters

    _sb_scores(k_ref, qpad, z_scr, j0, 0)
    _sb_scores(k_ref, qpad, z_scr, j0 - 1, 1)
    laters = [jnp.ones((1, TQ), f32)] * SB_NH
    for d in range(NBQ // 2):
        jd = NBQ - 1 - 2 * d
        laters = step(i * NBQ + jd, d % 2, laters, (jd * SB_TK, (jd - 1) * SB_TK))

    def body(t, carry):
        j = i * NBQ - 1 - 4 * t
        l = step(j, 0, list(carry), (None, None))
        return tuple(step(j - 2, 1, l, (None, None)))

    lax.fori_loop(0, i * (NBQ // 4), body, tuple(laters))
    _sb_accumulate(vt_ref, a_scr, acc_scr, 1, 2)
    _sb_accumulate(vt_ref, a_scr, acc_scr, 0, 3)
    o_ref[...] = jnp.concatenate([acc_scr[h].T for h in range(SB_NH)], axis=1).astype(bf16)


def sb_attention(kb_perm, vtb_perm, qtb, *, B, S, TQ=SB_TQ):
    TQ = min(TQ, S)
    W = SB_NH * HEAD_DIM
    return pl.pallas_call(
        functools.partial(_sb_kernel, TQ=TQ),
        grid=(B, SB_HEADS // SB_NH, S // TQ),
        in_specs=[
            pl.BlockSpec((None, S, W), lambda b, hp, i: (b, 0, hp)),
            pl.BlockSpec((None, W, S), lambda b, hp, i: (b, hp, 0)),
            pl.BlockSpec((None, W, TQ), lambda b, hp, i: (b, hp, i)),
        ],
        out_specs=pl.BlockSpec((None, TQ, W), lambda b, hp, i: (b, i, hp)),
        out_shape=jax.ShapeDtypeStruct((B, S, B_WIDTH), bf16),
        scratch_shapes=[pltpu.VMEM((SB_NH, HEAD_DIM, TQ), f32), pltpu.VMEM((4, SB_NH, SB_TK, TQ), f32),
                        pltpu.VMEM((2, SB_NH, SB_TK, TQ), f32), pltpu.VMEM((4, SB_NH, SB_TK, TQ), bf16),
                        pltpu.VMEM((SB_NH, W, TQ), bf16)],
        compiler_params=_cparams(("arbitrary", "arbitrary", "arbitrary")),
        name="sb_attention",
    )(kb_perm, vtb_perm, qtb)


def _perm_rows(z, S):
    B = z.shape[0]
    V = SB_TK // SUBLANES
    return z.reshape(B, S // SB_TK, SUBLANES, V, z.shape[-1]).swapaxes(2, 3).reshape(B, S, z.shape[-1])


def _perm_lanes(z, S):
    B, C = z.shape[0], z.shape[1]
    V = SB_TK // SUBLANES
    return z.reshape(B, C, S // SB_TK, SUBLANES, V).swapaxes(3, 4).reshape(B, C, S)


def _swa_kernel(kp_ref, kc_ref, vp_ref, vc_ref, qt_ref, slope_ref, sink_ref, o_ref, *, TQ):
    i = pl.program_id(1)
    kk = jnp.concatenate([kp_ref[...], kc_ref[...]], axis=0)
    vv = jnp.concatenate([vp_ref[...], vc_ref[...]], axis=0)
    qt = qt_ref[...]
    r = lax.broadcasted_iota(i32, (2 * TQ, 2 * TQ), 0)
    ql = lax.broadcasted_iota(i32, (2 * TQ, 2 * TQ), 1) % TQ
    dist = TQ + ql - r
    key_pos = (i - 1) * TQ + r
    mask = (dist >= 0) & (dist < SWA_WINDOW) & (key_pos >= 0)
    distf = dist.astype(f32)
    zero = jnp.zeros((HEAD_DIM, 2 * TQ), bf16)
    heads = []
    for g in range(SWA_KV):
        q2 = jnp.concatenate([qt[(2 * g) * 64:(2 * g + 1) * 64], qt[(2 * g + 1) * 64:(2 * g + 2) * 64]], axis=1)
        qpad = jnp.concatenate([q2, zero], axis=0) if g == 0 else jnp.concatenate([zero, q2], axis=0)
        s = jnp.dot(kk, qpad, preferred_element_type=f32)
        s = jnp.where(mask, s - slope_ref[g:g + 1, :] * distf, NEG)
        sink = sink_ref[g:g + 1, :]
        m = jnp.maximum(jnp.max(s, axis=0, keepdims=True), sink)
        p = jnp.where(mask, jnp.exp(s - m), 0.0)
        den = jnp.sum(p, axis=0, keepdims=True) + jnp.exp(sink - m)
        p = (p / den).astype(bf16)
        o = lax.dot_general(vv, p, (((0,), (0,)), ((), ())), preferred_element_type=f32)
        og = o[64 * g:64 * g + 64]
        heads += [og[:, 0:TQ], og[:, TQ:2 * TQ]]
    o_ref[...] = jnp.concatenate(heads, axis=0).T.astype(bf16)


def swa_attention(kvc, qtc, slope_l, sink_l, *, B, S, TQ=Q_BLOCK):
    NB = S // TQ
    prev = lambda c: (lambda b, i: (b, jnp.maximum(i - 1, 0), c))
    cur = lambda c: (lambda b, i: (b, i, c))
    return pl.pallas_call(
        functools.partial(_swa_kernel, TQ=TQ),
        grid=(B, NB),
        in_specs=[
            pl.BlockSpec((None, TQ, LANES), prev(0)), pl.BlockSpec((None, TQ, LANES), cur(0)),
            pl.BlockSpec((None, TQ, LANES), prev(1)), pl.BlockSpec((None, TQ, LANES), cur(1)),
            pl.BlockSpec((None, C_WIDTH, TQ), lambda b, i: (b, 0, i)),
            pl.BlockSpec((SWA_KV, 2 * TQ), lambda b, i: (0, 0)),
            pl.BlockSpec((SWA_KV, 2 * TQ), lambda b, i: (0, 0)),
        ],
        out_specs=pl.BlockSpec((None, TQ, C_WIDTH), lambda b, i: (b, i, 0)),
        out_shape=jax.ShapeDtypeStruct((B, S, C_WIDTH), bf16),
        compiler_params=_cparams(("arbitrary", "arbitrary")),
        name="swa_attention",
    )(kvc, kvc, kvc, kvc, qtc, slope_l, sink_l)


def _nsa_compress_kernel(ch_ref, w1_ref, pe_ref, w2_ref, gain_ref, o_ref, *, NCP):
    w1 = w1_ref[...]
    p = jnp.dot(ch_ref[...], w1, preferred_element_type=f32)
    pb = jnp.dot(pe_ref[...], w1, preferred_element_type=f32)
    bias = pb[0:1, 0:512] + pb[1:2, 512:1024]
    hid = p[:, 0:512] + pltpu.roll(p[:, 512:1024], NCP - 1, axis=0) + bias
    y = jnp.dot(jax.nn.gelu(hid).astype(bf16), w2_ref[...], preferred_element_type=f32)
    is_k = lax.broadcasted_iota(i32, y.shape, 1) < HEAD_DIM
    ss = jnp.sum(jnp.where(is_k, y * y, 0.0), axis=-1, keepdims=True)
    yn = y * lax.rsqrt(ss * (1.0 / HEAD_DIM) + EPS) * gain_ref[...]
    o_ref[...] = jnp.where(is_k, yn, y).astype(bf16)


def nsa_compress(ch, w1big, pe2, w2big, gain, *, B, S):
    NCP = S // CMP_STRIDE
    W = CMP_STRIDE * LANES
    return pl.pallas_call(
        functools.partial(_nsa_compress_kernel, NCP=NCP),
        grid=(B,),
        in_specs=[
            pl.BlockSpec((None, NCP, W), lambda b: (b, 0, 0)),
            pl.BlockSpec((W, 4 * CMP_HID), lambda b: (0, 0)),
            pl.BlockSpec((SUBLANES, W), lambda b: (0, 0)),
            pl.BlockSpec((2 * CMP_HID, LANES), lambda b: (0, 0)),
            pl.BlockSpec((1, LANES), lambda b: (0, 0)),
        ],
        out_specs=pl.BlockSpec((None, NCP, LANES), lambda b: (b, 0, 0)),
        out_shape=jax.ShapeDtypeStruct((B, NCP, LANES), bf16),
        compiler_params=_cparams(("arbitrary",)),
        name="nsa_compress",
    )(ch, w1big, pe2, w2big, gain)


def _q_all_heads(qt, TQ, extra=None):
    qr = jnp.concatenate([qt[h * HEAD_DIM:(h + 1) * HEAD_DIM] for h in range(NSA_HEADS)], axis=1)
    if extra is None:
        return jnp.concatenate([qr, jnp.zeros_like(qr)], axis=0)
    pad = jnp.zeros((HEAD_DIM - extra.shape[0], qr.shape[1]), qr.dtype)
    return jnp.concatenate([qr, extra.astype(qr.dtype), pad], axis=0)


def _nsa_select_kernel(qt_ref, kv_ref, slope_ref, ocmp_ref, sel_ref, flag_ref, s_scr, *, TQ, S):
    NSEL = S // SEL_BLOCK
    NC = S // CMP_STRIDE - 1
    R = SEL_BLOCK // CMP_STRIDE
    KSEL = min(SEL_TOPK, NSEL)
    HW = NSA_HEADS * TQ
    i = pl.program_id(1)
    qpad = _q_all_heads(qt_ref[...], TQ)
    slope = slope_ref[...]
    t = i * TQ + lax.broadcasted_iota(i32, (1, HW), 1) % TQ
    j_col = lax.broadcasted_iota(i32, (NSEL, 1), 0)
    m = jnp.full((1, HW), NEG, f32)
    for r in range(R):
        s = jnp.dot(kv_ref[r * NSEL:(r + 1) * NSEL, :], qpad, preferred_element_type=f32)
        c = R * j_col + r
        dist = t - (CMP_STRIDE * c + (CMP_LEN - 1))
        s = jnp.where((dist >= 0) & (c < NC), s - slope * dist.astype(f32), NEG)
        s_scr[r] = s
        m = jnp.maximum(m, jnp.max(s, axis=0, keepdims=True))
    l = jnp.zeros((1, HW), f32)
    for r in range(R):
        s = s_scr[r]
        p = jnp.where(s > 0.5 * NEG, jnp.exp(s - m), 0.0)
        s_scr[r] = p
        l = l + jnp.sum(p, axis=0, keepdims=True)
    inv = 1.0 / jnp.maximum(l, 1e-30)
    o = jnp.zeros((LANES, HW), f32)
    ps = []
    for r in range(R):
        pn = s_scr[r] * inv
        o = o + lax.dot_general(kv_ref[r * NSEL:(r + 1) * NSEL, :], pn.astype(bf16), (((0,), (0,)), ((), ())),
                                preferred_element_type=f32)
        acc = pn[:, 0:TQ]
        for h in range(1, NSA_HEADS):
            acc = acc + pn[:, h * TQ:(h + 1) * TQ]
        ps.append(acc)
    ocmp_ref[...] = o[HEAD_DIM:2 * HEAD_DIM].astype(ocmp_ref.dtype)
    jrow = lax.broadcasted_iota(i32, (NSEL, TQ), 0)
    prev3 = jnp.where(jrow == 0, 0.0, pltpu.roll(ps[3], 1, axis=0))
    imp = 0.5 * prev3 + ps[0] + ps[1] + ps[2] + 0.5 * ps[3]
    tq = i * TQ + lax.broadcasted_iota(i32, (1, TQ), 1)
    cur = tq // SEL_BLOCK
    forced = (jrow == 0) | (jrow == cur) | (jrow == cur - 1)
    causal = jrow <= cur
    score = jnp.where(causal, jnp.where(forced, BIG, imp), -1e38)
    jf = jrow.astype(f32)
    taken = jnp.zeros((NSEL, TQ), jnp.bool_)
    picks = []
    for _ in range(KSEL):
        mx = jnp.max(score, axis=0, keepdims=True)
        idx = jnp.min(jnp.where(score == mx, jf, float(NSEL)), axis=0, keepdims=True)
        hit = jf == idx
        taken = taken | hit
        score = jnp.where(hit, -jnp.inf, score)
        picks.append(idx)
    picks += [picks[-1]] * (SUBLANES - KSEL)
    sel_ref[...] = jnp.concatenate(picks, axis=0).astype(i32)
    used = jnp.where(taken & causal, 1.0, 0.0).astype(bf16)
    flag_ref[...] = lax.dot_general(jnp.ones((SUBLANES, TQ), bf16), used, (((1,), (1,)), ((), ())),
                                    preferred_element_type=f32)


def nsa_select(qta, kvcmp, slope_lane, *, B, S, TQ=Q_BLOCK):
    NB = S // TQ
    NCP = S // CMP_STRIDE
    NSEL = S // SEL_BLOCK
    HW = NSA_HEADS * TQ
    return pl.pallas_call(
        functools.partial(_nsa_select_kernel, TQ=TQ, S=S),
        grid=(B, NB),
        in_specs=[
            pl.BlockSpec((None, A_WIDTH, TQ), lambda b, i: (b, 0, i)),
            pl.BlockSpec((None, NCP, LANES), lambda b, i: (b, 0, 0)),
            pl.BlockSpec((1, HW), lambda b, i: (0, 0)),
        ],
        out_specs=(
            pl.BlockSpec((None, None, HEAD_DIM, HW), lambda b, i: (b, i, 0, 0)),
            pl.BlockSpec((None, SUBLANES, TQ), lambda b, i: (b, 0, i)),
            pl.BlockSpec((None, None, SUBLANES, NSEL), lambda b, i: (b, i, 0, 0)),
        ),
        out_shape=(
            jax.ShapeDtypeStruct((B, NB, HEAD_DIM, HW), bf16),
            jax.ShapeDtypeStruct((B, SUBLANES, S), i32),
            jax.ShapeDtypeStruct((B, NB, SUBLANES, NSEL), f32),
        ),
        scratch_shapes=[pltpu.VMEM((SEL_BLOCK // CMP_STRIDE, NSEL, HW), f32)],
        compiler_params=_cparams(("arbitrary", "arbitrary")),
        name="nsa_select",
    )(qta, kvcmp, slope_lane)


_SLC_CHUNK = 4


def _nsa_attend_kernel(cnt_ref, lst_ref, qt_ref, kslc_ref, kwin_ref, sel_ref, ocmp_ref, gt_ref, slope_ref,
                       o_ref, acc_scr, m_scr, l_scr, *, TQ, S):
    NB = S // TQ
    NSEL = S // SEL_BLOCK
    HW = NSA_HEADS * TQ
    b = pl.program_id(0)
    i = pl.program_id(1)
    count = cnt_ref[b * NB + i]
    qpad = _q_all_heads(qt_ref[...], TQ)
    slope = slope_ref[...]
    tq = i * TQ + lax.broadcasted_iota(i32, (1, HW), 1) % TQ
    sel = sel_ref[...]
    tn = (((0,), (0,)), ((), ()))

    acc_scr[...] = jnp.zeros_like(acc_scr)
    m_scr[...] = jnp.full_like(m_scr, NEG)
    l_scr[...] = jnp.zeros_like(l_scr)

    def chunk(c, carry):
        blks, masks, biases = [], [], []
        for b4 in range(_SLC_CHUNK):
            idx = _SLC_CHUNK * c + b4
            ok = idx < count
            j = jnp.where(ok, lst_ref[0, 0, jnp.minimum(idx, NSEL - 1)], 0)
            blks.append(kslc_ref[pl.ds(pl.multiple_of(j * SEL_BLOCK, SEL_BLOCK), SEL_BLOCK), :])
            picked = jnp.max(jnp.where(sel == j, 1.0, 0.0), axis=0, keepdims=True) > 0.0
            picked = jnp.concatenate([picked & ok] * NSA_HEADS, axis=1)
            dist = tq - (j * SEL_BLOCK + lax.broadcasted_iota(i32, (SEL_BLOCK, 1), 0))
            masks.append((dist >= 0) & picked)
            biases.append(slope * dist.astype(f32))
        kc = jnp.concatenate(blks, axis=0)
        s = jnp.dot(kc, qpad, preferred_element_type=f32)
        parts = [jnp.where(masks[b4], s[b4 * SEL_BLOCK:(b4 + 1) * SEL_BLOCK] - biases[b4], NEG)
                 for b4 in range(_SLC_CHUNK)]
        m_old = m_scr[...]
        m_new = m_old
        for part in parts:
            m_new = jnp.maximum(m_new, jnp.max(part, axis=0, keepdims=True))
        alpha = jnp.exp(m_old - m_new)
        ps = [jnp.where(masks[b4], jnp.exp(parts[b4] - m_new), 0.0) for b4 in range(_SLC_CHUNK)]
        lsum = ps[0].sum(axis=0, keepdims=True)
        for part in ps[1:]:
            lsum = lsum + part.sum(axis=0, keepdims=True)
        l_scr[...] = l_scr[...] * alpha + lsum
        pv = lax.dot_general(kc, jnp.concatenate(ps, axis=0).astype(bf16), tn, preferred_element_type=f32)
        acc_scr[...] = acc_scr[...] * alpha + pv[HEAD_DIM:2 * HEAD_DIM]
        m_scr[...] = m_new
        return carry

    lax.fori_loop(0, (count + _SLC_CHUNK - 1) // _SLC_CHUNK, chunk, 0)
    o_slc = acc_scr[...] / jnp.maximum(l_scr[...], 1e-30)

    NWB = NSA_WINDOW // TQ + 1
    blks, masks, biases = [], [], []
    for w in range(NWB):
        jb = i - (NWB - 1) + w
        blks.append(kwin_ref[pl.ds(pl.multiple_of(jnp.maximum(jb, 0) * TQ, TQ), TQ), :])
        dist = tq - (jb * TQ + lax.broadcasted_iota(i32, (TQ, 1), 0))
        masks.append((dist >= 0) & (dist < NSA_WINDOW) & (jb >= 0))
        biases.append(slope * dist.astype(f32))
    kw = jnp.concatenate(blks, axis=0)
    s = jnp.dot(kw, qpad, preferred_element_type=f32)
    parts = [jnp.where(masks[w], s[w * TQ:(w + 1) * TQ] - biases[w], NEG) for w in range(NWB)]
    m = parts[0].max(axis=0, keepdims=True)
    for part in parts[1:]:
        m = jnp.maximum(m, part.max(axis=0, keepdims=True))
    ps = [jnp.where(masks[w], jnp.exp(parts[w] - m), 0.0) for w in range(NWB)]
    l = ps[0].sum(axis=0, keepdims=True)
    for part in ps[1:]:
        l = l + part.sum(axis=0, keepdims=True)
    pn = (jnp.concatenate(ps, axis=0) * (1.0 / jnp.maximum(l, 1e-30))).astype(bf16)
    o_win = lax.dot_general(kw, pn, tn, preferred_element_type=f32)[HEAD_DIM:2 * HEAD_DIM]

    gt = gt_ref[...]
    gate = [jnp.concatenate([gt[br * NSA_HEADS + h:br * NSA_HEADS + h + 1] for h in range(NSA_HEADS)], axis=1)
            for br in range(3)]
    o = gate[0] * ocmp_ref[...].astype(f32) + gate[1] * o_slc + gate[2] * o_win
    o = jnp.concatenate([o[:, h * TQ:(h + 1) * TQ] for h in range(NSA_HEADS)], axis=0)
    o_ref[...] = o.T.astype(bf16)


def nsa_attend(counts, lists, qta, kva, selT, ocmp, gT, slope_lane, *, B, S, TQ=Q_BLOCK):
    NB = S // TQ
    NSEL = S // SEL_BLOCK
    HW = NSA_HEADS * TQ
    grid_spec = pltpu.PrefetchScalarGridSpec(
        num_scalar_prefetch=1,
        grid=(B, NB),
        in_specs=[
            pl.BlockSpec((1, 1, NSEL), lambda b, i, cnt: (b * NB + i, 0, 0), memory_space=pltpu.SMEM),
            pl.BlockSpec((None, A_WIDTH, TQ), lambda b, i, cnt: (b, 0, i)),
            pl.BlockSpec((None, S, LANES), lambda b, i, cnt: (b, 0, 1)),
            pl.BlockSpec((None, S, LANES), lambda b, i, cnt: (b, 0, 2)),
            pl.BlockSpec((None, SUBLANES, TQ), lambda b, i, cnt: (b, 0, i)),
            pl.BlockSpec((None, None, HEAD_DIM, HW), lambda b, i, cnt: (b, i, 0, 0)),
            pl.BlockSpec((None, 32, TQ), lambda b, i, cnt: (b, 0, i)),
            pl.BlockSpec((1, HW), lambda b, i, cnt: (0, 0)),
        ],
        out_specs=pl.BlockSpec((None, TQ, A_WIDTH), lambda b, i, cnt: (b, i, 0)),
        scratch_shapes=[pltpu.VMEM((HEAD_DIM, HW), f32), pltpu.VMEM((1, HW), f32), pltpu.VMEM((1, HW), f32)],
    )
    return pl.pallas_call(
        functools.partial(_nsa_attend_kernel, TQ=TQ, S=S),
        grid_spec=grid_spec,
        out_shape=jax.ShapeDtypeStruct((B, S, A_WIDTH), bf16),
        compiler_params=_cparams(("arbitrary", "arbitrary")),
        name="nsa_attend",
    )(counts, lists, qta, kva, kva, selT, ocmp, gT, slope_lane)


POS_COLS = 16
NSA_CMP_CHUNK = 64


def _pos_operand(pos, extra=None):
    n = pos.shape[0]
    col = lax.broadcasted_iota(i32, (n, LANES), 1) - HEAD_DIM
    a = jnp.right_shift(pos, 7)
    b = jnp.bitwise_and(pos, 127)
    val = jnp.where((col >= 0) & (col < 6), jnp.where(col % 2 == 0, a, b), 0)
    if extra is not None:
        val = jnp.where(col == extra, 1, val)
    return val.astype(f32).astype(bf16)


def _nsa_select_kernel(qt_ref, kv_ref, cend_ref, posr_ref, ocmp_ref, sel_ref, flag_ref,
                       s_scr, ps_scr, o_scr, *, TQ, S, CH):
    NSEL = S // SEL_BLOCK
    R = SEL_BLOCK // CMP_STRIDE
    KSEL = min(SEL_TOPK, NSEL)
    HW = NSA_HEADS * TQ
    i = pl.program_id(1)
    posr = posr_ref[...]
    qpad = _q_all_heads(qt_ref[...], TQ, extra=jnp.concatenate([posr, jnp.zeros_like(posr)], axis=0))
    t = i * TQ + lax.broadcasted_iota(i32, (1, HW), 1) % TQ
    tn = (((0,), (0,)), ((), ()))
    nq = jnp.minimum((2 * i + 1) // CH + 1, NSEL // CH)
    ps_scr[...] = jnp.zeros_like(ps_scr)
    o_scr[...] = jnp.zeros_like(o_scr)

    def rows_of(r, q):
        return pl.ds(pl.multiple_of(r * NSEL + q * CH, CH), CH)

    def scores(q, m):
        for r in range(R):
            rows = rows_of(r, q)
            kv = kv_ref[rows, :]
            dist = jnp.maximum(i * TQ + (TQ - 1) - cend_ref[rows, :], 0)
            is_k = lax.broadcasted_iota(i32, kv.shape, 1) < HEAD_DIM
            s = jnp.dot(jnp.where(is_k, kv, _pos_operand(dist)), qpad, preferred_element_type=f32)
            s = jnp.where(cend_ref[rows, :] <= t, s, NEG)
            s_scr[r, pl.ds(pl.multiple_of(q * CH, CH), CH), :] = s
            m = jnp.maximum(m, jnp.max(s, axis=0, keepdims=True))
        return m

    m = lax.fori_loop(0, nq, scores, jnp.full((1, HW), NEG, f32))
    m = jnp.maximum(m, 0.1 * NEG)

    def probs(q, l):
        for r in range(R):
            sl = pl.ds(pl.multiple_of(q * CH, CH), CH)
            p = jnp.exp2(s_scr[r, sl, :] - m)
            s_scr[r, sl, :] = p
            l = l + jnp.sum(p, axis=0, keepdims=True)
            o_scr[...] += lax.dot_general(kv_ref[rows_of(r, q), :], p.astype(bf16), tn, preferred_element_type=f32)
        return l

    l = lax.fori_loop(0, nq, probs, jnp.zeros((1, HW), f32))
    inv = 1.0 / jnp.maximum(l, 1e-30)
    ocmp_ref[...] = (o_scr[HEAD_DIM:2 * HEAD_DIM, :] * inv).astype(ocmp_ref.dtype)

    def head_sums(q, carry):
        for r in range(R):
            sl = pl.ds(pl.multiple_of(q * CH, CH), CH)
            pn = s_scr[r, sl, :] * inv
            acc = pn[:, 0:TQ]
            for h in range(1, NSA_HEADS):
                acc = acc + pn[:, h * TQ:(h + 1) * TQ]
            ps_scr[r, sl, :] = acc
        return carry

    lax.fori_loop(0, nq, head_sums, 0)
    jrow = lax.broadcasted_iota(i32, (NSEL, TQ), 0)
    ps3 = ps_scr[3]
    prev3 = jnp.where(jrow == 0, 0.0, pltpu.roll(ps3, 1, axis=0))
    imp = 0.5 * prev3 + ps_scr[0] + ps_scr[1] + ps_scr[2] + 0.5 * ps3
    tq = i * TQ + lax.broadcasted_iota(i32, (1, TQ), 1)
    cur = tq // SEL_BLOCK
    forced = (jrow == 0) | (jrow == cur) | (jrow == cur - 1)
    causal = jrow <= cur
    score = jnp.where(causal, jnp.where(forced, BIG, imp), -1e38)
    jf = jrow.astype(f32)
    taken = jnp.zeros((NSEL, TQ), jnp.bool_)
    picks = []
    for _ in range(KSEL):
        mx = jnp.max(score, axis=0, keepdims=True)
        idx = jnp.min(jnp.where(score == mx, jf, float(NSEL)), axis=0, keepdims=True)
        hit = jf == idx
        taken = taken | hit
        score = jnp.where(hit, -jnp.inf, score)
        picks.append(idx)
    picks += [picks[-1]] * (SUBLANES - KSEL)
    sel_ref[...] = jnp.concatenate(picks, axis=0).astype(i32)
    used = jnp.where(taken & causal, 1.0, 0.0).astype(bf16)
    flag_ref[...] = lax.dot_general(jnp.ones((SUBLANES, TQ), bf16), used, (((1,), (1,)), ((), ())),
                                    preferred_element_type=f32)


def nsa_select(qta, kvcmp, cend, posr, *, B, S, TQ=Q_BLOCK):
    NB = S // TQ
    NCP = S // CMP_STRIDE
    NSEL = S // SEL_BLOCK
    R = SEL_BLOCK // CMP_STRIDE
    HW = NSA_HEADS * TQ
    CH = min(NSA_CMP_CHUNK, NSEL)
    return pl.pallas_call(
        functools.partial(_nsa_select_kernel, TQ=TQ, S=S, CH=CH),
        grid=(B, NB),
        in_specs=[
            pl.BlockSpec((None, A_WIDTH, TQ), lambda b, i: (b, 0, i)),
            pl.BlockSpec((None, NCP, LANES), lambda b, i: (b, 0, 0)),
            pl.BlockSpec((NCP, 1), lambda b, i: (0, 0)),
            pl.BlockSpec((SUBLANES, HW), lambda b, i: (0, 0)),
        ],
        out_specs=(
            pl.BlockSpec((None, None, HEAD_DIM, HW), lambda b, i: (b, i, 0, 0)),
            pl.BlockSpec((None, SUBLANES, TQ), lambda b, i: (b, 0, i)),
            pl.BlockSpec((None, None, SUBLANES, NSEL), lambda b, i: (b, i, 0, 0)),
        ),
        out_shape=(
            jax.ShapeDtypeStruct((B, NB, HEAD_DIM, HW), bf16),
            jax.ShapeDtypeStruct((B, SUBLANES, S), i32),
            jax.ShapeDtypeStruct((B, NB, SUBLANES, NSEL), f32),
        ),
        scratch_shapes=[pltpu.VMEM((R, NSEL, HW), f32), pltpu.VMEM((R, NSEL, TQ), f32), pltpu.VMEM((LANES, HW), f32)],
        compiler_params=_cparams(("arbitrary", "arbitrary")),
        name="nsa_select",
    )(qta, kvcmp, cend, posr)


_SLC_CHUNK = 4


def _nsa_attend_kernel(cnt_ref, lst_ref, qt_ref, kslc_ref, kwin_ref, sel_ref, ocmp_ref, gt_ref, posr_ref,
                       winb_ref, diagb_ref, o_ref, acc_scr, m_scr, l_scr, qa_scr, s_scr, p_scr, al_scr, *, TQ, S):
    NB = S // TQ
    NSEL = S // SEL_BLOCK
    HW = NSA_HEADS * TQ
    b = pl.program_id(0)
    i = pl.program_id(1)
    count = cnt_ref[b * NB + i]
    qpad = _q_all_heads(qt_ref[...], TQ)
    qa_scr[...] = qpad
    posr = posr_ref[...]
    sel = sel_ref[...]
    tn = (((0,), (0,)), ((), ()))

    kd = kslc_ref[pl.ds(pl.multiple_of(i * TQ, TQ), TQ), :]
    s = jnp.dot(kd, qpad, preferred_element_type=f32) + diagb_ref[...]
    m = jnp.max(s, axis=0, keepdims=True)
    p = jnp.exp2(s - m)
    m_scr[...] = m
    l_scr[...] = jnp.sum(p, axis=0, keepdims=True)
    acc_scr[...] = lax.dot_general(kd, p.astype(bf16), tn, preferred_element_type=f32)[HEAD_DIM:2 * HEAD_DIM]

    def gather(c):
        blks, js, oks = [], [], []
        for b4 in range(_SLC_CHUNK):
            idx = _SLC_CHUNK * c + b4
            ok = (idx >= 0) & (idx < count)
            j = jnp.where(ok, lst_ref[0, 0, jnp.clip(idx, 0, NSEL - 1)], 0)
            blks.append(kslc_ref[pl.ds(pl.multiple_of(j * SEL_BLOCK, SEL_BLOCK), SEL_BLOCK), :])
            js.append(j)
            oks.append(ok)
        return jnp.concatenate(blks, axis=0), js, oks

    def scores(c, par):
        kc, js, oks = gather(c)
        pos, pens = [], []
        for j, ok in zip(js, oks):
            pos.append(i * TQ + (TQ - 1) - j * SEL_BLOCK - lax.broadcasted_iota(i32, (SEL_BLOCK, 1), 0))
            picked = jnp.max(jnp.where(sel == j, 1.0, 0.0), axis=0, keepdims=True) > 0.0
            pen = jnp.where(picked & ok, 0.0, NEG)
            pens.append(jnp.concatenate([pen] * NSA_HEADS, axis=1))
        blk_id = lax.broadcasted_iota(i32, (_SLC_CHUNK * SEL_BLOCK, 1), 0) // SEL_BLOCK
        pm = _pos_operand(jnp.concatenate(pos, axis=0), extra=SUBLANES + blk_id)
        is_k = lax.broadcasted_iota(i32, kc.shape, 1) < HEAD_DIM
        pens += [jnp.zeros((1, HW), f32)] * (SUBLANES - _SLC_CHUNK)
        qa_scr[HEAD_DIM:HEAD_DIM + POS_COLS, :] = jnp.concatenate([posr, jnp.concatenate(pens, axis=0)],
                                                                  axis=0).astype(bf16)
        s_scr[par] = jnp.dot(jnp.where(is_k, kc, pm), qa_scr[...], preferred_element_type=f32)

    def weights(par):
        s = s_scr[par]
        m_old = m_scr[...]
        m_new = jnp.maximum(m_old, jnp.max(s, axis=0, keepdims=True))
        alpha = jnp.exp2(m_old - m_new)
        p = jnp.exp2(s - m_new)
        l_scr[...] = l_scr[...] * alpha + jnp.sum(p, axis=0, keepdims=True)
        p_scr[par] = p.astype(bf16)
        al_scr[par] = alpha
        m_scr[...] = m_new

    def values(c, par):
        kc, _, _ = gather(c)
        pv = lax.dot_general(kc, p_scr[par], tn, preferred_element_type=f32)
        acc_scr[...] = acc_scr[...] * al_scr[par] + pv[HEAD_DIM:2 * HEAD_DIM]

    nch = (count + _SLC_CHUNK - 1) // _SLC_CHUNK
    p_scr[1] = jnp.zeros_like(p_scr[1])
    al_scr[1] = jnp.ones_like(al_scr[1])
    scores(0, 0)

    def body(t, carry):
        c = 2 * t
        values(c - 1, 1)
        scores(c + 1, 1)
        weights(0)
        values(c, 0)
        scores(c + 2, 0)
        weights(1)
        return carry

    npairs = (nch + 1) // 2
    lax.fori_loop(0, npairs, body, 0)
    values(2 * npairs - 1, 1)
    o_slc = acc_scr[...] / l_scr[...]

    NWB = NSA_WINDOW // TQ + 1
    blks, biases = [], []
    for w in range(NWB):
        jb = i - (NWB - 1) + w
        blks.append(kwin_ref[pl.ds(pl.multiple_of(jnp.maximum(jb, 0) * TQ, TQ), TQ), :])
        biases.append(winb_ref[pl.ds(pl.multiple_of(jnp.where(jb >= 0, w, NWB) * TQ, TQ), TQ), :])
    kw = jnp.concatenate(blks, axis=0)
    s = jnp.dot(kw, qpad, preferred_element_type=f32)
    parts = [s[w * TQ:(w + 1) * TQ] + biases[w] for w in range(NWB)]
    m = parts[0].max(axis=0, keepdims=True)
    for part in parts[1:]:
        m = jnp.maximum(m, part.max(axis=0, keepdims=True))
    ps = [jnp.exp2(part - m) for part in parts]
    l = ps[0].sum(axis=0, keepdims=True)
    for part in ps[1:]:
        l = l + part.sum(axis=0, keepdims=True)
    pw = jnp.concatenate(ps, axis=0).astype(bf16)
    o_win = lax.dot_general(kw, pw, tn, preferred_element_type=f32)[HEAD_DIM:2 * HEAD_DIM] * (1.0 / l)

    gt = gt_ref[...]
    gate = [jnp.concatenate([gt[br * NSA_HEADS + h:br * NSA_HEADS + h + 1] for h in range(NSA_HEADS)], axis=1)
            for br in range(3)]
    o = gate[0] * ocmp_ref[...].astype(f32) + gate[1] * o_slc + gate[2] * o_win
    o = jnp.concatenate([o[:, h * TQ:(h + 1) * TQ] for h in range(NSA_HEADS)], axis=0)
    o_ref[...] = o.T.astype(bf16)


def nsa_attend(counts, lists, qta, kva, selT, ocmp, gT, posr, winb, diagb, *, B, S, TQ=Q_BLOCK):
    NB = S // TQ
    NSEL = S // SEL_BLOCK
    HW = NSA_HEADS * TQ
    NWB = NSA_WINDOW // TQ + 1
    const = lambda b, i, cnt: (0, 0)
    grid_spec = pltpu.PrefetchScalarGridSpec(
        num_scalar_prefetch=1,
        grid=(B, NB),
        in_specs=[
            pl.BlockSpec((1, 1, NSEL), lambda b, i, cnt: (b * NB + i, 0, 0), memory_space=pltpu.SMEM),
            pl.BlockSpec((None, A_WIDTH, TQ), lambda b, i, cnt: (b, 0, i)),
            pl.BlockSpec((None, S, LANES), lambda b, i, cnt: (b, 0, 1)),
            pl.BlockSpec((None, S, LANES), lambda b, i, cnt: (b, 0, 2)),
            pl.BlockSpec((None, SUBLANES, TQ), lambda b, i, cnt: (b, 0, i)),
            pl.BlockSpec((None, None, HEAD_DIM, HW), lambda b, i, cnt: (b, i, 0, 0)),
            pl.BlockSpec((None, 32, TQ), lambda b, i, cnt: (b, 0, i)),
            pl.BlockSpec((SUBLANES, HW), const),
            pl.BlockSpec(((NWB + 1) * TQ, HW), const),
            pl.BlockSpec((TQ, HW), const),
        ],
        out_specs=pl.BlockSpec((None, TQ, A_WIDTH), lambda b, i, cnt: (b, i, 0)),
        scratch_shapes=[pltpu.VMEM((HEAD_DIM, HW), f32), pltpu.VMEM((1, HW), f32), pltpu.VMEM((1, HW), f32),
                        pltpu.VMEM((LANES, HW), bf16),
                        pltpu.VMEM((2, _SLC_CHUNK * SEL_BLOCK, HW), f32),
                        pltpu.VMEM((2, _SLC_CHUNK * SEL_BLOCK, HW), bf16),
                        pltpu.VMEM((2, 1, HW), f32)],
    )
    return pl.pallas_call(
        functools.partial(_nsa_attend_kernel, TQ=TQ, S=S),
        grid_spec=grid_spec,
        out_shape=jax.ShapeDtypeStruct((B, S, A_WIDTH), bf16),
        compiler_params=_cparams(("arbitrary", "arbitrary")),
        name="nsa_attend",
    )(counts, lists, qta, kva, kva, selT, ocmp, gT, posr, winb, diagb)


def nsa_constants(slopes, S, TQ=Q_BLOCK):
    NSEL = S // SEL_BLOCK
    NCP = S // CMP_STRIDE
    R = SEL_BLOCK // CMP_STRIDE
    NWB = NSA_WINDOW // TQ + 1
    s2 = jnp.repeat(slopes * np.float32(np.log2(np.e)), TQ)
    hi = s2.astype(bf16).astype(f32)
    mid = (s2 - hi).astype(bf16).astype(f32)
    lo = (s2 - hi - mid).astype(bf16).astype(f32)
    zero = jnp.zeros_like(s2)
    posr = -jnp.stack([128.0 * hi, hi, 128.0 * mid, mid, 128.0 * lo, lo, zero, zero], axis=0)
    rho = np.arange(NCP)
    c = R * (rho % NSEL) + rho // NSEL
    cend = CMP_STRIDE * c + (CMP_LEN - 1)
    cend_mask = np.where(c < NCP - 1, cend, np.iinfo(np.int32).max)
    tl = jnp.tile(jnp.arange(TQ, dtype=i32), NSA_HEADS)[None, :]
    wpos = jnp.arange((NWB + 1) * TQ, dtype=i32)[:, None]
    dist = (NWB - 1) * TQ + tl - wpos
    winb = jnp.where((dist >= 0) & (dist < NSA_WINDOW) & (wpos < NWB * TQ),
                     -s2[None, :] * (NWB * TQ - 1 - wpos).astype(f32), NEG)
    dpos = jnp.arange(TQ, dtype=i32)[:, None]
    diagb = jnp.where(dpos <= tl, -s2[None, :] * (TQ - 1 - dpos).astype(f32), NEG)
    return (posr, jnp.asarray(cend_mask[:, None], i32), winb, diagb)


def nsa_branch(qta, kva, gT, cmp_w, nsa_c, *, B, S):
    NB = S // Q_BLOCK
    NSEL = S // SEL_BLOCK
    NCP = S // CMP_STRIDE
    R = SEL_BLOCK // CMP_STRIDE
    posr, cend, winb, diagb = nsa_c
    kva = kva.reshape(B, S, 6 * HEAD_DIM)
    ch = kva[:, :, 0:LANES].reshape(B, NCP, CMP_STRIDE * LANES)
    kvcmp = nsa_compress(ch, *cmp_w, B=B, S=S)
    kvcmp = kvcmp.reshape(B, NSEL, R, LANES).swapaxes(1, 2).reshape(B, NCP, LANES)
    ocmp, selT, flags = nsa_select(qta, kvcmp, cend, posr, B=B, S=S)
    ar = jnp.arange(NSEL, dtype=i32)
    own = (Q_BLOCK // SEL_BLOCK) * jnp.arange(NB, dtype=i32)
    used = (flags[:, :, 0, :] > 0.5) & (ar[None, None, :] < own[None, :, None])
    lists = jnp.minimum(jnp.sort(jnp.where(used, ar, ar + NSEL), axis=-1), NSEL - 1)
    counts = jnp.sum(used, axis=-1).astype(i32)
    return nsa_attend(counts.reshape(B * NB), lists.reshape(B * NB, 1, NSEL), qta, kva, selT, ocmp, gT,
                      posr, winb, diagb, B=B, S=S)


def _route(logit, rb):
    sc = jax.nn.sigmoid(logit)
    sel = sc + rb
    srow = [sel[e:e + 1] for e in range(N_EXPERTS)]
    crow = [sc[e:e + 1] for e in range(N_EXPERTS)]
    gscore = []
    for g in range(N_GROUPS):
        a, b, c, d = srow[4 * g:4 * g + 4]
        top2 = jnp.maximum(jnp.maximum(jnp.maximum(a + b, a + c), jnp.maximum(a + d, b + c)),
                           jnp.maximum(b + d, c + d))
        gscore.append(top2)
    best, gi = gscore[0], jnp.zeros_like(gscore[0], dtype=i32)
    for g in range(1, N_GROUPS):
        better = gscore[g] > best
        gi = jnp.where(better, g, gi)
        best = jnp.where(better, gscore[g], best)

    def pick_group(rows, k):
        v = rows[k]
        for g in range(1, N_GROUPS):
            v = jnp.where(gi == g, rows[4 * g + k], v)
        return v

    iv = [pick_group(srow, k) for k in range(EXPERTS_PER_GROUP)]
    ic = [pick_group(crow, k) for k in range(EXPERTS_PER_GROUP)]
    b1, i1, w1 = iv[0], jnp.zeros_like(gi), ic[0]
    for k in range(1, EXPERTS_PER_GROUP):
        better = iv[k] > b1
        i1 = jnp.where(better, k, i1)
        w1 = jnp.where(better, ic[k], w1)
        b1 = jnp.where(better, iv[k], b1)
    b2 = jnp.full_like(b1, -jnp.inf)
    i2, w2 = jnp.zeros_like(gi), jnp.zeros_like(w1)
    for k in range(EXPERTS_PER_GROUP):
        better = (i1 != k) & (iv[k] > b2)
        i2 = jnp.where(better, k, i2)
        w2 = jnp.where(better, ic[k], w2)
        b2 = jnp.where(better, iv[k], b2)
    tot = w1 + w2
    eidx = jnp.concatenate([gi * EXPERTS_PER_GROUP + i1, gi * EXPERTS_PER_GROUP + i2], axis=0)
    ew = jnp.concatenate([w1 / tot, w2 / tot], axis=0)
    return eidx, ew


def _mix_out_kernel(x_ref, a_ref, sh_ref, ga_ref, oa_ref, ob_ref, oc_ref, wm_ref, wa_ref, wb_ref, wc_ref, wo_ref,
                    am_ref, shm_ref, wr_ref, rb_ref, xo_ref, h2_ref, eidx_ref, ew_ref):
    x = x_ref[...]
    h = _norm_mod(x, a_ref[...], sh_ref[...]).astype(bf16)
    gates = jax.nn.sigmoid(jnp.dot(h, wm_ref[...], preferred_element_type=f32))
    D = D_MODEL
    y = gates[:, 0:D] * jnp.dot(oa_ref[...], wa_ref[...], preferred_element_type=f32)
    y = y + gates[:, D:2 * D] * jnp.dot(ob_ref[...], wb_ref[...], preferred_element_type=f32)
    y = y + gates[:, 2 * D:3 * D] * jnp.dot(oc_ref[...], wc_ref[...], preferred_element_type=f32)
    xn = x + ga_ref[...] * jnp.dot(y.astype(bf16), wo_ref[...], preferred_element_type=f32)
    xo_ref[...] = xn
    h2 = _norm_mod(xn, am_ref[...], shm_ref[...])
    hi = h2.astype(bf16)
    h2_ref[...] = hi
    lo = (h2 - hi.astype(f32)).astype(bf16)
    wr = wr_ref[...]
    whi = wr.astype(bf16)
    wlo = (wr - whi.astype(f32)).astype(bf16)
    nt = lambda p, q: lax.dot_general(p, q, (((1,), (1,)), ((), ())), preferred_element_type=f32)
    logit = nt(whi, hi) + nt(whi, lo) + nt(wlo, hi)
    eidx, ew = _route(logit, rb_ref[...])
    eidx_ref[...] = eidx
    ew_ref[...] = ew


def mix_out(x2, a_mod, sh_mod, g_a, oa, ob, oc, wm, wa, wb, wc, wo, am_moe, shm_moe, wrT, rb, *, B, S, tm=512):
    N, D = x2.shape
    tpb = S // tm
    row = lambda i: (i, 0)
    bmap = lambda i: (i // tpb, 0, 0)
    full = lambda i: (0, 0)
    col = lambda i: (0, i)
    return pl.pallas_call(
        _mix_out_kernel,
        grid=(N // tm,),
        in_specs=[
            pl.BlockSpec((tm, D), row),
            pl.BlockSpec((None, 1, D), bmap), pl.BlockSpec((None, 1, D), bmap), pl.BlockSpec((None, 1, D), bmap),
            pl.BlockSpec((tm, A_WIDTH), row), pl.BlockSpec((tm, B_WIDTH), row), pl.BlockSpec((tm, C_WIDTH), row),
            pl.BlockSpec((D, 3 * D), full), pl.BlockSpec((A_WIDTH, D), full), pl.BlockSpec((B_WIDTH, D), full),
            pl.BlockSpec((C_WIDTH, D), full), pl.BlockSpec((D, D), full),
            pl.BlockSpec((None, 1, D), bmap), pl.BlockSpec((None, 1, D), bmap),
            pl.BlockSpec((N_EXPERTS, D), full), pl.BlockSpec((N_EXPERTS, 1), full),
        ],
        out_specs=(pl.BlockSpec((tm, D), row), pl.BlockSpec((tm, D), row),
                   pl.BlockSpec((TOP_K, tm), col), pl.BlockSpec((TOP_K, tm), col)),
        out_shape=(jax.ShapeDtypeStruct((N, D), f32), jax.ShapeDtypeStruct((N, D), bf16),
                   jax.ShapeDtypeStruct((TOP_K, N), i32), jax.ShapeDtypeStruct((TOP_K, N), f32)),
        compiler_params=_cparams(("arbitrary",)),
        name="mix_out",
    )(x2, a_mod, sh_mod, g_a, oa, ob, oc, wm, wa, wb, wc, wo, am_moe, shm_moe, wrT, rb)


def _moe_ffn_kernel(be_ref, nu_ref, xs_ref, wg_ref, wu_ref, wd_ref, y_ref, wg_s, wu_s, wd_s):
    i = pl.program_id(0)

    @pl.when((i == 0) | (be_ref[i] != be_ref[jnp.maximum(i - 1, 0)]))
    def _():
        wg_s[...] = wg_ref[...].astype(bf16)
        wu_s[...] = wu_ref[...].astype(bf16)
        wd_s[...] = wd_ref[...].astype(bf16)

    @pl.when(i < nu_ref[0])
    def _():
        xs = xs_ref[...]
        g = jnp.dot(xs, wg_s[...], preferred_element_type=f32)
        u = jnp.dot(xs, wu_s[...], preferred_element_type=f32)
        a = (jax.nn.silu(g) * u).astype(bf16)
        y_ref[...] = jnp.dot(a, wd_s[...], preferred_element_type=f32).astype(y_ref.dtype)

    @pl.when(i >= nu_ref[0])
    def _():
        y_ref[...] = jnp.zeros_like(y_ref)


def moe_ffn(blk_e, n_used, xs, wg, wu, wd, *, layer):
    n_rows, D = xs.shape
    n_blk = n_rows // MOE_BLOCK
    wmap = lambda i, be, nu: (layer, be[i], 0, 0)
    grid_spec = pltpu.PrefetchScalarGridSpec(
        num_scalar_prefetch=2,
        grid=(n_blk,),
        in_specs=[
            pl.BlockSpec((MOE_BLOCK, D), lambda i, be, nu: (i, 0)),
            pl.BlockSpec((None, None, D, D_EXPERT), wmap),
            pl.BlockSpec((None, None, D, D_EXPERT), wmap),
            pl.BlockSpec((None, None, D_EXPERT, D), wmap),
        ],
        out_specs=pl.BlockSpec((MOE_BLOCK, D), lambda i, be, nu: (i, 0)),
        scratch_shapes=[pltpu.VMEM((D, D_EXPERT), bf16), pltpu.VMEM((D, D_EXPERT), bf16),
                        pltpu.VMEM((D_EXPERT, D), bf16)],
    )
    return pl.pallas_call(
        _moe_ffn_kernel,
        grid_spec=grid_spec,
        out_shape=jax.ShapeDtypeStruct((n_rows, D), bf16),
        compiler_params=_cparams(("arbitrary",)),
        name="moe_ffn",
    )(blk_e, n_used, xs, wg, wu, wd)


def moe_dispatch(eidx, N):
    NK = N * TOP_K
    n_blk = -(-NK // MOE_BLOCK) + N_EXPERTS
    n_slot = n_blk * MOE_BLOCK
    experts = jnp.arange(N_EXPERTS, dtype=i32)
    flat_e = eidx.T.reshape(-1)
    counts = jnp.sum(flat_e[:, None] == experts[None, :], axis=0, dtype=i32)
    padded = (counts + MOE_BLOCK - 1) // MOE_BLOCK * MOE_BLOCK
    p_end = jnp.cumsum(padded)
    blk_first = jnp.arange(n_blk, dtype=i32) * MOE_BLOCK
    blk_e = jnp.minimum(jnp.sum(p_end[None, :] <= blk_first[:, None], axis=1, dtype=i32), N_EXPERTS - 1)
    n_used = (p_end[-1] // MOE_BLOCK).astype(i32).reshape(1)
    stride = NK + MOE_BLOCK
    pair = jnp.arange(NK, dtype=i32)
    q = jnp.arange(MOE_BLOCK, dtype=i32)
    pad_key = jnp.where(q[None, :] < (padded - counts)[:, None], experts[:, None] * stride + NK + q[None, :],
                        jnp.iinfo(jnp.int32).max)
    keys = jnp.concatenate([flat_e * stride + pair, pad_key.reshape(-1)])
    pair_or_pad = jnp.concatenate([pair, jnp.full((n_slot - NK,), NK, i32)])
    _, slot_pair = lax.sort((keys, pair_or_pad), num_keys=1)
    slot_tok = jnp.minimum(slot_pair // TOP_K, N - 1)
    _, pos = lax.sort((slot_pair, jnp.arange(n_slot, dtype=i32)), num_keys=1)
    return slot_tok, blk_e, n_used, pos[:NK].reshape(N, TOP_K)


def _prep_cmp_weights(pe_k, w1_k, w2_k, pe_v, w1_v, w2_v, k_gain0):
    T = CMP_STRIDE
    w1big = jnp.zeros((T, 2, HEAD_DIM, 4, CMP_HID), f32)
    w1k = w1_k.reshape(2, T, HEAD_DIM, CMP_HID)
    w1v = w1_v.reshape(2, T, HEAD_DIM, CMP_HID)
    w1big = w1big.at[:, 0, :, 0].set(w1k[0]).at[:, 1, :, 1].set(w1v[0])
    w1big = w1big.at[:, 0, :, 2].set(w1k[1]).at[:, 1, :, 3].set(w1v[1])
    w1big = w1big.reshape(T * LANES, 4 * CMP_HID).astype(bf16)
    pe = jnp.stack([pe_k.reshape(2, T, HEAD_DIM), pe_v.reshape(2, T, HEAD_DIM)], axis=2)
    pe2 = jnp.zeros((SUBLANES, T * LANES), f32).at[0:2].set(pe.reshape(2, T * LANES)).astype(bf16)
    w2big = jnp.zeros((2 * CMP_HID, LANES), f32)
    w2big = w2big.at[0:CMP_HID, 0:HEAD_DIM].set(w2_k).at[CMP_HID:, HEAD_DIM:].set(w2_v).astype(bf16)
    gain = jnp.concatenate([k_gain0, jnp.ones((HEAD_DIM,), f32)])[None, :]
    return w1big, pe2, w2big, gain


def _prep_proj_weights(w_in, nsa_q_gain, nsa_k_gain, swa_q_gain, swa_k_gain):
    w = w_in
    wtok = jnp.concatenate([w[:, _O_KVA:_O_GA], w[:, _O_KB:_O_VB], w[:, _O_KC:_O_MG]], axis=1).astype(bf16)
    gate_perm = np.array([h * 3 + br for br in range(3) for h in range(NSA_HEADS)])
    scale = HEAD_DIM ** -0.5
    wch = jnp.concatenate([w[:, _O_QA:_O_KVA], w[:, _O_QB:_O_KB] * (0.5 * scale), w[:, _O_QC:_O_KC],
                           w[:, _O_VB:_O_QC], w[:, _O_GA:_O_QB][:, gate_perm], jnp.zeros((D_MODEL, 8), f32)],
                          axis=1).T.astype(bf16)
    gtok = jnp.zeros((1, TOK_W), f32)
    gtok = gtok.at[0, 128:192].set(nsa_k_gain[1]).at[0, 256:320].set(nsa_k_gain[2])
    gtok = gtok.at[0, 640:704].set(swa_k_gain).at[0, 704:768].set(swa_k_gain)
    mtok = jnp.zeros((1, TOK_W), f32).at[0, 128:192].set(1.0).at[0, 256:320].set(1.0).at[0, 640:768].set(1.0)
    gcha = (jnp.tile(nsa_q_gain, NSA_HEADS) * (scale * np.float32(np.log2(np.e))))[:, None]
    gchc = (jnp.tile(swa_q_gain, SWA_HEADS) * scale)[:, None]
    blk = np.arange(LANES) // HEAD_DIM
    bd = jnp.asarray(blk[:, None] == blk[None, :], bf16)
    return wtok, wch, gtok, mtok, gcha, gchc, bd


def _adaln_kernel(c_ref, w_ref, b_ref, o_ref):
    cond = jax.nn.silu(c_ref[...]).astype(bf16)
    o_ref[...] = jnp.dot(cond, w_ref[...].astype(bf16), preferred_element_type=f32) + b_ref[...]


def adaln(c_pad, w_ada, b_ada):
    L, D, D6 = w_ada.shape
    return pl.pallas_call(
        _adaln_kernel,
        grid=(L, D6 // D),
        in_specs=[
            pl.BlockSpec((SUBLANES, D), lambda l, j: (0, 0)),
            pl.BlockSpec((None, D, D), lambda l, j: (l, 0, j)),
            pl.BlockSpec((None, 1, D), lambda l, j: (l, 0, j)),
        ],
        out_specs=pl.BlockSpec((None, SUBLANES, D), lambda l, j: (l, 0, j)),
        out_shape=jax.ShapeDtypeStruct((L, SUBLANES, D6), f32),
        compiler_params=_cparams(("arbitrary", "arbitrary")),
        name="adaln",
    )(c_pad, w_ada, b_ada.reshape(L, 1, D6))


def _moe_combine_kernel(x_ref, y0_ref, y1_ref, w_ref, gm_ref, o_ref):
    w = w_ref[...]
    y = y0_ref[...].astype(f32) * w[:, 0:1] + y1_ref[...].astype(f32) * w[:, 1:2]
    o_ref[...] = x_ref[...] + gm_ref[...] * y


def moe_combine(x2, y0, y1, ew_tok, g_m, *, B, S, tm=512):
    N, D = x2.shape
    tpb = S // tm
    row = lambda i: (i, 0)
    return pl.pallas_call(
        _moe_combine_kernel,
        grid=(N // tm,),
        in_specs=[pl.BlockSpec((tm, D), row), pl.BlockSpec((tm, D), row), pl.BlockSpec((tm, D), row),
                  pl.BlockSpec((tm, TOP_K), row), pl.BlockSpec((None, 1, D), lambda i: (i // tpb, 0, 0))],
        out_specs=pl.BlockSpec((tm, D), row),
        out_shape=jax.ShapeDtypeStruct((N, D), f32),
        compiler_params=_cparams(("arbitrary",)),
        name="moe_combine",
    )(x2, y0, y1, ew_tok, g_m)


def kernel(x, c, w_ada, b_ada, g_norm_mix, g_norm_moe, w_in, cmp_pe_k, cmp_w1_k, cmp_w2_k, cmp_pe_v, cmp_w1_v,
           cmp_w2_v, nsa_q_gain, nsa_k_gain, swa_q_gain, swa_k_gain, swa_sinks, w_branch_a, w_branch_b,
           w_branch_c, w_out, w_router, router_bias, w_exp_gate, w_exp_up, w_exp_down):
    B, S, D = x.shape
    N = B * S
    L = w_ada.shape[0]
    slopes = 2.0 ** (-8.0 * jnp.arange(1, SWA_HEADS + NSA_HEADS + 1, dtype=f32) / (SWA_HEADS + NSA_HEADS))
    swa_slope_l = jnp.repeat(slopes[:SWA_HEADS].reshape(SWA_KV, SWA_HEADS // SWA_KV), Q_BLOCK, axis=1)
    nsa_c = nsa_constants(slopes[SWA_HEADS:], S)
    c_pad = jnp.zeros((SUBLANES, D), f32).at[:B].set(c)
    mod_all = adaln(c_pad, w_ada, b_ada)[:, :B]
    wrT = w_router.T
    rb = router_bias[:, None]
    x2 = x.reshape(N, D)
    for l in range(L):
        sh_a, sc_a, g_a, sh_m, sc_m, g_m = [m[:, None, :] for m in jnp.split(mod_all[l], 6, axis=-1)]
        a_mix = g_norm_mix[l][None, None, :] * (1.0 + sc_a)
        a_moe = g_norm_moe[l][None, None, :] * (1.0 + sc_m)
        pw = _prep_proj_weights(w_in[l], nsa_q_gain[l], nsa_k_gain[l], swa_q_gain[l], swa_k_gain[l])
        kva, kb, kvc, qta, qtb, qtc, vtb, gT = proj_in(x2, a_mix, sh_a, *pw, B=B, S=S)
        cw = _prep_cmp_weights(cmp_pe_k[l], cmp_w1_k[l], cmp_w2_k[l], cmp_pe_v[l], cmp_w1_v[l], cmp_w2_v[l],
                               nsa_k_gain[l][0])
        o_a = nsa_branch(qta, kva, gT, cw, nsa_c, B=B, S=S)
        o_b = sb_attention(_perm_rows(kb.reshape(B, S, B_WIDTH), S), _perm_lanes(vtb, S), qtb, B=B, S=S)
        sink_l = jnp.repeat(swa_sinks[l].reshape(SWA_KV, SWA_HEADS // SWA_KV), Q_BLOCK, axis=1)
        o_c = swa_attention(kvc.reshape(B, S, 2 * SWA_KV * HEAD_DIM), qtc, swa_slope_l, sink_l, B=B, S=S)
        x2, h2, eidx, ew = mix_out(
            x2, a_mix, sh_a, g_a, o_a.reshape(N, A_WIDTH), o_b.reshape(N, B_WIDTH), o_c.reshape(N, C_WIDTH),
            w_in[l][:, _O_MG:_O_END].astype(bf16), w_branch_a[l].astype(bf16), w_branch_b[l].astype(bf16),
            w_branch_c[l].astype(bf16), w_out[l].astype(bf16), a_moe, sh_m, wrT, rb, B=B, S=S)
        slot_tok, blk_e, n_used, pos = moe_dispatch(eidx, N)
        y = moe_ffn(blk_e, n_used, h2[slot_tok], w_exp_gate, w_exp_up, w_exp_down, layer=l)
        x2 = moe_combine(x2, y[pos[:, 0]], y[pos[:, 1]], ew.T, g_m, B=B, S=S)
    return x2.reshape(B, S, D)
```

```python
import functools

import numpy as np
import jax
import jax.numpy as jnp
from jax import lax
from jax.experimental import pallas as pl
from jax.experimental.pallas import tpu as pltpu

f32 = jnp.float32
bf16 = jnp.bfloat16
i32 = jnp.int32

D_MODEL = 1024
HEAD_DIM = 64
Q_BLOCK = 128
NSA_HEADS = 8
CMP_STRIDE = 16
CMP_LEN = 32
CMP_HID = 256
SEL_BLOCK = 64
SEL_TOPK = 8
NSA_WINDOW = 512
SB_HEADS = 4
SWA_HEADS = 4
SWA_KV = 2
SWA_WINDOW = 128
N_EXPERTS = 16
N_GROUPS = 4
EXPERTS_PER_GROUP = 4
TOP_K = 2
D_EXPERT = 512
MOE_BLOCK = 512
EPS = 1e-6
NEG = -1e30
BIG = 1e30

A_WIDTH = NSA_HEADS * HEAD_DIM
B_WIDTH = SB_HEADS * HEAD_DIM
C_WIDTH = SWA_HEADS * HEAD_DIM

LANES = 128
SUBLANES = 8
VMEM_LIMIT = 56 * 1024 * 1024

_O_QA = 0
_O_KVA = _O_QA + A_WIDTH
_O_GA = _O_KVA + 6 * HEAD_DIM
_O_QB = _O_GA + 3 * NSA_HEADS
_O_KB = _O_QB + B_WIDTH
_O_VB = _O_KB + B_WIDTH
_O_QC = _O_VB + B_WIDTH
_O_KC = _O_QC + C_WIDTH
_O_VC = _O_KC + SWA_KV * HEAD_DIM
_O_MG = _O_VC + SWA_KV * HEAD_DIM
_O_END = _O_MG + 3 * D_MODEL

TOK_W = 6 * HEAD_DIM + B_WIDTH + 2 * SWA_KV * HEAD_DIM
CH_W = A_WIDTH + B_WIDTH + C_WIDTH + B_WIDTH + 32
_TOK_NORM_CHUNKS = (1, 2, 5)


def _cparams(sem):
    return pltpu.CompilerParams(dimension_semantics=sem, vmem_limit_bytes=VMEM_LIMIT)


def _split_dot(x, w):
    hi = x.astype(bf16)
    lo = (x - hi.astype(f32)).astype(bf16)
    return jnp.dot(hi, w, preferred_element_type=f32) + jnp.dot(lo, w, preferred_element_type=f32)


def _norm_mod(x, a, sh):
    rs = lax.rsqrt(jnp.mean(x * x, axis=-1, keepdims=True) + EPS)
    return x * rs * a + sh


def _group_norm_ch(y, gain_col, n, tm):
    y3 = y.reshape(n, HEAD_DIM, tm)
    ss = jnp.sum(y3 * y3, axis=1, keepdims=True)
    yn = y3 * lax.rsqrt(ss * (1.0 / HEAD_DIM) + EPS)
    return yn.reshape(n * HEAD_DIM, tm) * gain_col


def _proj_in_kernel(x_ref, a_ref, sh_ref, wtok_ref, wch_ref, gtok_ref, mtok_ref, gcha_ref, gchc_ref, bd_ref,
                    kva_ref, kb_ref, kvc_ref, qta_ref, qtb_ref, qtc_ref, vtb_ref, gt_ref, *, tm):
    h = _norm_mod(x_ref[...], a_ref[...], sh_ref[...]).astype(bf16)
    t = jnp.dot(h, wtok_ref[...], preferred_element_type=f32)
    cols = [t[:, c * LANES:(c + 1) * LANES] for c in range(TOK_W // LANES)]
    for c in _TOK_NORM_CHUNKS:
        y = cols[c]
        ss = _split_dot(y * y, bd_ref[...])
        yn = y * lax.rsqrt(ss * (1.0 / HEAD_DIM) + EPS) * gtok_ref[:, c * LANES:(c + 1) * LANES]
        cols[c] = jnp.where(mtok_ref[:, c * LANES:(c + 1) * LANES] > 0.0, yn, y)
    t = jnp.concatenate(cols, axis=1).astype(bf16)
    kva_ref[...] = t[:, 0:384]
    kb_ref[...] = t[:, 384:640]
    kvc_ref[...] = t[:, 640:896]
    c = lax.dot_general(wch_ref[...], h, (((1,), (1,)), ((), ())), preferred_element_type=f32)
    qta_ref[...] = _group_norm_ch(c[0:512], gcha_ref[...], NSA_HEADS, tm).astype(bf16)
    qtb_ref[...] = c[512:768].astype(bf16)
    qtc_ref[...] = _group_norm_ch(c[768:1024], gchc_ref[...], SWA_HEADS, tm).astype(bf16)
    vtb_ref[...] = c[1024:1280].astype(bf16)
    gt_ref[...] = jax.nn.sigmoid(c[1280:1312])


def proj_in(x2, a_mod, sh_mod, wtok, wch, gtok, mtok, gcha, gchc, bd, *, B, S, tm=512):
    N, D = x2.shape
    tpb = S // tm
    row = lambda i: (i, 0)
    bmap = lambda i: (i // tpb, 0, 0)
    cmap = lambda i: (i // tpb, 0, i % tpb)
    full = lambda i: (0, 0)
    out_shape = (
        jax.ShapeDtypeStruct((N, 384), bf16), jax.ShapeDtypeStruct((N, 256), bf16), jax.ShapeDtypeStruct((N, 256), bf16),
        jax.ShapeDtypeStruct((B, 512, S), bf16), jax.ShapeDtypeStruct((B, 256, S), bf16),
        jax.ShapeDtypeStruct((B, 256, S), bf16), jax.ShapeDtypeStruct((B, 256, S), bf16),
        jax.ShapeDtypeStruct((B, 32, S), f32),
    )
    return pl.pallas_call(
        functools.partial(_proj_in_kernel, tm=tm),
        grid=(N // tm,),
        in_specs=[
            pl.BlockSpec((tm, D), row),
            pl.BlockSpec((None, 1, D), bmap), pl.BlockSpec((None, 1, D), bmap),
            pl.BlockSpec((D, TOK_W), full), pl.BlockSpec((CH_W, D), full),
            pl.BlockSpec((1, TOK_W), full), pl.BlockSpec((1, TOK_W), full),
            pl.BlockSpec((512, 1), full), pl.BlockSpec((256, 1), full),
            pl.BlockSpec((LANES, LANES), full),
        ],
        out_specs=(
            pl.BlockSpec((tm, 384), row), pl.BlockSpec((tm, 256), row), pl.BlockSpec((tm, 256), row),
            pl.BlockSpec((None, 512, tm), cmap), pl.BlockSpec((None, 256, tm), cmap),
            pl.BlockSpec((None, 256, tm), cmap), pl.BlockSpec((None, 256, tm), cmap),
            pl.BlockSpec((None, 32, tm), cmap),
        ),
        out_shape=out_shape,
        compiler_params=_cparams(("arbitrary",)),
        name="proj_in",
    )(x2, a_mod, sh_mod, wtok, wch, gtok, mtok, gcha, gchc, bd)


def _excl_suffix_prod8(t, rows):
    def shift_up(x, k):
        return jnp.where(rows < SUBLANES - k, pltpu.roll(x, SUBLANES - k, axis=0), 1.0)
    x = shift_up(t, 1)
    x = x * shift_up(x, 1)
    x = x * shift_up(x, 2)
    x = x * shift_up(x, 4)
    return x


SB_TK = 128
SB_TQ = 512
SB_NH = 2


def _sb_scores(k_ref, qpad, z_scr, j, slot):
    kb = k_ref[pl.ds(pl.multiple_of(j * SB_TK, SB_TK), SB_TK), :]
    for h in range(SB_NH):
        z_scr[slot, h] = jnp.dot(kb, qpad[h], preferred_element_type=f32)


def _sb_weights(z_scr, p_scr, a_scr, slot, laters, mask_off, TQ):
    V = SB_TK // SUBLANES
    rows = lax.broadcasted_iota(i32, (SUBLANES, TQ), 0)
    qidx = lax.broadcasted_iota(i32, (SUBLANES, TQ), 1)
    new_laters = []
    for h in range(SB_NH):
        run = jnp.ones((SUBLANES, TQ), f32)
        for v in reversed(range(V)):
            sl = slice(v * SUBLANES, (v + 1) * SUBLANES)
            beta = 0.5 - 0.5 * jnp.tanh(z_scr[slot, h, sl, :])
            if mask_off is not None:
                beta = jnp.where(rows * V + (v + mask_off) < qidx, beta, 1.0)
            nxt = run * beta
            p_scr[slot % 2, h, sl, :] = run - nxt
            run = nxt
        g = _excl_suffix_prod8(run, rows) * laters[h]
        for v2 in range(V // 2):
            sl = slice(2 * v2 * SUBLANES, (2 * v2 + 2) * SUBLANES)
            a_scr[slot, h, sl, :] = (p_scr[slot % 2, h, sl, :] * jnp.concatenate([g, g], axis=0)).astype(bf16)
        new_laters.append((g * run)[0:1, :])
    return new_laters


def _sb_accumulate(vt_ref, a_scr, acc_scr, j, slot):
    vtb = vt_ref[:, pl.ds(pl.multiple_of(j * SB_TK, SB_TK), SB_TK)]
    for h in range(SB_NH):
        acc_scr[h] += jnp.dot(vtb[h * HEAD_DIM:(h + 1) * HEAD_DIM], a_scr[slot, h], preferred_element_type=f32)


def _sb_kernel(k_ref, vt_ref, qt_ref, o_ref, acc_scr, z_scr, p_scr, a_scr, qpad, *, TQ):
    NBQ = TQ // SB_TK
    i = pl.program_id(2)
    qpad[...] = jnp.zeros_like(qpad)
    for h in range(SB_NH):
        qpad[h, h * HEAD_DIM:(h + 1) * HEAD_DIM, :] = qt_ref[h * HEAD_DIM:(h + 1) * HEAD_DIM, :]
    acc_scr[...] = jnp.zeros_like(acc_scr)
    a_scr[2] = jnp.zeros_like(a_scr[2])
    a_scr[3] = jnp.zeros_like(a_scr[3])
    j0 = i * NBQ + NBQ - 1
    assert NBQ % 4 == 0

    def step(j, par, laters, mask_offs):
        o = 2 * (1 - par)
        _sb_scores(k_ref, qpad, z_scr, jnp.maximum(j - 2, 0), o)
        _sb_scores(k_ref, qpad, z_scr, jnp.maximum(j - 3, 0), o + 1)
        _sb_accumulate(vt_ref, a_scr, acc_scr, jnp.minimum(j + 2, j0), o)
        _sb_accumulate(vt_ref, a_scr, acc_scr, jnp.minimum(j + 1, j0), o + 1)
        laters = _sb_weights(z_scr, p_scr, a_scr, 2 * par, laters, mask_offs[0], TQ)
        laters = _sb_weights(z_scr, p_scr, a_scr, 2 * par + 1, laters, mask_offs[1], TQ)
        return laters

    _sb_scores(k_ref, qpad, z_scr, j0, 0)
    _sb_scores(k_ref, qpad, z_scr, j0 - 1, 1)
    laters = [jnp.ones((1, TQ), f32)] * SB_NH
    for d in range(NBQ // 2):
        jd = NBQ - 1 - 2 * d
        laters = step(i * NBQ + jd, d % 2, laters, (jd * SB_TK, (jd - 1) * SB_TK))

    def body(t, carry):
        j = i * NBQ - 1 - 4 * t
        l = step(j, 0, list(carry), (None, None))
        return tuple(step(j - 2, 1, l, (None, None)))

    lax.fori_loop(0, i * (NBQ // 4), body, tuple(laters))
    _sb_accumulate(vt_ref, a_scr, acc_scr, 1, 2)
    _sb_accumulate(vt_ref, a_scr, acc_scr, 0, 3)
    o_ref[...] = jnp.concatenate([acc_scr[h].T for h in range(SB_NH)], axis=1).astype(bf16)


def sb_attention(kb_perm, vtb_perm, qtb, *, B, S, TQ=SB_TQ):
    TQ = min(TQ, S)
    W = SB_NH * HEAD_DIM
    return pl.pallas_call(
        functools.partial(_sb_kernel, TQ=TQ),
        grid=(B, SB_HEADS // SB_NH, S // TQ),
        in_specs=[
            pl.BlockSpec((None, S, W), lambda b, hp, i: (b, 0, hp)),
            pl.BlockSpec((None, W, S), lambda b, hp, i: (b, hp, 0)),
            pl.BlockSpec((None, W, TQ), lambda b, hp, i: (b, hp, i)),
        ],
        out_specs=pl.BlockSpec((None, TQ, W), lambda b, hp, i: (b, i, hp)),
        out_shape=jax.ShapeDtypeStruct((B, S, B_WIDTH), bf16),
        scratch_shapes=[pltpu.VMEM((SB_NH, HEAD_DIM, TQ), f32), pltpu.VMEM((4, SB_NH, SB_TK, TQ), f32),
                        pltpu.VMEM((2, SB_NH, SB_TK, TQ), f32), pltpu.VMEM((4, SB_NH, SB_TK, TQ), bf16),
                        pltpu.VMEM((SB_NH, W, TQ), bf16)],
        compiler_params=_cparams(("arbitrary", "arbitrary", "arbitrary")),
        name="sb_attention",
    )(kb_perm, vtb_perm, qtb)


def _perm_rows(z, S):
    B = z.shape[0]
    V = SB_TK // SUBLANES
    return z.reshape(B, S // SB_TK, SUBLANES, V, z.shape[-1]).swapaxes(2, 3).reshape(B, S, z.shape[-1])


def _perm_lanes(z, S):
    B, C = z.shape[0], z.shape[1]
    V = SB_TK // SUBLANES
    return z.reshape(B, C, S // SB_TK, SUBLANES, V).swapaxes(3, 4).reshape(B, C, S)


def _swa_kernel(kp_ref, kc_ref, vp_ref, vc_ref, qt_ref, slope_ref, sink_ref, o_ref, *, TQ):
    i = pl.program_id(1)
    kk = jnp.concatenate([kp_ref[...], kc_ref[...]], axis=0)
    vv = jnp.concatenate([vp_ref[...], vc_ref[...]], axis=0)
    qt = qt_ref[...]
    r = lax.broadcasted_iota(i32, (2 * TQ, 2 * TQ), 0)
    ql = lax.broadcasted_iota(i32, (2 * TQ, 2 * TQ), 1) % TQ
    dist = TQ + ql - r
    key_pos = (i - 1) * TQ + r
    mask = (dist >= 0) & (dist < SWA_WINDOW) & (key_pos >= 0)
    distf = dist.astype(f32)
    zero = jnp.zeros((HEAD_DIM, 2 * TQ), bf16)
    heads = []
    for g in range(SWA_KV):
        q2 = jnp.concatenate([qt[(2 * g) * 64:(2 * g + 1) * 64], qt[(2 * g + 1) * 64:(2 * g + 2) * 64]], axis=1)
        qpad = jnp.concatenate([q2, zero], axis=0) if g == 0 else jnp.concatenate([zero, q2], axis=0)
        s = jnp.dot(kk, qpad, preferred_element_type=f32)
        s = jnp.where(mask, s - slope_ref[g:g + 1, :] * distf, NEG)
        sink = sink_ref[g:g + 1, :]
        m = jnp.maximum(jnp.max(s, axis=0, keepdims=True), sink)
        p = jnp.where(mask, jnp.exp(s - m), 0.0)
        den = jnp.sum(p, axis=0, keepdims=True) + jnp.exp(sink - m)
        p = (p / den).astype(bf16)
        o = lax.dot_general(vv, p, (((0,), (0,)), ((), ())), preferred_element_type=f32)
        og = o[64 * g:64 * g + 64]
        heads += [og[:, 0:TQ], og[:, TQ:2 * TQ]]
    o_ref[...] = jnp.concatenate(heads, axis=0).T.astype(bf16)


def swa_attention(kvc, qtc, slope_l, sink_l, *, B, S, TQ=Q_BLOCK):
    NB = S // TQ
    prev = lambda c: (lambda b, i: (b, jnp.maximum(i - 1, 0), c))
    cur = lambda c: (lambda b, i: (b, i, c))
    return pl.pallas_call(
        functools.partial(_swa_kernel, TQ=TQ),
        grid=(B, NB),
        in_specs=[
            pl.BlockSpec((None, TQ, LANES), prev(0)), pl.BlockSpec((None, TQ, LANES), cur(0)),
            pl.BlockSpec((None, TQ, LANES), prev(1)), pl.BlockSpec((None, TQ, LANES), cur(1)),
            pl.BlockSpec((None, C_WIDTH, TQ), lambda b, i: (b, 0, i)),
            pl.BlockSpec((SWA_KV, 2 * TQ), lambda b, i: (0, 0)),
            pl.BlockSpec((SWA_KV, 2 * TQ), lambda b, i: (0, 0)),
        ],
        out_specs=pl.BlockSpec((None, TQ, C_WIDTH), lambda b, i: (b, i, 0)),
        out_shape=jax.ShapeDtypeStruct((B, S, C_WIDTH), bf16),
        compiler_params=_cparams(("arbitrary", "arbitrary")),
        name="swa_attention",
    )(kvc, kvc, kvc, kvc, qtc, slope_l, sink_l)


def _nsa_compress_kernel(ch_ref, w1_ref, pe_ref, w2_ref, gain_ref, o_ref, *, NCP):
    w1 = w1_ref[...]
    p = jnp.dot(ch_ref[...], w1, preferred_element_type=f32)
    pb = jnp.dot(pe_ref[...], w1, preferred_element_type=f32)
    bias = pb[0:1, 0:512] + pb[1:2, 512:1024]
    hid = p[:, 0:512] + pltpu.roll(p[:, 512:1024], NCP - 1, axis=0) + bias
    y = jnp.dot(jax.nn.gelu(hid).astype(bf16), w2_ref[...], preferred_element_type=f32)
    is_k = lax.broadcasted_iota(i32, y.shape, 1) < HEAD_DIM
    ss = jnp.sum(jnp.where(is_k, y * y, 0.0), axis=-1, keepdims=True)
    yn = y * lax.rsqrt(ss * (1.0 / HEAD_DIM) + EPS) * gain_ref[...]
    o_ref[...] = jnp.where(is_k, yn, y).astype(bf16)


def nsa_compress(ch, w1big, pe2, w2big, gain, *, B, S):
    NCP = S // CMP_STRIDE
    W = CMP_STRIDE * LANES
    return pl.pallas_call(
        functools.partial(_nsa_compress_kernel, NCP=NCP),
        grid=(B,),
        in_specs=[
            pl.BlockSpec((None, NCP, W), lambda b: (b, 0, 0)),
            pl.BlockSpec((W, 4 * CMP_HID), lambda b: (0, 0)),
            pl.BlockSpec((SUBLANES, W), lambda b: (0, 0)),
            pl.BlockSpec((2 * CMP_HID, LANES), lambda b: (0, 0)),
            pl.BlockSpec((1, LANES), lambda b: (0, 0)),
        ],
        out_specs=pl.BlockSpec((None, NCP, LANES), lambda b: (b, 0, 0)),
        out_shape=jax.ShapeDtypeStruct((B, NCP, LANES), bf16),
        compiler_params=_cparams(("arbitrary",)),
        name="nsa_compress",
    )(ch, w1big, pe2, w2big, gain)


def _q_all_heads(qt, TQ, extra=None):
    qr = jnp.concatenate([qt[h * HEAD_DIM:(h + 1) * HEAD_DIM] for h in range(NSA_HEADS)], axis=1)
    if extra is None:
        return jnp.concatenate([qr, jnp.zeros_like(qr)], axis=0)
    pad = jnp.zeros((HEAD_DIM - extra.shape[0], qr.shape[1]), qr.dtype)
    return jnp.concatenate([qr, extra.astype(qr.dtype), pad], axis=0)


POS_COLS = 16
NSA_CMP_CHUNK = 64


def _pos_operand(pos, extra=None):
    n = pos.shape[0]
    col = lax.broadcasted_iota(i32, (n, LANES), 1) - HEAD_DIM
    a = jnp.right_shift(pos, 7)
    b = jnp.bitwise_and(pos, 127)
    val = jnp.where((col >= 0) & (col < 6), jnp.where(col % 2 == 0, a, b), 0)
    if extra is not None:
        val = jnp.where(col == extra, 1, val)
    return val.astype(f32).astype(bf16)


def _nsa_select_kernel(qt_ref, kv_ref, cend_ref, posr_ref, ocmp_ref, sel_ref, flag_ref,
                       s_scr, ps_scr, o_scr, *, TQ, S, CH):
    NSEL = S // SEL_BLOCK
    R = SEL_BLOCK // CMP_STRIDE
    KSEL = min(SEL_TOPK, NSEL)
    HW = NSA_HEADS * TQ
    i = pl.program_id(1)
    posr = posr_ref[...]
    qpad = _q_all_heads(qt_ref[...], TQ, extra=jnp.concatenate([posr, jnp.zeros_like(posr)], axis=0))
    t = i * TQ + lax.broadcasted_iota(i32, (1, HW), 1) % TQ
    tn = (((0,), (0,)), ((), ()))
    nq = jnp.minimum((2 * i + 1) // CH + 1, NSEL // CH)
    ps_scr[...] = jnp.zeros_like(ps_scr)
    o_scr[...] = jnp.zeros_like(o_scr)

    def rows_of(r, q):
        return pl.ds(pl.multiple_of(r * NSEL + q * CH, CH), CH)

    def scores(q, m):
        for r in range(R):
            rows = rows_of(r, q)
            kv = kv_ref[rows, :]
            dist = jnp.maximum(i * TQ + (TQ - 1) - cend_ref[rows, :], 0)
            is_k = lax.broadcasted_iota(i32, kv.shape, 1) < HEAD_DIM
            s = jnp.dot(jnp.where(is_k, kv, _pos_operand(dist)), qpad, preferred_element_type=f32)
            s = jnp.where(cend_ref[rows, :] <= t, s, NEG)
            s_scr[r, pl.ds(pl.multiple_of(q * CH, CH), CH), :] = s
            m = jnp.maximum(m, jnp.max(s, axis=0, keepdims=True))
        return m

    m = lax.fori_loop(0, nq, scores, jnp.full((1, HW), NEG, f32))
    m = jnp.maximum(m, 0.1 * NEG)

    def probs(q, l):
        for r in range(R):
            sl = pl.ds(pl.multiple_of(q * CH, CH), CH)
            p = jnp.exp2(s_scr[r, sl, :] - m)
            s_scr[r, sl, :] = p
            l = l + jnp.sum(p, axis=0, keepdims=True)
            o_scr[...] += lax.dot_general(kv_ref[rows_of(r, q), :], p.astype(bf16), tn, preferred_element_type=f32)
        return l

    l = lax.fori_loop(0, nq, probs, jnp.zeros((1, HW), f32))
    inv = 1.0 / jnp.maximum(l, 1e-30)
    ocmp_ref[...] = (o_scr[HEAD_DIM:2 * HEAD_DIM, :] * inv).astype(ocmp_ref.dtype)

    def head_sums(q, carry):
        for r in range(R):
            sl = pl.ds(pl.multiple_of(q * CH, CH), CH)
            pn = s_scr[r, sl, :] * inv
            acc = pn[:, 0:TQ]
            for h in range(1, NSA_HEADS):
                acc = acc + pn[:, h * TQ:(h + 1) * TQ]
            ps_scr[r, sl, :] = acc
        return carry

    lax.fori_loop(0, nq, head_sums, 0)
    jrow = lax.broadcasted_iota(i32, (NSEL, TQ), 0)
    ps3 = ps_scr[3]
    prev3 = jnp.where(jrow == 0, 0.0, pltpu.roll(ps3, 1, axis=0))
    imp = 0.5 * prev3 + ps_scr[0] + ps_scr[1] + ps_scr[2] + 0.5 * ps3
    tq = i * TQ + lax.broadcasted_iota(i32, (1, TQ), 1)
    cur = tq // SEL_BLOCK
    forced = (jrow == 0) | (jrow == cur) | (jrow == cur - 1)
    causal = jrow <= cur
    score = jnp.where(causal, jnp.where(forced, BIG, imp), -1e38)
    jf = jrow.astype(f32)
    taken = jnp.zeros((NSEL, TQ), jnp.bool_)
    picks = []
    for _ in range(KSEL):
        mx = jnp.max(score, axis=0, keepdims=True)
        idx = jnp.min(jnp.where(score == mx, jf, float(NSEL)), axis=0, keepdims=True)
        hit = jf == idx
        taken = taken | hit
        score = jnp.where(hit, -jnp.inf, score)
        picks.append(idx)
    picks += [picks[-1]] * (SUBLANES - KSEL)
    sel_ref[...] = jnp.concatenate(picks, axis=0).astype(i32)
    used = jnp.where(taken & causal, 1.0, 0.0).astype(bf16)
    flag_ref[...] = lax.dot_general(jnp.ones((SUBLANES, TQ), bf16), used, (((1,), (1,)), ((), ())),
                                    preferred_element_type=f32)


def nsa_select(qta, kvcmp, cend, posr, *, B, S, TQ=Q_BLOCK):
    NB = S // TQ
    NCP = S // CMP_STRIDE
    NSEL = S // SEL_BLOCK
    R = SEL_BLOCK // CMP_STRIDE
    HW = NSA_HEADS * TQ
    CH = min(NSA_CMP_CHUNK, NSEL)
    return pl.pallas_call(
        functools.partial(_nsa_select_kernel, TQ=TQ, S=S, CH=CH),
        grid=(B, NB),
        in_specs=[
            pl.BlockSpec((None, A_WIDTH, TQ), lambda b, i: (b, 0, i)),
            pl.BlockSpec((None, NCP, LANES), lambda b, i: (b, 0, 0)),
            pl.BlockSpec((NCP, 1), lambda b, i: (0, 0)),
            pl.BlockSpec((SUBLANES, HW), lambda b, i: (0, 0)),
        ],
        out_specs=(
            pl.BlockSpec((None, None, HEAD_DIM, HW), lambda b, i: (b, i, 0, 0)),
            pl.BlockSpec((None, SUBLANES, TQ), lambda b, i: (b, 0, i)),
            pl.BlockSpec((None, None, SUBLANES, NSEL), lambda b, i: (b, i, 0, 0)),
        ),
        out_shape=(
            jax.ShapeDtypeStruct((B, NB, HEAD_DIM, HW), bf16),
            jax.ShapeDtypeStruct((B, SUBLANES, S), i32),
            jax.ShapeDtypeStruct((B, NB, SUBLANES, NSEL), f32),
        ),
        scratch_shapes=[pltpu.VMEM((R, NSEL, HW), f32), pltpu.VMEM((R, NSEL, TQ), f32), pltpu.VMEM((LANES, HW), f32)],
        compiler_params=_cparams(("arbitrary", "arbitrary")),
        name="nsa_select",
    )(qta, kvcmp, cend, posr)


_SLC_CHUNK = 4


def _nsa_attend_kernel(cnt_ref, lst_ref, qt_ref, kslc_ref, kwin_ref, sel_ref, ocmp_ref, gt_ref, posr_ref,
                       winb_ref, diagb_ref, o_ref, acc_scr, m_scr, l_scr, qa_scr, s_scr, p_scr, al_scr, *, TQ, S):
    NB = S // TQ
    NSEL = S // SEL_BLOCK
    HW = NSA_HEADS * TQ
    b = pl.program_id(0)
    i = pl.program_id(1)
    count = cnt_ref[b * NB + i]
    qpad = _q_all_heads(qt_ref[...], TQ)
    qa_scr[...] = qpad
    posr = posr_ref[...]
    sel = sel_ref[...]
    tn = (((0,), (0,)), ((), ()))

    NWB = NSA_WINDOW // TQ + 1
    blks, biases = [], []
    for w in range(NWB):
        jb = i - (NWB - 1) + w
        blks.append(kwin_ref[pl.ds(pl.multiple_of(jnp.maximum(jb, 0) * TQ, TQ), TQ), :])
        biases.append(winb_ref[pl.ds(pl.multiple_of(jnp.where(jb >= 0, w, NWB) * TQ, TQ), TQ), :])
    kw = jnp.concatenate(blks, axis=0)
    s = jnp.dot(kw, qpad, preferred_element_type=f32)
    parts = [s[w * TQ:(w + 1) * TQ] + biases[w] for w in range(NWB)]
    m = parts[0].max(axis=0, keepdims=True)
    for part in parts[1:]:
        m = jnp.maximum(m, part.max(axis=0, keepdims=True))
    ps = [jnp.exp2(part - m) for part in parts]
    l = ps[0].sum(axis=0, keepdims=True)
    for part in ps[1:]:
        l = l + part.sum(axis=0, keepdims=True)
    pw = jnp.concatenate(ps, axis=0).astype(bf16)
    o_win = lax.dot_general(kw, pw, tn, preferred_element_type=f32)[HEAD_DIM:2 * HEAD_DIM] * (1.0 / l)

    kd = kslc_ref[pl.ds(pl.multiple_of(i * TQ, TQ), TQ), :]
    s = jnp.dot(kd, qpad, preferred_element_type=f32) + diagb_ref[...]
    m = jnp.max(s, axis=0, keepdims=True)
    p = jnp.exp2(s - m)
    m_scr[...] = m
    l_scr[...] = jnp.sum(p, axis=0, keepdims=True)
    acc_scr[...] = lax.dot_general(kd, p.astype(bf16), tn, preferred_element_type=f32)[HEAD_DIM:2 * HEAD_DIM]

    def gather(c):
        blks, js, oks = [], [], []
        for b4 in range(_SLC_CHUNK):
            idx = _SLC_CHUNK * c + b4
            ok = (idx >= 0) & (idx < count)
            j = jnp.where(ok, lst_ref[0, 0, jnp.clip(idx, 0, NSEL - 1)], 0)
            blks.append(kslc_ref[pl.ds(pl.multiple_of(j * SEL_BLOCK, SEL_BLOCK), SEL_BLOCK), :])
            js.append(j)
            oks.append(ok)
        return jnp.concatenate(blks, axis=0), js, oks

    def scores(c, par):
        kc, js, oks = gather(c)
        pos, pens = [], []
        for j, ok in zip(js, oks):
            pos.append(i * TQ + (TQ - 1) - j * SEL_BLOCK - lax.broadcasted_iota(i32, (SEL_BLOCK, 1), 0))
            picked = jnp.max(jnp.where(sel == j, 1.0, 0.0), axis=0, keepdims=True) > 0.0
            pen = jnp.where(picked & ok, 0.0, NEG)
            pens.append(jnp.concatenate([pen] * NSA_HEADS, axis=1))
        blk_id = lax.broadcasted_iota(i32, (_SLC_CHUNK * SEL_BLOCK, 1), 0) // SEL_BLOCK
        pm = _pos_operand(jnp.concatenate(pos, axis=0), extra=SUBLANES + blk_id)
        is_k = lax.broadcasted_iota(i32, kc.shape, 1) < HEAD_DIM
        pens += [jnp.zeros((1, HW), f32)] * (SUBLANES - _SLC_CHUNK)
        qa_scr[HEAD_DIM:HEAD_DIM + POS_COLS, :] = jnp.concatenate([posr, jnp.concatenate(pens, axis=0)],
                                                                  axis=0).astype(bf16)
        s_scr[par] = jnp.dot(jnp.where(is_k, kc, pm), qa_scr[...], preferred_element_type=f32)

    def weights(par):
        s = s_scr[par]
        m_old = m_scr[...]
        m_new = jnp.maximum(m_old, jnp.max(s, axis=0, keepdims=True))
        alpha = jnp.exp2(m_old - m_new)
        p = jnp.exp2(s - m_new)
        l_scr[...] = l_scr[...] * alpha + jnp.sum(p, axis=0, keepdims=True)
        p_scr[par] = p.astype(bf16)
        al_scr[par] = alpha
        m_scr[...] = m_new

    def values(c, par):
        kc, _, _ = gather(c)
        pv = lax.dot_general(kc, p_scr[par], tn, preferred_element_type=f32)
        acc_scr[...] = acc_scr[...] * al_scr[par] + pv[HEAD_DIM:2 * HEAD_DIM]

    nch = (count + _SLC_CHUNK - 1) // _SLC_CHUNK
    p_scr[1] = jnp.zeros_like(p_scr[1])
    al_scr[1] = jnp.ones_like(al_scr[1])
    scores(0, 0)

    def body(t, carry):
        c = 2 * t
        values(c - 1, 1)
        scores(c + 1, 1)
        weights(0)
        values(c, 0)
        scores(c + 2, 0)
        weights(1)
        return carry

    npairs = (nch + 1) // 2
    lax.fori_loop(0, npairs, body, 0)
    values(2 * npairs - 1, 1)
    o_slc = acc_scr[...] / l_scr[...]

    gt = gt_ref[...]
    gate = [jnp.concatenate([gt[br * NSA_HEADS + h:br * NSA_HEADS + h + 1] for h in range(NSA_HEADS)], axis=1)
            for br in range(3)]
    o = gate[0] * ocmp_ref[...].astype(f32) + gate[1] * o_slc + gate[2] * o_win
    o = jnp.concatenate([o[:, h * TQ:(h + 1) * TQ] for h in range(NSA_HEADS)], axis=0)
    o_ref[...] = o.T.astype(bf16)


def nsa_attend(counts, lists, qta, kva, selT, ocmp, gT, posr, winb, diagb, *, B, S, TQ=Q_BLOCK):
    NB = S // TQ
    NSEL = S // SEL_BLOCK
    HW = NSA_HEADS * TQ
    NWB = NSA_WINDOW // TQ + 1
    const = lambda b, i, cnt: (0, 0)
    grid_spec = pltpu.PrefetchScalarGridSpec(
        num_scalar_prefetch=1,
        grid=(B, NB),
        in_specs=[
            pl.BlockSpec((1, 1, NSEL), lambda b, i, cnt: (b * NB + i, 0, 0), memory_space=pltpu.SMEM),
            pl.BlockSpec((None, A_WIDTH, TQ), lambda b, i, cnt: (b, 0, i)),
            pl.BlockSpec((None, S, LANES), lambda b, i, cnt: (b, 0, 1)),
            pl.BlockSpec((None, S, LANES), lambda b, i, cnt: (b, 0, 2)),
            pl.BlockSpec((None, SUBLANES, TQ), lambda b, i, cnt: (b, 0, i)),
            pl.BlockSpec((None, None, HEAD_DIM, HW), lambda b, i, cnt: (b, i, 0, 0)),
            pl.BlockSpec((None, 32, TQ), lambda b, i, cnt: (b, 0, i)),
            pl.BlockSpec((SUBLANES, HW), const),
            pl.BlockSpec(((NWB + 1) * TQ, HW), const),
            pl.BlockSpec((TQ, HW), const),
        ],
        out_specs=pl.BlockSpec((None, TQ, A_WIDTH), lambda b, i, cnt: (b, i, 0)),
        scratch_shapes=[pltpu.VMEM((HEAD_DIM, HW), f32), pltpu.VMEM((1, HW), f32), pltpu.VMEM((1, HW), f32),
                        pltpu.VMEM((LANES, HW), bf16),
                        pltpu.VMEM((2, _SLC_CHUNK * SEL_BLOCK, HW), f32),
                        pltpu.VMEM((2, _SLC_CHUNK * SEL_BLOCK, HW), bf16),
                        pltpu.VMEM((2, 1, HW), f32)],
    )
    return pl.pallas_call(
        functools.partial(_nsa_attend_kernel, TQ=TQ, S=S),
        grid_spec=grid_spec,
        out_shape=jax.ShapeDtypeStruct((B, S, A_WIDTH), bf16),
        compiler_params=_cparams(("arbitrary", "arbitrary")),
        name="nsa_attend",
    )(counts, lists, qta, kva, kva, selT, ocmp, gT, posr, winb, diagb)


def nsa_constants(slopes, S, TQ=Q_BLOCK):
    NSEL = S // SEL_BLOCK
    NCP = S // CMP_STRIDE
    R = SEL_BLOCK // CMP_STRIDE
    NWB = NSA_WINDOW // TQ + 1
    s2 = jnp.repeat(slopes * np.float32(np.log2(np.e)), TQ)
    hi = s2.astype(bf16).astype(f32)
    mid = (s2 - hi).astype(bf16).astype(f32)
    lo = (s2 - hi - mid).astype(bf16).astype(f32)
    zero = jnp.zeros_like(s2)
    posr = -jnp.stack([128.0 * hi, hi, 128.0 * mid, mid, 128.0 * lo, lo, zero, zero], axis=0)
    rho = np.arange(NCP)
    c = R * (rho % NSEL) + rho // NSEL
    cend = CMP_STRIDE * c + (CMP_LEN - 1)
    cend_mask = np.where(c < NCP - 1, cend, np.iinfo(np.int32).max)
    tl = jnp.tile(jnp.arange(TQ, dtype=i32), NSA_HEADS)[None, :]
    wpos = jnp.arange((NWB + 1) * TQ, dtype=i32)[:, None]
    dist = (NWB - 1) * TQ + tl - wpos
    winb = jnp.where((dist >= 0) & (dist < NSA_WINDOW) & (wpos < NWB * TQ),
                     -s2[None, :] * (NWB * TQ - 1 - wpos).astype(f32), NEG)
    dpos = jnp.arange(TQ, dtype=i32)[:, None]
    diagb = jnp.where(dpos <= tl, -s2[None, :] * (TQ - 1 - dpos).astype(f32), NEG)
    return (posr, jnp.asarray(cend_mask[:, None], i32), winb, diagb)


def nsa_branch(qta, kva, gT, cmp_w, nsa_c, *, B, S):
    NB = S // Q_BLOCK
    NSEL = S // SEL_BLOCK
    NCP = S // CMP_STRIDE
    R = SEL_BLOCK // CMP_STRIDE
    posr, cend, winb, diagb = nsa_c
    kva = kva.reshape(B, S, 6 * HEAD_DIM)
    ch = kva[:, :, 0:LANES].reshape(B, NCP, CMP_STRIDE * LANES)
    kvcmp = nsa_compress(ch, *cmp_w, B=B, S=S)
    kvcmp = kvcmp.reshape(B, NSEL, R, LANES).swapaxes(1, 2).reshape(B, NCP, LANES)
    ocmp, selT, flags = nsa_select(qta, kvcmp, cend, posr, B=B, S=S)
    ar = jnp.arange(NSEL, dtype=i32)
    own = (Q_BLOCK // SEL_BLOCK) * jnp.arange(NB, dtype=i32)
    used = (flags[:, :, 0, :] > 0.5) & (ar[None, None, :] < own[None, :, None])
    lists = jnp.minimum(jnp.sort(jnp.where(used, ar, ar + NSEL), axis=-1), NSEL - 1)
    counts = jnp.sum(used, axis=-1).astype(i32)
    return nsa_attend(counts.reshape(B * NB), lists.reshape(B * NB, 1, NSEL), qta, kva, selT, ocmp, gT,
                      posr, winb, diagb, B=B, S=S)


def _route(logit, rb):
    sc = jax.nn.sigmoid(logit)
    sel = sc + rb
    srow = [sel[e:e + 1] for e in range(N_EXPERTS)]
    crow = [sc[e:e + 1] for e in range(N_EXPERTS)]
    gscore = []
    for g in range(N_GROUPS):
        a, b, c, d = srow[4 * g:4 * g + 4]
        top2 = jnp.maximum(jnp.maximum(jnp.maximum(a + b, a + c), jnp.maximum(a + d, b + c)),
                           jnp.maximum(b + d, c + d))
        gscore.append(top2)
    best, gi = gscore[0], jnp.zeros_like(gscore[0], dtype=i32)
    for g in range(1, N_GROUPS):
        better = gscore[g] > best
        gi = jnp.where(better, g, gi)
        best = jnp.where(better, gscore[g], best)

    def pick_group(rows, k):
        v = rows[k]
        for g in range(1, N_GROUPS):
            v = jnp.where(gi == g, rows[4 * g + k], v)
        return v

    iv = [pick_group(srow, k) for k in range(EXPERTS_PER_GROUP)]
    ic = [pick_group(crow, k) for k in range(EXPERTS_PER_GROUP)]
    b1, i1, w1 = iv[0], jnp.zeros_like(gi), ic[0]
    for k in range(1, EXPERTS_PER_GROUP):
        better = iv[k] > b1
        i1 = jnp.where(better, k, i1)
        w1 = jnp.where(better, ic[k], w1)
        b1 = jnp.where(better, iv[k], b1)
    b2 = jnp.full_like(b1, -jnp.inf)
    i2, w2 = jnp.zeros_like(gi), jnp.zeros_like(w1)
    for k in range(EXPERTS_PER_GROUP):
        better = (i1 != k) & (iv[k] > b2)
        i2 = jnp.where(better, k, i2)
        w2 = jnp.where(better, ic[k], w2)
        b2 = jnp.where(better, iv[k], b2)
    tot = w1 + w2
    eidx = jnp.concatenate([gi * EXPERTS_PER_GROUP + i1, gi * EXPERTS_PER_GROUP + i2], axis=0)
    ew = jnp.concatenate([w1 / tot, w2 / tot], axis=0)
    return eidx, ew


def _mix_out_kernel(x_ref, a_ref, sh_ref, ga_ref, oa_ref, ob_ref, oc_ref, wm_ref, wa_ref, wb_ref, wc_ref, wo_ref,
                    am_ref, shm_ref, wr_ref, rb_ref, xo_ref, h2_ref, eidx_ref, ew_ref):
    x = x_ref[...]
    h = _norm_mod(x, a_ref[...], sh_ref[...]).astype(bf16)
    gates = jax.nn.sigmoid(jnp.dot(h, wm_ref[...], preferred_element_type=f32))
    D = D_MODEL
    y = gates[:, 0:D] * jnp.dot(oa_ref[...], wa_ref[...], preferred_element_type=f32)
    y = y + gates[:, D:2 * D] * jnp.dot(ob_ref[...], wb_ref[...], preferred_element_type=f32)
    y = y + gates[:, 2 * D:3 * D] * jnp.dot(oc_ref[...], wc_ref[...], preferred_element_type=f32)
    xn = x + ga_ref[...] * jnp.dot(y.astype(bf16), wo_ref[...], preferred_element_type=f32)
    xo_ref[...] = xn
    h2 = _norm_mod(xn, am_ref[...], shm_ref[...])
    hi = h2.astype(bf16)
    h2_ref[...] = hi
    lo = (h2 - hi.astype(f32)).astype(bf16)
    wr = wr_ref[...]
    whi = wr.astype(bf16)
    wlo = (wr - whi.astype(f32)).astype(bf16)
    nt = lambda p, q: lax.dot_general(p, q, (((1,), (1,)), ((), ())), preferred_element_type=f32)
    logit = nt(whi, hi) + nt(whi, lo) + nt(wlo, hi)
    eidx, ew = _route(logit, rb_ref[...])
    eidx_ref[...] = eidx
    ew_ref[...] = ew


def mix_out(x2, a_mod, sh_mod, g_a, oa, ob, oc, wm, wa, wb, wc, wo, am_moe, shm_moe, wrT, rb, *, B, S, tm=512):
    N, D = x2.shape
    tpb = S // tm
    row = lambda i: (i, 0)
    bmap = lambda i: (i // tpb, 0, 0)
    full = lambda i: (0, 0)
    col = lambda i: (0, i)
    return pl.pallas_call(
        _mix_out_kernel,
        grid=(N // tm,),
        in_specs=[
            pl.BlockSpec((tm, D), row),
            pl.BlockSpec((None, 1, D), bmap), pl.BlockSpec((None, 1, D), bmap), pl.BlockSpec((None, 1, D), bmap),
            pl.BlockSpec((tm, A_WIDTH), row), pl.BlockSpec((tm, B_WIDTH), row), pl.BlockSpec((tm, C_WIDTH), row),
            pl.BlockSpec((D, 3 * D), full), pl.BlockSpec((A_WIDTH, D), full), pl.BlockSpec((B_WIDTH, D), full),
            pl.BlockSpec((C_WIDTH, D), full), pl.BlockSpec((D, D), full),
            pl.BlockSpec((None, 1, D), bmap), pl.BlockSpec((None, 1, D), bmap),
            pl.BlockSpec((N_EXPERTS, D), full), pl.BlockSpec((N_EXPERTS, 1), full),
        ],
        out_specs=(pl.BlockSpec((tm, D), row), pl.BlockSpec((tm, D), row),
                   pl.BlockSpec((TOP_K, tm), col), pl.BlockSpec((TOP_K, tm), col)),
        out_shape=(jax.ShapeDtypeStruct((N, D), f32), jax.ShapeDtypeStruct((N, D), bf16),
                   jax.ShapeDtypeStruct((TOP_K, N), i32), jax.ShapeDtypeStruct((TOP_K, N), f32)),
        compiler_params=_cparams(("arbitrary",)),
        name="mix_out",
    )(x2, a_mod, sh_mod, g_a, oa, ob, oc, wm, wa, wb, wc, wo, am_moe, shm_moe, wrT, rb)


def _moe_ffn_kernel(be_ref, nu_ref, xs_ref, wg_ref, wu_ref, wd_ref, y_ref, wg_s, wu_s, wd_s):
    i = pl.program_id(0)

    @pl.when((i == 0) | (be_ref[i] != be_ref[jnp.maximum(i - 1, 0)]))
    def _():
        wg_s[...] = wg_ref[...].astype(bf16)
        wu_s[...] = wu_ref[...].astype(bf16)
        wd_s[...] = wd_ref[...].astype(bf16)

    @pl.when(i < nu_ref[0])
    def _():
        xs = xs_ref[...]
        g = jnp.dot(xs, wg_s[...], preferred_element_type=f32)
        u = jnp.dot(xs, wu_s[...], preferred_element_type=f32)
        a = (jax.nn.silu(g) * u).astype(bf16)
        y_ref[...] = jnp.dot(a, wd_s[...], preferred_element_type=f32).astype(y_ref.dtype)

    @pl.when(i >= nu_ref[0])
    def _():
        y_ref[...] = jnp.zeros_like(y_ref)


def moe_ffn(blk_e, n_used, xs, wg, wu, wd, *, layer):
    n_rows, D = xs.shape
    n_blk = n_rows // MOE_BLOCK
    wmap = lambda i, be, nu: (layer, be[i], 0, 0)
    grid_spec = pltpu.PrefetchScalarGridSpec(
        num_scalar_prefetch=2,
        grid=(n_blk,),
        in_specs=[
            pl.BlockSpec((MOE_BLOCK, D), lambda i, be, nu: (i, 0)),
            pl.BlockSpec((None, None, D, D_EXPERT), wmap),
            pl.BlockSpec((None, None, D, D_EXPERT), wmap),
            pl.BlockSpec((None, None, D_EXPERT, D), wmap),
        ],
        out_specs=pl.BlockSpec((MOE_BLOCK, D), lambda i, be, nu: (i, 0)),
        scratch_shapes=[pltpu.VMEM((D, D_EXPERT), bf16), pltpu.VMEM((D, D_EXPERT), bf16),
                        pltpu.VMEM((D_EXPERT, D), bf16)],
    )
    return pl.pallas_call(
        _moe_ffn_kernel,
        grid_spec=grid_spec,
        out_shape=jax.ShapeDtypeStruct((n_rows, D), bf16),
        compiler_params=_cparams(("arbitrary",)),
        name="moe_ffn",
    )(blk_e, n_used, xs, wg, wu, wd)


def moe_dispatch(eidx, N):
    NK = N * TOP_K
    n_blk = -(-NK // MOE_BLOCK) + N_EXPERTS
    n_slot = n_blk * MOE_BLOCK
    experts = jnp.arange(N_EXPERTS, dtype=i32)
    flat_e = eidx.T.reshape(-1)
    counts = jnp.sum(flat_e[:, None] == experts[None, :], axis=0, dtype=i32)
    padded = (counts + MOE_BLOCK - 1) // MOE_BLOCK * MOE_BLOCK
    p_end = jnp.cumsum(padded)
    blk_first = jnp.arange(n_blk, dtype=i32) * MOE_BLOCK
    blk_e = jnp.minimum(jnp.sum(p_end[None, :] <= blk_first[:, None], axis=1, dtype=i32), N_EXPERTS - 1)
    n_used = (p_end[-1] // MOE_BLOCK).astype(i32).reshape(1)
    stride = NK + MOE_BLOCK
    pair = jnp.arange(NK, dtype=i32)
    q = jnp.arange(MOE_BLOCK, dtype=i32)
    pad_key = jnp.where(q[None, :] < (padded - counts)[:, None], experts[:, None] * stride + NK + q[None, :],
                        jnp.iinfo(jnp.int32).max)
    keys = jnp.concatenate([flat_e * stride + pair, pad_key.reshape(-1)])
    pair_or_pad = jnp.concatenate([pair, jnp.full((n_slot - NK,), NK, i32)])
    _, slot_pair = lax.sort((keys, pair_or_pad), num_keys=1)
    slot_tok = jnp.minimum(slot_pair // TOP_K, N - 1)
    _, pos = lax.sort((slot_pair, jnp.arange(n_slot, dtype=i32)), num_keys=1)
    return slot_tok, blk_e, n_used, pos[:NK].reshape(N, TOP_K)


def _prep_cmp_weights(pe_k, w1_k, w2_k, pe_v, w1_v, w2_v, k_gain0):
    T = CMP_STRIDE
    w1big = jnp.zeros((T, 2, HEAD_DIM, 4, CMP_HID), f32)
    w1k = w1_k.reshape(2, T, HEAD_DIM, CMP_HID)
    w1v = w1_v.reshape(2, T, HEAD_DIM, CMP_HID)
    w1big = w1big.at[:, 0, :, 0].set(w1k[0]).at[:, 1, :, 1].set(w1v[0])
    w1big = w1big.at[:, 0, :, 2].set(w1k[1]).at[:, 1, :, 3].set(w1v[1])
    w1big = w1big.reshape(T * LANES, 4 * CMP_HID).astype(bf16)
    pe = jnp.stack([pe_k.reshape(2, T, HEAD_DIM), pe_v.reshape(2, T, HEAD_DIM)], axis=2)
    pe2 = jnp.zeros((SUBLANES, T * LANES), f32).at[0:2].set(pe.reshape(2, T * LANES)).astype(bf16)
    w2big = jnp.zeros((2 * CMP_HID, LANES), f32)
    w2big = w2big.at[0:CMP_HID, 0:HEAD_DIM].set(w2_k).at[CMP_HID:, HEAD_DIM:].set(w2_v).astype(bf16)
    gain = jnp.concatenate([k_gain0, jnp.ones((HEAD_DIM,), f32)])[None, :]
    return w1big, pe2, w2big, gain


def _prep_proj_weights(w_in, nsa_q_gain, nsa_k_gain, swa_q_gain, swa_k_gain):
    w = w_in
    wtok = jnp.concatenate([w[:, _O_KVA:_O_GA], w[:, _O_KB:_O_VB], w[:, _O_KC:_O_MG]], axis=1).astype(bf16)
    gate_perm = np.array([h * 3 + br for br in range(3) for h in range(NSA_HEADS)])
    scale = HEAD_DIM ** -0.5
    wch = jnp.concatenate([w[:, _O_QA:_O_KVA], w[:, _O_QB:_O_KB] * (0.5 * scale), w[:, _O_QC:_O_KC],
                           w[:, _O_VB:_O_QC], w[:, _O_GA:_O_QB][:, gate_perm], jnp.zeros((D_MODEL, 8), f32)],
                          axis=1).T.astype(bf16)
    gtok = jnp.zeros((1, TOK_W), f32)
    gtok = gtok.at[0, 128:192].set(nsa_k_gain[1]).at[0, 256:320].set(nsa_k_gain[2])
    gtok = gtok.at[0, 640:704].set(swa_k_gain).at[0, 704:768].set(swa_k_gain)
    mtok = jnp.zeros((1, TOK_W), f32).at[0, 128:192].set(1.0).at[0, 256:320].set(1.0).at[0, 640:768].set(1.0)
    gcha = (jnp.tile(nsa_q_gain, NSA_HEADS) * (scale * np.float32(np.log2(np.e))))[:, None]
    gchc = (jnp.tile(swa_q_gain, SWA_HEADS) * scale)[:, None]
    blk = np.arange(LANES) // HEAD_DIM
    bd = jnp.asarray(blk[:, None] == blk[None, :], bf16)
    return wtok, wch, gtok, mtok, gcha, gchc, bd


def _adaln_kernel(c_ref, w_ref, b_ref, o_ref):
    cond = jax.nn.silu(c_ref[...]).astype(bf16)
    o_ref[...] = jnp.dot(cond, w_ref[...].astype(bf16), preferred_element_type=f32) + b_ref[...]


def adaln(c_pad, w_ada, b_ada):
    L, D, D6 = w_ada.shape
    return pl.pallas_call(
        _adaln_kernel,
        grid=(L, D6 // D),
        in_specs=[
            pl.BlockSpec((SUBLANES, D), lambda l, j: (0, 0)),
            pl.BlockSpec((None, D, D), lambda l, j: (l, 0, j)),
            pl.BlockSpec((None, 1, D), lambda l, j: (l, 0, j)),
        ],
        out_specs=pl.BlockSpec((None, SUBLANES, D), lambda l, j: (l, 0, j)),
        out_shape=jax.ShapeDtypeStruct((L, SUBLANES, D6), f32),
        compiler_params=_cparams(("arbitrary", "arbitrary")),
        name="adaln",
    )(c_pad, w_ada, b_ada.reshape(L, 1, D6))


def _moe_combine_kernel(x_ref, y0_ref, y1_ref, w_ref, gm_ref, o_ref):
    w = w_ref[...]
    y = y0_ref[...].astype(f32) * w[:, 0:1] + y1_ref[...].astype(f32) * w[:, 1:2]
    o_ref[...] = x_ref[...] + gm_ref[...] * y


def moe_combine(x2, y0, y1, ew_tok, g_m, *, B, S, tm=512):
    N, D = x2.shape
    tpb = S // tm
    row = lambda i: (i, 0)
    return pl.pallas_call(
        _moe_combine_kernel,
        grid=(N // tm,),
        in_specs=[pl.BlockSpec((tm, D), row), pl.BlockSpec((tm, D), row), pl.BlockSpec((tm, D), row),
                  pl.BlockSpec((tm, TOP_K), row), pl.BlockSpec((None, 1, D), lambda i: (i // tpb, 0, 0))],
        out_specs=pl.BlockSpec((tm, D), row),
        out_shape=jax.ShapeDtypeStruct((N, D), f32),
        compiler_params=_cparams(("arbitrary",)),
        name="moe_combine",
    )(x2, y0, y1, ew_tok, g_m)


def kernel(x, c, w_ada, b_ada, g_norm_mix, g_norm_moe, w_in, cmp_pe_k, cmp_w1_k, cmp_w2_k, cmp_pe_v, cmp_w1_v,
           cmp_w2_v, nsa_q_gain, nsa_k_gain, swa_q_gain, swa_k_gain, swa_sinks, w_branch_a, w_branch_b,
           w_branch_c, w_out, w_router, router_bias, w_exp_gate, w_exp_up, w_exp_down):
    B, S, D = x.shape
    N = B * S
    L = w_ada.shape[0]
    slopes = 2.0 ** (-8.0 * jnp.arange(1, SWA_HEADS + NSA_HEADS + 1, dtype=f32) / (SWA_HEADS + NSA_HEADS))
    swa_slope_l = jnp.repeat(slopes[:SWA_HEADS].reshape(SWA_KV, SWA_HEADS // SWA_KV), Q_BLOCK, axis=1)
    nsa_c = nsa_constants(slopes[SWA_HEADS:], S)
    c_pad = jnp.zeros((SUBLANES, D), f32).at[:B].set(c)
    mod_all = adaln(c_pad, w_ada, b_ada)[:, :B]
    wrT = w_router.T
    rb = router_bias[:, None]
    x2 = x.reshape(N, D)
    for l in range(L):
        sh_a, sc_a, g_a, sh_m, sc_m, g_m = [m[:, None, :] for m in jnp.split(mod_all[l], 6, axis=-1)]
        a_mix = g_norm_mix[l][None, None, :] * (1.0 + sc_a)
        a_moe = g_norm_moe[l][None, None, :] * (1.0 + sc_m)
        pw = _prep_proj_weights(w_in[l], nsa_q_gain[l], nsa_k_gain[l], swa_q_gain[l], swa_k_gain[l])
        kva, kb, kvc, qta, qtb, qtc, vtb, gT = proj_in(x2, a_mix, sh_a, *pw, B=B, S=S)
        cw = _prep_cmp_weights(cmp_pe_k[l], cmp_w1_k[l], cmp_w2_k[l], cmp_pe_v[l], cmp_w1_v[l], cmp_w2_v[l],
                               nsa_k_gain[l][0])
        o_a = nsa_branch(qta, kva, gT, cw, nsa_c, B=B, S=S)
        o_b = sb_attention(_perm_rows(kb.reshape(B, S, B_WIDTH), S), _perm_lanes(vtb, S), qtb, B=B, S=S)
        sink_l = jnp.repeat(swa_sinks[l].reshape(SWA_KV, SWA_HEADS // SWA_KV), Q_BLOCK, axis=1)
        o_c = swa_attention(kvc.reshape(B, S, 2 * SWA_KV * HEAD_DIM), qtc, swa_slope_l, sink_l, B=B, S=S)
        x2, h2, eidx, ew = mix_out(
            x2, a_mix, sh_a, g_a, o_a.reshape(N, A_WIDTH), o_b.reshape(N, B_WIDTH), o_c.reshape(N, C_WIDTH),
            w_in[l][:, _O_MG:_O_END].astype(bf16), w_branch_a[l].astype(bf16), w_branch_b[l].astype(bf16),
            w_branch_c[l].astype(bf16), w_out[l].astype(bf16), a_moe, sh_m, wrT, rb, B=B, S=S)
        slot_tok, blk_e, n_used, pos = moe_dispatch(eidx, N)
        y = moe_ffn(blk_e, n_used, h2[slot_tok], w_exp_gate, w_exp_up, w_exp_down, layer=l)
        x2 = moe_combine(x2, y[pos[:, 0]], y[pos[:, 1]], ew.T, g_m, B=B, S=S)
    return x2.reshape(B, S, D)
```

```python
import functools

import numpy as np
import jax
import jax.numpy as jnp
from jax import lax
from jax.experimental import pallas as pl
from jax.experimental.pallas import tpu as pltpu

f32 = jnp.float32
bf16 = jnp.bfloat16
i32 = jnp.int32

D_MODEL = 1024
HEAD_DIM = 64
Q_BLOCK = 128
NSA_HEADS = 8
CMP_STRIDE = 16
CMP_LEN = 32
CMP_HID = 256
SEL_BLOCK = 64
SEL_TOPK = 8
NSA_WINDOW = 512
SB_HEADS = 4
SWA_HEADS = 4
SWA_KV = 2
SWA_WINDOW = 128
N_EXPERTS = 16
N_GROUPS = 4
EXPERTS_PER_GROUP = 4
TOP_K = 2
D_EXPERT = 512
MOE_BLOCK = 512
EPS = 1e-6
NEG = -1e30
BIG = 1e30

A_WIDTH = NSA_HEADS * HEAD_DIM
B_WIDTH = SB_HEADS * HEAD_DIM
C_WIDTH = SWA_HEADS * HEAD_DIM

LANES = 128
SUBLANES = 8
VMEM_LIMIT = 56 * 1024 * 1024

_O_QA = 0
_O_KVA = _O_QA + A_WIDTH
_O_GA = _O_KVA + 6 * HEAD_DIM
_O_QB = _O_GA + 3 * NSA_HEADS
_O_KB = _O_QB + B_WIDTH
_O_VB = _O_KB + B_WIDTH
_O_QC = _O_VB + B_WIDTH
_O_KC = _O_QC + C_WIDTH
_O_VC = _O_KC + SWA_KV * HEAD_DIM
_O_MG = _O_VC + SWA_KV * HEAD_DIM
_O_END = _O_MG + 3 * D_MODEL

TOK_W = 6 * HEAD_DIM + B_WIDTH + 2 * SWA_KV * HEAD_DIM
CH_W = A_WIDTH + B_WIDTH + C_WIDTH + B_WIDTH + 32
_TOK_NORM_CHUNKS = (1, 2, 5)


def _cparams(sem):
    return pltpu.CompilerParams(dimension_semantics=sem, vmem_limit_bytes=VMEM_LIMIT)


def _split_dot(x, w):
    hi = x.astype(bf16)
    lo = (x - hi.astype(f32)).astype(bf16)
    return jnp.dot(hi, w, preferred_element_type=f32) + jnp.dot(lo, w, preferred_element_type=f32)


def _norm_mod(x, a, sh):
    rs = lax.rsqrt(jnp.mean(x * x, axis=-1, keepdims=True) + EPS)
    return x * rs * a + sh


def _group_norm_ch(y, gain_col, n, tm):
    y3 = y.reshape(n, HEAD_DIM, tm)
    ss = jnp.sum(y3 * y3, axis=1, keepdims=True)
    yn = y3 * lax.rsqrt(ss * (1.0 / HEAD_DIM) + EPS)
    return yn.reshape(n * HEAD_DIM, tm) * gain_col


def _proj_in_kernel(x_ref, a_ref, sh_ref, wtok_ref, wch_ref, gtok_ref, mtok_ref, gcha_ref, gchc_ref, bd_ref,
                    kva_ref, kb_ref, kvc_ref, qta_ref, qtb_ref, qtc_ref, vtb_ref, gt_ref, *, tm):
    h = _norm_mod(x_ref[...], a_ref[...], sh_ref[...]).astype(bf16)
    t = jnp.dot(h, wtok_ref[...], preferred_element_type=f32)
    cols = [t[:, c * LANES:(c + 1) * LANES] for c in range(TOK_W // LANES)]
    for c in _TOK_NORM_CHUNKS:
        y = cols[c]
        ss = _split_dot(y * y, bd_ref[...])
        yn = y * lax.rsqrt(ss * (1.0 / HEAD_DIM) + EPS) * gtok_ref[:, c * LANES:(c + 1) * LANES]
        cols[c] = jnp.where(mtok_ref[:, c * LANES:(c + 1) * LANES] > 0.0, yn, y)
    t = jnp.concatenate(cols, axis=1).astype(bf16)
    kva_ref[...] = t[:, 0:384]
    kb_ref[...] = t[:, 384:640]
    kvc_ref[...] = t[:, 640:896]
    c = lax.dot_general(wch_ref[...], h, (((1,), (1,)), ((), ())), preferred_element_type=f32)
    qta_ref[...] = _group_norm_ch(c[0:512], gcha_ref[...], NSA_HEADS, tm).astype(bf16)
    qtb_ref[...] = c[512:768].astype(bf16)
    qtc_ref[...] = _group_norm_ch(c[768:1024], gchc_ref[...], SWA_HEADS, tm).astype(bf16)
    vtb_ref[...] = c[1024:1280].astype(bf16)
    gt_ref[...] = jax.nn.sigmoid(c[1280:1312])


def proj_in(x2, a_mod, sh_mod, wtok, wch, gtok, mtok, gcha, gchc, bd, *, B, S, tm=512):
    N, D = x2.shape
    tpb = S // tm
    row = lambda i: (i, 0)
    bmap = lambda i: (i // tpb, 0, 0)
    cmap = lambda i: (i // tpb, 0, i % tpb)
    full = lambda i: (0, 0)
    out_shape = (
        jax.ShapeDtypeStruct((N, 384), bf16), jax.ShapeDtypeStruct((N, 256), bf16), jax.ShapeDtypeStruct((N, 256), bf16),
        jax.ShapeDtypeStruct((B, 512, S), bf16), jax.ShapeDtypeStruct((B, 256, S), bf16),
        jax.ShapeDtypeStruct((B, 256, S), bf16), jax.ShapeDtypeStruct((B, 256, S), bf16),
        jax.ShapeDtypeStruct((B, 32, S), f32),
    )
    return pl.pallas_call(
        functools.partial(_proj_in_kernel, tm=tm),
        grid=(N // tm,),
        in_specs=[
            pl.BlockSpec((tm, D), row),
            pl.BlockSpec((None, 1, D), bmap), pl.BlockSpec((None, 1, D), bmap),
            pl.BlockSpec((D, TOK_W), full), pl.BlockSpec((CH_W, D), full),
            pl.BlockSpec((1, TOK_W), full), pl.BlockSpec((1, TOK_W), full),
            pl.BlockSpec((512, 1), full), pl.BlockSpec((256, 1), full),
            pl.BlockSpec((LANES, LANES), full),
        ],
        out_specs=(
            pl.BlockSpec((tm, 384), row), pl.BlockSpec((tm, 256), row), pl.BlockSpec((tm, 256), row),
            pl.BlockSpec((None, 512, tm), cmap), pl.BlockSpec((None, 256, tm), cmap),
            pl.BlockSpec((None, 256, tm), cmap), pl.BlockSpec((None, 256, tm), cmap),
            pl.BlockSpec((None, 32, tm), cmap),
        ),
        out_shape=out_shape,
        compiler_params=_cparams(("arbitrary",)),
        name="proj_in",
    )(x2, a_mod, sh_mod, wtok, wch, gtok, mtok, gcha, gchc, bd)


def _excl_suffix_prod8(t, rows):
    def shift_up(x, k):
        return jnp.where(rows < SUBLANES - k, pltpu.roll(x, SUBLANES - k, axis=0), 1.0)
    x = shift_up(t, 1)
    x = x * shift_up(x, 1)
    x = x * shift_up(x, 2)
    x = x * shift_up(x, 4)
    return x


SB_TK = 128
SB_TQ = 512
SB_NH = 2


def _sb_scores(k_ref, qpad, z_scr, j, slot):
    kb = k_ref[pl.ds(pl.multiple_of(j * SB_TK, SB_TK), SB_TK), :]
    for h in range(SB_NH):
        z_scr[slot, h] = jnp.dot(kb, qpad[h], preferred_element_type=f32)


def _sb_weights(z_scr, p_scr, a_scr, slot, laters, mask_off, TQ):
    V = SB_TK // SUBLANES
    rows = lax.broadcasted_iota(i32, (SUBLANES, TQ), 0)
    qidx = lax.broadcasted_iota(i32, (SUBLANES, TQ), 1)
    new_laters = []
    for h in range(SB_NH):
        run = jnp.ones((SUBLANES, TQ), f32)
        for v in reversed(range(V)):
            sl = slice(v * SUBLANES, (v + 1) * SUBLANES)
            beta = 0.5 - 0.5 * jnp.tanh(z_scr[slot, h, sl, :])
            if mask_off is not None:
                beta = jnp.where(rows * V + (v + mask_off) < qidx, beta, 1.0)
            nxt = run * beta
            p_scr[slot % 2, h, sl, :] = run - nxt
            run = nxt
        g = _excl_suffix_prod8(run, rows) * laters[h]
        for v2 in range(V // 2):
            sl = slice(2 * v2 * SUBLANES, (2 * v2 + 2) * SUBLANES)
            a_scr[slot, h, sl, :] = (p_scr[slot % 2, h, sl, :] * jnp.concatenate([g, g], axis=0)).astype(bf16)
        new_laters.append((g * run)[0:1, :])
    return new_laters


def _sb_accumulate(vt_ref, a_scr, acc_scr, j, slot):
    vtb = vt_ref[:, pl.ds(pl.multiple_of(j * SB_TK, SB_TK), SB_TK)]
    for h in range(SB_NH):
        acc_scr[h] += jnp.dot(vtb[h * HEAD_DIM:(h + 1) * HEAD_DIM], a_scr[slot, h], preferred_element_type=f32)


def _sb_kernel(k_ref, vt_ref, qt_ref, o_ref, acc_scr, z_scr, p_scr, a_scr, qpad, *, TQ):
    NBQ = TQ // SB_TK
    i = pl.program_id(2)
    qpad[...] = jnp.zeros_like(qpad)
    for h in range(SB_NH):
        qpad[h, h * HEAD_DIM:(h + 1) * HEAD_DIM, :] = qt_ref[h * HEAD_DIM:(h + 1) * HEAD_DIM, :]
    acc_scr[...] = jnp.zeros_like(acc_scr)
    a_scr[2] = jnp.zeros_like(a_scr[2])
    a_scr[3] = jnp.zeros_like(a_scr[3])
    j0 = i * NBQ + NBQ - 1
    assert NBQ % 4 == 0

    def step(j, par, laters, mask_offs):
        o = 2 * (1 - par)
        _sb_scores(k_ref, qpad, z_scr, jnp.maximum(j - 2, 0), o)
        _sb_scores(k_ref, qpad, z_scr, jnp.maximum(j - 3, 0), o + 1)
        _sb_accumulate(vt_ref, a_scr, acc_scr, jnp.minimum(j + 2, j0), o)
        _sb_accumulate(vt_ref, a_scr, acc_scr, jnp.minimum(j + 1, j0), o + 1)
        laters = _sb_weights(z_scr, p_scr, a_scr, 2 * par, laters, mask_offs[0], TQ)
        laters = _sb_weights(z_scr, p_scr, a_scr, 2 * par + 1, laters, mask_offs[1], TQ)
        return laters

    _sb_scores(k_ref, qpad, z_scr, j0, 0)
    _sb_scores(k_ref, qpad, z_scr, j0 - 1, 1)
    laters = [jnp.ones((1, TQ), f32)] * SB_NH
    for d in range(NBQ // 2):
        jd = NBQ - 1 - 2 * d
        laters = step(i * NBQ + jd, d % 2, laters, (jd * SB_TK, (jd - 1) * SB_TK))

    def body(t, carry):
        j = i * NBQ - 1 - 4 * t
        l = step(j, 0, list(carry), (None, None))
        return tuple(step(j - 2, 1, l, (None, None)))

    lax.fori_loop(0, i * (NBQ // 4), body, tuple(laters))
    _sb_accumulate(vt_ref, a_scr, acc_scr, 1, 2)
    _sb_accumulate(vt_ref, a_scr, acc_scr, 0, 3)
    o_ref[...] = jnp.concatenate([acc_scr[h].T for h in range(SB_NH)], axis=1).astype(bf16)


def sb_attention(kb_perm, vtb_perm, qtb, *, B, S, TQ=SB_TQ):
    TQ = min(TQ, S)
    W = SB_NH * HEAD_DIM
    return pl.pallas_call(
        functools.partial(_sb_kernel, TQ=TQ),
        grid=(B, SB_HEADS // SB_NH, S // TQ),
        in_specs=[
            pl.BlockSpec((None, S, W), lambda b, hp, i: (b, 0, hp)),
            pl.BlockSpec((None, W, S), lambda b, hp, i: (b, hp, 0)),
            pl.BlockSpec((None, W, TQ), lambda b, hp, i: (b, hp, i)),
        ],
        out_specs=pl.BlockSpec((None, TQ, W), lambda b, hp, i: (b, i, hp)),
        out_shape=jax.ShapeDtypeStruct((B, S, B_WIDTH), bf16),
        scratch_shapes=[pltpu.VMEM((SB_NH, HEAD_DIM, TQ), f32), pltpu.VMEM((4, SB_NH, SB_TK, TQ), f32),
                        pltpu.VMEM((2, SB_NH, SB_TK, TQ), f32), pltpu.VMEM((4, SB_NH, SB_TK, TQ), bf16),
                        pltpu.VMEM((SB_NH, W, TQ), bf16)],
        compiler_params=_cparams(("arbitrary", "arbitrary", "arbitrary")),
        name="sb_attention",
    )(kb_perm, vtb_perm, qtb)


def _perm_rows(z, S):
    B = z.shape[0]
    V = SB_TK // SUBLANES
    return z.reshape(B, S // SB_TK, SUBLANES, V, z.shape[-1]).swapaxes(2, 3).reshape(B, S, z.shape[-1])


def _perm_lanes(z, S):
    B, C = z.shape[0], z.shape[1]
    V = SB_TK // SUBLANES
    return z.reshape(B, C, S // SB_TK, SUBLANES, V).swapaxes(3, 4).reshape(B, C, S)


SWA_GQ = 4


def _swa_kernel(kp_ref, kc_ref, vp_ref, vc_ref, qt_ref, slope_ref, sink_ref, o_ref, *, TQ):
    step = pl.program_id(1)
    kall = jnp.concatenate([kp_ref[...], kc_ref[...]], axis=0)
    vall = jnp.concatenate([vp_ref[...], vc_ref[...]], axis=0)
    r = lax.broadcasted_iota(i32, (2 * TQ, 2 * TQ), 0)
    ql = lax.broadcasted_iota(i32, (2 * TQ, 2 * TQ), 1) % TQ
    dist = TQ + ql - r
    window = (dist >= 0) & (dist < SWA_WINDOW)
    distf = dist.astype(f32)
    zero = jnp.zeros((HEAD_DIM, 2 * TQ), bf16)
    for gq in range(SWA_GQ):
        i = step * SWA_GQ + gq
        kk = kall[gq * TQ:(gq + 2) * TQ]
        vv = vall[gq * TQ:(gq + 2) * TQ]
        qt = qt_ref[:, gq * TQ:(gq + 1) * TQ]
        mask = window & ((i - 1) * TQ + r >= 0)
        heads = []
        for g in range(SWA_KV):
            q2 = jnp.concatenate([qt[(2 * g) * 64:(2 * g + 1) * 64], qt[(2 * g + 1) * 64:(2 * g + 2) * 64]], axis=1)
            qpad = jnp.concatenate([q2, zero], axis=0) if g == 0 else jnp.concatenate([zero, q2], axis=0)
            s = jnp.dot(kk, qpad, preferred_element_type=f32)
            s = jnp.where(mask, s - slope_ref[g:g + 1, :] * distf, NEG)
            sink = sink_ref[g:g + 1, :]
            m = jnp.maximum(jnp.max(s, axis=0, keepdims=True), sink)
            p = jnp.where(mask, jnp.exp(s - m), 0.0)
            den = jnp.sum(p, axis=0, keepdims=True) + jnp.exp(sink - m)
            p = (p / den).astype(bf16)
            o = lax.dot_general(vv, p, (((0,), (0,)), ((), ())), preferred_element_type=f32)
            og = o[64 * g:64 * g + 64]
            heads += [og[:, 0:TQ], og[:, TQ:2 * TQ]]
        o_ref[gq * TQ:(gq + 1) * TQ, :] = jnp.concatenate(heads, axis=0).T.astype(bf16)


def swa_attention(kvc, qtc, slope_l, sink_l, *, B, S, TQ=Q_BLOCK):
    G = SWA_GQ
    prev = lambda c: (lambda b, s: (b, jnp.maximum(G * s - 1, 0), c))
    cur = lambda c: (lambda b, s: (b, s, c))
    return pl.pallas_call(
        functools.partial(_swa_kernel, TQ=TQ),
        grid=(B, S // (G * TQ)),
        in_specs=[
            pl.BlockSpec((None, TQ, LANES), prev(0)), pl.BlockSpec((None, G * TQ, LANES), cur(0)),
            pl.BlockSpec((None, TQ, LANES), prev(1)), pl.BlockSpec((None, G * TQ, LANES), cur(1)),
            pl.BlockSpec((None, C_WIDTH, G * TQ), lambda b, s: (b, 0, s)),
            pl.BlockSpec((SWA_KV, 2 * TQ), lambda b, s: (0, 0)),
            pl.BlockSpec((SWA_KV, 2 * TQ), lambda b, s: (0, 0)),
        ],
        out_specs=pl.BlockSpec((None, G * TQ, C_WIDTH), lambda b, s: (b, s, 0)),
        out_shape=jax.ShapeDtypeStruct((B, S, C_WIDTH), bf16),
        compiler_params=_cparams(("arbitrary", "arbitrary")),
        name="swa_attention",
    )(kvc, kvc, kvc, kvc, qtc, slope_l, sink_l)


def _nsa_compress_kernel(ch_ref, w1_ref, pe_ref, w2_ref, gain_ref, o_ref, *, NCP):
    w1 = w1_ref[...]
    p = jnp.dot(ch_ref[...], w1, preferred_element_type=f32)
    pb = jnp.dot(pe_ref[...], w1, preferred_element_type=f32)
    bias = pb[0:1, 0:512] + pb[1:2, 512:1024]
    hid = p[:, 0:512] + pltpu.roll(p[:, 512:1024], NCP - 1, axis=0) + bias
    y = jnp.dot(jax.nn.gelu(hid).astype(bf16), w2_ref[...], preferred_element_type=f32)
    is_k = lax.broadcasted_iota(i32, y.shape, 1) < HEAD_DIM
    ss = jnp.sum(jnp.where(is_k, y * y, 0.0), axis=-1, keepdims=True)
    yn = y * lax.rsqrt(ss * (1.0 / HEAD_DIM) + EPS) * gain_ref[...]
    o_ref[...] = jnp.where(is_k, yn, y).astype(bf16)


def nsa_compress(ch, w1big, pe2, w2big, gain, *, B, S):
    NCP = S // CMP_STRIDE
    W = CMP_STRIDE * LANES
    return pl.pallas_call(
        functools.partial(_nsa_compress_kernel, NCP=NCP),
        grid=(B,),
        in_specs=[
            pl.BlockSpec((None, NCP, W), lambda b: (b, 0, 0)),
            pl.BlockSpec((W, 4 * CMP_HID), lambda b: (0, 0)),
            pl.BlockSpec((SUBLANES, W), lambda b: (0, 0)),
            pl.BlockSpec((2 * CMP_HID, LANES), lambda b: (0, 0)),
            pl.BlockSpec((1, LANES), lambda b: (0, 0)),
        ],
        out_specs=pl.BlockSpec((None, NCP, LANES), lambda b: (b, 0, 0)),
        out_shape=jax.ShapeDtypeStruct((B, NCP, LANES), bf16),
        compiler_params=_cparams(("arbitrary",)),
        name="nsa_compress",
    )(ch, w1big, pe2, w2big, gain)


def _q_all_heads(qt, TQ, extra=None):
    qr = jnp.concatenate([qt[h * HEAD_DIM:(h + 1) * HEAD_DIM] for h in range(NSA_HEADS)], axis=1)
    if extra is None:
        return jnp.concatenate([qr, jnp.zeros_like(qr)], axis=0)
    pad = jnp.zeros((HEAD_DIM - extra.shape[0], qr.shape[1]), qr.dtype)
    return jnp.concatenate([qr, extra.astype(qr.dtype), pad], axis=0)


POS_COLS = 16
NSA_CMP_CHUNK = 64


def _pos_operand(pos, extra=None):
    n = pos.shape[0]
    col = lax.broadcasted_iota(i32, (n, LANES), 1) - HEAD_DIM
    a = jnp.right_shift(pos, 7)
    b = jnp.bitwise_and(pos, 127)
    val = jnp.where((col >= 0) & (col < 6), jnp.where(col % 2 == 0, a, b), 0)
    if extra is not None:
        val = jnp.where(col == extra, 1, val)
    return val.astype(f32).astype(bf16)


def _nsa_select_kernel(qt_ref, kv_ref, cend_ref, posr_ref, ocmp_ref, sel_ref, flag_ref,
                       s_scr, ps_scr, o_scr, *, TQ, S, CH):
    NSEL = S // SEL_BLOCK
    R = SEL_BLOCK // CMP_STRIDE
    KSEL = min(SEL_TOPK, NSEL)
    HW = NSA_HEADS * TQ
    i = pl.program_id(1)
    posr = posr_ref[...]
    qpad = _q_all_heads(qt_ref[...], TQ, extra=jnp.concatenate([posr, jnp.zeros_like(posr)], axis=0))
    t = i * TQ + lax.broadcasted_iota(i32, (1, HW), 1) % TQ
    tn = (((0,), (0,)), ((), ()))
    nq = jnp.minimum((2 * i + 1) // CH + 1, NSEL // CH)
    ps_scr[...] = jnp.zeros_like(ps_scr)
    o_scr[...] = jnp.zeros_like(o_scr)

    def rows_of(r, q):
        return pl.ds(pl.multiple_of(r * NSEL + q * CH, CH), CH)

    def scores(q, m):
        for r in range(R):
            rows = rows_of(r, q)
            kv = kv_ref[rows, :]
            dist = jnp.maximum(i * TQ + (TQ - 1) - cend_ref[rows, :], 0)
            is_k = lax.broadcasted_iota(i32, kv.shape, 1) < HEAD_DIM
            s = jnp.dot(jnp.where(is_k, kv, _pos_operand(dist)), qpad, preferred_element_type=f32)
            s = jnp.where(cend_ref[rows, :] <= t, s, NEG)
            s_scr[r, pl.ds(pl.multiple_of(q * CH, CH), CH), :] = s
            m = jnp.maximum(m, jnp.max(s, axis=0, keepdims=True))
        return m

    m = lax.fori_loop(0, nq, scores, jnp.full((1, HW), NEG, f32))
    m = jnp.maximum(m, 0.1 * NEG)

    def probs(q, l):
        for r in range(R):
            sl = pl.ds(pl.multiple_of(q * CH, CH), CH)
            p = jnp.exp2(s_scr[r, sl, :] - m)
            s_scr[r, sl, :] = p
            l = l + jnp.sum(p, axis=0, keepdims=True)
            o_scr[...] += lax.dot_general(kv_ref[rows_of(r, q), :], p.astype(bf16), tn, preferred_element_type=f32)
        return l

    l = lax.fori_loop(0, nq, probs, jnp.zeros((1, HW), f32))
    inv = 1.0 / jnp.maximum(l, 1e-30)
    ocmp_ref[...] = (o_scr[HEAD_DIM:2 * HEAD_DIM, :] * inv).astype(ocmp_ref.dtype)

    def head_sums(q, carry):
        for r in range(R):
            sl = pl.ds(pl.multiple_of(q * CH, CH), CH)
            pn = s_scr[r, sl, :] * inv
            acc = pn[:, 0:TQ]
            for h in range(1, NSA_HEADS):
                acc = acc + pn[:, h * TQ:(h + 1) * TQ]
            ps_scr[r, sl, :] = acc
        return carry

    lax.fori_loop(0, nq, head_sums, 0)
    jrow = lax.broadcasted_iota(i32, (NSEL, TQ), 0)
    ps3 = ps_scr[3]
    prev3 = jnp.where(jrow == 0, 0.0, pltpu.roll(ps3, 1, axis=0))
    imp = 0.5 * prev3 + ps_scr[0] + ps_scr[1] + ps_scr[2] + 0.5 * ps3
    tq = i * TQ + lax.broadcasted_iota(i32, (1, TQ), 1)
    cur = tq // SEL_BLOCK
    forced = (jrow == 0) | (jrow == cur) | (jrow == cur - 1)
    causal = jrow <= cur
    score = jnp.where(causal, jnp.where(forced, BIG, imp), -1e38)
    jf = jrow.astype(f32)
    taken = jnp.zeros((NSEL, TQ), jnp.bool_)
    picks = []
    for _ in range(KSEL):
        mx = jnp.max(score, axis=0, keepdims=True)
        idx = jnp.min(jnp.where(score == mx, jf, float(NSEL)), axis=0, keepdims=True)
        hit = jf == idx
        taken = taken | hit
        score = jnp.where(hit, -jnp.inf, score)
        picks.append(idx)
    picks += [picks[-1]] * (SUBLANES - KSEL)
    sel_ref[...] = jnp.concatenate(picks, axis=0).astype(i32)
    used = jnp.where(taken & causal, 1.0, 0.0).astype(bf16)
    flag_ref[...] = lax.dot_general(jnp.ones((SUBLANES, TQ), bf16), used, (((1,), (1,)), ((), ())),
                                    preferred_element_type=f32)


def nsa_select(qta, kvcmp, cend, posr, *, B, S, TQ=Q_BLOCK):
    NB = S // TQ
    NCP = S // CMP_STRIDE
    NSEL = S // SEL_BLOCK
    R = SEL_BLOCK // CMP_STRIDE
    HW = NSA_HEADS * TQ
    CH = min(NSA_CMP_CHUNK, NSEL)
    return pl.pallas_call(
        functools.partial(_nsa_select_kernel, TQ=TQ, S=S, CH=CH),
        grid=(B, NB),
        in_specs=[
            pl.BlockSpec((None, A_WIDTH, TQ), lambda b, i: (b, 0, i)),
            pl.BlockSpec((None, NCP, LANES), lambda b, i: (b, 0, 0)),
            pl.BlockSpec((NCP, 1), lambda b, i: (0, 0)),
            pl.BlockSpec((SUBLANES, HW), lambda b, i: (0, 0)),
        ],
        out_specs=(
            pl.BlockSpec((None, None, HEAD_DIM, HW), lambda b, i: (b, i, 0, 0)),
            pl.BlockSpec((None, SUBLANES, TQ), lambda b, i: (b, 0, i)),
            pl.BlockSpec((None, None, SUBLANES, NSEL), lambda b, i: (b, i, 0, 0)),
        ),
        out_shape=(
            jax.ShapeDtypeStruct((B, NB, HEAD_DIM, HW), bf16),
            jax.ShapeDtypeStruct((B, SUBLANES, S), i32),
            jax.ShapeDtypeStruct((B, NB, SUBLANES, NSEL), f32),
        ),
        scratch_shapes=[pltpu.VMEM((R, NSEL, HW), f32), pltpu.VMEM((R, NSEL, TQ), f32), pltpu.VMEM((LANES, HW), f32)],
        compiler_params=_cparams(("arbitrary", "arbitrary")),
        name="nsa_select",
    )(qta, kvcmp, cend, posr)


_SLC_CHUNK = 4


def _nsa_attend_kernel(cnt_ref, lst_ref, qt_ref, kslc_ref, kwin_ref, sel_ref, ocmp_ref, gt_ref, posr_ref,
                       winb_ref, diagb_ref, o_ref, acc_scr, m_scr, l_scr, qa_scr, s_scr, p_scr, al_scr, *, TQ, S):
    NB = S // TQ
    NSEL = S // SEL_BLOCK
    HW = NSA_HEADS * TQ
    b = pl.program_id(0)
    i = pl.program_id(1)
    count = cnt_ref[b * NB + i]
    qpad = _q_all_heads(qt_ref[...], TQ)
    qa_scr[...] = qpad
    posr = posr_ref[...]
    sel = sel_ref[...]
    tn = (((0,), (0,)), ((), ()))

    kd = kslc_ref[pl.ds(pl.multiple_of(i * TQ, TQ), TQ), :]
    s = jnp.dot(kd, qpad, preferred_element_type=f32) + diagb_ref[...]
    m = jnp.max(s, axis=0, keepdims=True)
    p = jnp.exp2(s - m)
    m_scr[...] = m
    l_scr[...] = jnp.sum(p, axis=0, keepdims=True)
    acc_scr[...] = lax.dot_general(kd, p.astype(bf16), tn, preferred_element_type=f32)[HEAD_DIM:2 * HEAD_DIM]

    def gather(c):
        blks, js, oks = [], [], []
        for b4 in range(_SLC_CHUNK):
            idx = _SLC_CHUNK * c + b4
            ok = (idx >= 0) & (idx < count)
            j = jnp.where(ok, lst_ref[0, 0, jnp.clip(idx, 0, NSEL - 1)], 0)
            blks.append(kslc_ref[pl.ds(pl.multiple_of(j * SEL_BLOCK, SEL_BLOCK), SEL_BLOCK), :])
            js.append(j)
            oks.append(ok)
        return jnp.concatenate(blks, axis=0), js, oks

    def scores(c, par):
        kc, js, oks = gather(c)
        pos, pens = [], []
        for j, ok in zip(js, oks):
            pos.append(i * TQ + (TQ - 1) - j * SEL_BLOCK - lax.broadcasted_iota(i32, (SEL_BLOCK, 1), 0))
            picked = jnp.max(jnp.where(sel == j, 1.0, 0.0), axis=0, keepdims=True) > 0.0
            pen = jnp.where(picked & ok, 0.0, NEG)
            pens.append(jnp.concatenate([pen] * NSA_HEADS, axis=1))
        blk_id = lax.broadcasted_iota(i32, (_SLC_CHUNK * SEL_BLOCK, 1), 0) // SEL_BLOCK
        pm = _pos_operand(jnp.concatenate(pos, axis=0), extra=SUBLANES + blk_id)
        is_k = lax.broadcasted_iota(i32, kc.shape, 1) < HEAD_DIM
        pens += [jnp.zeros((1, HW), f32)] * (SUBLANES - _SLC_CHUNK)
        qa_scr[HEAD_DIM:HEAD_DIM + POS_COLS, :] = jnp.concatenate([posr, jnp.concatenate(pens, axis=0)],
                                                                  axis=0).astype(bf16)
        s_scr[par] = jnp.dot(jnp.where(is_k, kc, pm), qa_scr[...], preferred_element_type=f32)

    def weights(par):
        s = s_scr[par]
        m_old = m_scr[...]
        m_new = jnp.maximum(m_old, jnp.max(s, axis=0, keepdims=True))
        alpha = jnp.exp2(m_old - m_new)
        p = jnp.exp2(s - m_new)
        l_scr[...] = l_scr[...] * alpha + jnp.sum(p, axis=0, keepdims=True)
        p_scr[par] = p.astype(bf16)
        al_scr[par] = alpha
        m_scr[...] = m_new

    def values(c, par):
        kc, _, _ = gather(c)
        pv = lax.dot_general(kc, p_scr[par], tn, preferred_element_type=f32)
        acc_scr[...] = acc_scr[...] * al_scr[par] + pv[HEAD_DIM:2 * HEAD_DIM]

    nch = (count + _SLC_CHUNK - 1) // _SLC_CHUNK
    p_scr[1] = jnp.zeros_like(p_scr[1])
    al_scr[1] = jnp.ones_like(al_scr[1])
    scores(0, 0)

    def body(t, carry):
        c = 2 * t
        values(c - 1, 1)
        scores(c + 1, 1)
        weights(0)
        values(c, 0)
        scores(c + 2, 0)
        weights(1)
        return carry

    npairs = (nch + 1) // 2
    lax.fori_loop(0, npairs, body, 0)
    values(2 * npairs - 1, 1)
    o_slc = acc_scr[...] / l_scr[...]

    NWB = NSA_WINDOW // TQ + 1
    blks, biases = [], []
    for w in range(NWB):
        jb = i - (NWB - 1) + w
        blks.append(kwin_ref[pl.ds(pl.multiple_of(jnp.maximum(jb, 0) * TQ, TQ), TQ), :])
        biases.append(winb_ref[pl.ds(pl.multiple_of(jnp.where(jb >= 0, w, NWB) * TQ, TQ), TQ), :])
    kw = jnp.concatenate(blks, axis=0)
    s = jnp.dot(kw, qpad, preferred_element_type=f32)
    parts = [s[w * TQ:(w + 1) * TQ] + biases[w] for w in range(NWB)]
    m = parts[0].max(axis=0, keepdims=True)
    for part in parts[1:]:
        m = jnp.maximum(m, part.max(axis=0, keepdims=True))
    ps = [jnp.exp2(part - m) for part in parts]
    l = ps[0].sum(axis=0, keepdims=True)
    for part in ps[1:]:
        l = l + part.sum(axis=0, keepdims=True)
    pw = jnp.concatenate(ps, axis=0).astype(bf16)
    o_win = lax.dot_general(kw, pw, tn, preferred_element_type=f32)[HEAD_DIM:2 * HEAD_DIM] * (1.0 / l)

    gt = gt_ref[...]
    gate = [jnp.concatenate([gt[br * NSA_HEADS + h:br * NSA_HEADS + h + 1] for h in range(NSA_HEADS)], axis=1)
            for br in range(3)]
    o = gate[0] * ocmp_ref[...].astype(f32) + gate[1] * o_slc + gate[2] * o_win
    o = jnp.concatenate([o[:, h * TQ:(h + 1) * TQ] for h in range(NSA_HEADS)], axis=0)
    o_ref[...] = o.T.astype(bf16)


def nsa_attend(counts, lists, qta, kva, selT, ocmp, gT, posr, winb, diagb, *, B, S, TQ=Q_BLOCK):
    NB = S // TQ
    NSEL = S // SEL_BLOCK
    HW = NSA_HEADS * TQ
    NWB = NSA_WINDOW // TQ + 1
    const = lambda b, i, cnt: (0, 0)
    grid_spec = pltpu.PrefetchScalarGridSpec(
        num_scalar_prefetch=1,
        grid=(B, NB),
        in_specs=[
            pl.BlockSpec((1, 1, NSEL), lambda b, i, cnt: (b * NB + i, 0, 0), memory_space=pltpu.SMEM),
            pl.BlockSpec((None, A_WIDTH, TQ), lambda b, i, cnt: (b, 0, i)),
            pl.BlockSpec((None, S, LANES), lambda b, i, cnt: (b, 0, 1)),
            pl.BlockSpec((None, S, LANES), lambda b, i, cnt: (b, 0, 2)),
            pl.BlockSpec((None, SUBLANES, TQ), lambda b, i, cnt: (b, 0, i)),
            pl.BlockSpec((None, None, HEAD_DIM, HW), lambda b, i, cnt: (b, i, 0, 0)),
            pl.BlockSpec((None, 32, TQ), lambda b, i, cnt: (b, 0, i)),
            pl.BlockSpec((SUBLANES, HW), const),
            pl.BlockSpec(((NWB + 1) * TQ, HW), const),
            pl.BlockSpec((TQ, HW), const),
        ],
        out_specs=pl.BlockSpec((None, TQ, A_WIDTH), lambda b, i, cnt: (b, i, 0)),
        scratch_shapes=[pltpu.VMEM((HEAD_DIM, HW), f32), pltpu.VMEM((1, HW), f32), pltpu.VMEM((1, HW), f32),
                        pltpu.VMEM((LANES, HW), bf16),
                        pltpu.VMEM((2, _SLC_CHUNK * SEL_BLOCK, HW), f32),
                        pltpu.VMEM((2, _SLC_CHUNK * SEL_BLOCK, HW), bf16),
                        pltpu.VMEM((2, 1, HW), f32)],
    )
    return pl.pallas_call(
        functools.partial(_nsa_attend_kernel, TQ=TQ, S=S),
        grid_spec=grid_spec,
        out_shape=jax.ShapeDtypeStruct((B, S, A_WIDTH), bf16),
        compiler_params=_cparams(("arbitrary", "arbitrary")),
        name="nsa_attend",
    )(counts, lists, qta, kva, kva, selT, ocmp, gT, posr, winb, diagb)


def nsa_constants(slopes, S, TQ=Q_BLOCK):
    NSEL = S // SEL_BLOCK
    NCP = S // CMP_STRIDE
    R = SEL_BLOCK // CMP_STRIDE
    NWB = NSA_WINDOW // TQ + 1
    s2 = jnp.repeat(slopes * np.float32(np.log2(np.e)), TQ)
    hi = s2.astype(bf16).astype(f32)
    mid = (s2 - hi).astype(bf16).astype(f32)
    lo = (s2 - hi - mid).astype(bf16).astype(f32)
    zero = jnp.zeros_like(s2)
    posr = -jnp.stack([128.0 * hi, hi, 128.0 * mid, mid, 128.0 * lo, lo, zero, zero], axis=0)
    rho = np.arange(NCP)
    c = R * (rho % NSEL) + rho // NSEL
    cend = CMP_STRIDE * c + (CMP_LEN - 1)
    cend_mask = np.where(c < NCP - 1, cend, np.iinfo(np.int32).max)
    tl = jnp.tile(jnp.arange(TQ, dtype=i32), NSA_HEADS)[None, :]
    wpos = jnp.arange((NWB + 1) * TQ, dtype=i32)[:, None]
    dist = (NWB - 1) * TQ + tl - wpos
    winb = jnp.where((dist >= 0) & (dist < NSA_WINDOW) & (wpos < NWB * TQ),
                     -s2[None, :] * (NWB * TQ - 1 - wpos).astype(f32), NEG)
    dpos = jnp.arange(TQ, dtype=i32)[:, None]
    diagb = jnp.where(dpos <= tl, -s2[None, :] * (TQ - 1 - dpos).astype(f32), NEG)
    return (posr, jnp.asarray(cend_mask[:, None], i32), winb, diagb)


def nsa_branch(qta, kva, gT, cmp_w, nsa_c, *, B, S):
    NB = S // Q_BLOCK
    NSEL = S // SEL_BLOCK
    NCP = S // CMP_STRIDE
    R = SEL_BLOCK // CMP_STRIDE
    posr, cend, winb, diagb = nsa_c
    kva = kva.reshape(B, S, 6 * HEAD_DIM)
    ch = kva[:, :, 0:LANES].reshape(B, NCP, CMP_STRIDE * LANES)
    kvcmp = nsa_compress(ch, *cmp_w, B=B, S=S)
    kvcmp = kvcmp.reshape(B, NSEL, R, LANES).swapaxes(1, 2).reshape(B, NCP, LANES)
    ocmp, selT, flags = nsa_select(qta, kvcmp, cend, posr, B=B, S=S)
    ar = jnp.arange(NSEL, dtype=i32)
    own = (Q_BLOCK // SEL_BLOCK) * jnp.arange(NB, dtype=i32)
    used = (flags[:, :, 0, :] > 0.5) & (ar[None, None, :] < own[None, :, None])
    lists = jnp.minimum(jnp.sort(jnp.where(used, ar, ar + NSEL), axis=-1), NSEL - 1)
    counts = jnp.sum(used, axis=-1).astype(i32)
    return nsa_attend(counts.reshape(B * NB), lists.reshape(B * NB, 1, NSEL), qta, kva, selT, ocmp, gT,
                      posr, winb, diagb, B=B, S=S)


def _route(logit, rb):
    sc = jax.nn.sigmoid(logit)
    sel = sc + rb
    srow = [sel[e:e + 1] for e in range(N_EXPERTS)]
    crow = [sc[e:e + 1] for e in range(N_EXPERTS)]
    gscore = []
    for g in range(N_GROUPS):
        a, b, c, d = srow[4 * g:4 * g + 4]
        top2 = jnp.maximum(jnp.maximum(jnp.maximum(a + b, a + c), jnp.maximum(a + d, b + c)),
                           jnp.maximum(b + d, c + d))
        gscore.append(top2)
    best, gi = gscore[0], jnp.zeros_like(gscore[0], dtype=i32)
    for g in range(1, N_GROUPS):
        better = gscore[g] > best
        gi = jnp.where(better, g, gi)
        best = jnp.where(better, gscore[g], best)

    def pick_group(rows, k):
        v = rows[k]
        for g in range(1, N_GROUPS):
            v = jnp.where(gi == g, rows[4 * g + k], v)
        return v

    iv = [pick_group(srow, k) for k in range(EXPERTS_PER_GROUP)]
    ic = [pick_group(crow, k) for k in range(EXPERTS_PER_GROUP)]
    b1, i1, w1 = iv[0], jnp.zeros_like(gi), ic[0]
    for k in range(1, EXPERTS_PER_GROUP):
        better = iv[k] > b1
        i1 = jnp.where(better, k, i1)
        w1 = jnp.where(better, ic[k], w1)
        b1 = jnp.where(better, iv[k], b1)
    b2 = jnp.full_like(b1, -jnp.inf)
    i2, w2 = jnp.zeros_like(gi), jnp.zeros_like(w1)
    for k in range(EXPERTS_PER_GROUP):
        better = (i1 != k) & (iv[k] > b2)
        i2 = jnp.where(better, k, i2)
        w2 = jnp.where(better, ic[k], w2)
        b2 = jnp.where(better, iv[k], b2)
    tot = w1 + w2
    eidx = jnp.concatenate([gi * EXPERTS_PER_GROUP + i1, gi * EXPERTS_PER_GROUP + i2], axis=0)
    ew = jnp.concatenate([w1 / tot, w2 / tot], axis=0)
    return eidx, ew


def _mix_out_kernel(x_ref, a_ref, sh_ref, ga_ref, oa_ref, ob_ref, oc_ref, wm_ref, wa_ref, wb_ref, wc_ref, wo_ref,
                    am_ref, shm_ref, wr_ref, rb_ref, xo_ref, h2_ref, eidx_ref, ew_ref):
    x = x_ref[...]
    h = _norm_mod(x, a_ref[...], sh_ref[...]).astype(bf16)
    gates = jax.nn.sigmoid(jnp.dot(h, wm_ref[...], preferred_element_type=f32))
    D = D_MODEL
    y = gates[:, 0:D] * jnp.dot(oa_ref[...], wa_ref[...], preferred_element_type=f32)
    y = y + gates[:, D:2 * D] * jnp.dot(ob_ref[...], wb_ref[...], preferred_element_type=f32)
    y = y + gates[:, 2 * D:3 * D] * jnp.dot(oc_ref[...], wc_ref[...], preferred_element_type=f32)
    xn = x + ga_ref[...] * jnp.dot(y.astype(bf16), wo_ref[...], preferred_element_type=f32)
    xo_ref[...] = xn
    h2 = _norm_mod(xn, am_ref[...], shm_ref[...])
    hi = h2.astype(bf16)
    h2_ref[...] = hi
    lo = (h2 - hi.astype(f32)).astype(bf16)
    wr = wr_ref[...]
    whi = wr.astype(bf16)
    wlo = (wr - whi.astype(f32)).astype(bf16)
    nt = lambda p, q: lax.dot_general(p, q, (((1,), (1,)), ((), ())), preferred_element_type=f32)
    logit = nt(whi, hi) + nt(whi, lo) + nt(wlo, hi)
    eidx, ew = _route(logit, rb_ref[...])
    eidx_ref[...] = eidx
    ew_ref[...] = ew


def mix_out(x2, a_mod, sh_mod, g_a, oa, ob, oc, wm, wa, wb, wc, wo, am_moe, shm_moe, wrT, rb, *, B, S, tm=1024):
    N, D = x2.shape
    tpb = S // tm
    row = lambda i: (i, 0)
    bmap = lambda i: (i // tpb, 0, 0)
    full = lambda i: (0, 0)
    col = lambda i: (0, i)
    return pl.pallas_call(
        _mix_out_kernel,
        grid=(N // tm,),
        in_specs=[
            pl.BlockSpec((tm, D), row),
            pl.BlockSpec((None, 1, D), bmap), pl.BlockSpec((None, 1, D), bmap), pl.BlockSpec((None, 1, D), bmap),
            pl.BlockSpec((tm, A_WIDTH), row), pl.BlockSpec((tm, B_WIDTH), row), pl.BlockSpec((tm, C_WIDTH), row),
            pl.BlockSpec((D, 3 * D), full), pl.BlockSpec((A_WIDTH, D), full), pl.BlockSpec((B_WIDTH, D), full),
            pl.BlockSpec((C_WIDTH, D), full), pl.BlockSpec((D, D), full),
            pl.BlockSpec((None, 1, D), bmap), pl.BlockSpec((None, 1, D), bmap),
            pl.BlockSpec((N_EXPERTS, D), full), pl.BlockSpec((N_EXPERTS, 1), full),
        ],
        out_specs=(pl.BlockSpec((tm, D), row), pl.BlockSpec((tm, D), row),
                   pl.BlockSpec((TOP_K, tm), col), pl.BlockSpec((TOP_K, tm), col)),
        out_shape=(jax.ShapeDtypeStruct((N, D), f32), jax.ShapeDtypeStruct((N, D), bf16),
                   jax.ShapeDtypeStruct((TOP_K, N), i32), jax.ShapeDtypeStruct((TOP_K, N), f32)),
        compiler_params=_cparams(("arbitrary",)),
        name="mix_out",
    )(x2, a_mod, sh_mod, g_a, oa, ob, oc, wm, wa, wb, wc, wo, am_moe, shm_moe, wrT, rb)


def _moe_ffn_kernel(be_ref, nu_ref, xs_ref, wg_ref, wu_ref, wd_ref, y_ref, wg_s, wu_s, wd_s):
    i = pl.program_id(0)

    @pl.when((i == 0) | (be_ref[i] != be_ref[jnp.maximum(i - 1, 0)]))
    def _():
        wg_s[...] = wg_ref[...].astype(bf16)
        wu_s[...] = wu_ref[...].astype(bf16)
        wd_s[...] = wd_ref[...].astype(bf16)

    @pl.when(i < nu_ref[0])
    def _():
        xs = xs_ref[...]
        g = jnp.dot(xs, wg_s[...], preferred_element_type=f32)
        u = jnp.dot(xs, wu_s[...], preferred_element_type=f32)
        a = (jax.nn.silu(g) * u).astype(bf16)
        y_ref[...] = jnp.dot(a, wd_s[...], preferred_element_type=f32).astype(y_ref.dtype)

    @pl.when(i >= nu_ref[0])
    def _():
        y_ref[...] = jnp.zeros_like(y_ref)


def moe_ffn(blk_e, n_used, xs, wg, wu, wd, *, layer):
    n_rows, D = xs.shape
    n_blk = n_rows // MOE_BLOCK
    wmap = lambda i, be, nu: (layer, be[i], 0, 0)
    grid_spec = pltpu.PrefetchScalarGridSpec(
        num_scalar_prefetch=2,
        grid=(n_blk,),
        in_specs=[
            pl.BlockSpec((MOE_BLOCK, D), lambda i, be, nu: (i, 0)),
            pl.BlockSpec((None, None, D, D_EXPERT), wmap),
            pl.BlockSpec((None, None, D, D_EXPERT), wmap),
            pl.BlockSpec((None, None, D_EXPERT, D), wmap),
        ],
        out_specs=pl.BlockSpec((MOE_BLOCK, D), lambda i, be, nu: (i, 0)),
        scratch_shapes=[pltpu.VMEM((D, D_EXPERT), bf16), pltpu.VMEM((D, D_EXPERT), bf16),
                        pltpu.VMEM((D_EXPERT, D), bf16)],
    )
    return pl.pallas_call(
        _moe_ffn_kernel,
        grid_spec=grid_spec,
        out_shape=jax.ShapeDtypeStruct((n_rows, D), bf16),
        compiler_params=_cparams(("arbitrary",)),
        name="moe_ffn",
    )(blk_e, n_used, xs, wg, wu, wd)


def moe_dispatch(eidx, N):
    NK = N * TOP_K
    n_blk = -(-NK // MOE_BLOCK) + N_EXPERTS
    n_slot = n_blk * MOE_BLOCK
    experts = jnp.arange(N_EXPERTS, dtype=i32)
    flat_e = eidx.T.reshape(-1)
    counts = jnp.sum(flat_e[:, None] == experts[None, :], axis=0, dtype=i32)
    padded = (counts + MOE_BLOCK - 1) // MOE_BLOCK * MOE_BLOCK
    p_end = jnp.cumsum(padded)
    blk_first = jnp.arange(n_blk, dtype=i32) * MOE_BLOCK
    blk_e = jnp.minimum(jnp.sum(p_end[None, :] <= blk_first[:, None], axis=1, dtype=i32), N_EXPERTS - 1)
    n_used = (p_end[-1] // MOE_BLOCK).astype(i32).reshape(1)
    stride = NK + MOE_BLOCK
    pair = jnp.arange(NK, dtype=i32)
    q = jnp.arange(MOE_BLOCK, dtype=i32)
    pad_key = jnp.where(q[None, :] < (padded - counts)[:, None], experts[:, None] * stride + NK + q[None, :],
                        jnp.iinfo(jnp.int32).max)
    keys = jnp.concatenate([flat_e * stride + pair, pad_key.reshape(-1)])
    pair_or_pad = jnp.concatenate([pair, jnp.full((n_slot - NK,), NK, i32)])
    _, slot_pair = lax.sort((keys, pair_or_pad), num_keys=1)
    slot_tok = jnp.minimum(slot_pair // TOP_K, N - 1)
    _, pos = lax.sort((slot_pair, jnp.arange(n_slot, dtype=i32)), num_keys=1)
    return slot_tok, blk_e, n_used, pos[:NK].reshape(N, TOP_K)


def _prep_cmp_weights(pe_k, w1_k, w2_k, pe_v, w1_v, w2_v, k_gain0):
    T = CMP_STRIDE
    w1big = jnp.zeros((T, 2, HEAD_DIM, 4, CMP_HID), f32)
    w1k = w1_k.reshape(2, T, HEAD_DIM, CMP_HID)
    w1v = w1_v.reshape(2, T, HEAD_DIM, CMP_HID)
    w1big = w1big.at[:, 0, :, 0].set(w1k[0]).at[:, 1, :, 1].set(w1v[0])
    w1big = w1big.at[:, 0, :, 2].set(w1k[1]).at[:, 1, :, 3].set(w1v[1])
    w1big = w1big.reshape(T * LANES, 4 * CMP_HID).astype(bf16)
    pe = jnp.stack([pe_k.reshape(2, T, HEAD_DIM), pe_v.reshape(2, T, HEAD_DIM)], axis=2)
    pe2 = jnp.zeros((SUBLANES, T * LANES), f32).at[0:2].set(pe.reshape(2, T * LANES)).astype(bf16)
    w2big = jnp.zeros((2 * CMP_HID, LANES), f32)
    w2big = w2big.at[0:CMP_HID, 0:HEAD_DIM].set(w2_k).at[CMP_HID:, HEAD_DIM:].set(w2_v).astype(bf16)
    gain = jnp.concatenate([k_gain0, jnp.ones((HEAD_DIM,), f32)])[None, :]
    return w1big, pe2, w2big, gain


def _prep_proj_weights(w_in, nsa_q_gain, nsa_k_gain, swa_q_gain, swa_k_gain):
    w = w_in
    wtok = jnp.concatenate([w[:, _O_KVA:_O_GA], w[:, _O_KB:_O_VB], w[:, _O_KC:_O_MG]], axis=1).astype(bf16)
    gate_perm = np.array([h * 3 + br for br in range(3) for h in range(NSA_HEADS)])
    scale = HEAD_DIM ** -0.5
    wch = jnp.concatenate([w[:, _O_QA:_O_KVA], w[:, _O_QB:_O_KB] * (0.5 * scale), w[:, _O_QC:_O_KC],
                           w[:, _O_VB:_O_QC], w[:, _O_GA:_O_QB][:, gate_perm], jnp.zeros((D_MODEL, 8), f32)],
                          axis=1).T.astype(bf16)
    gtok = jnp.zeros((1, TOK_W), f32)
    gtok = gtok.at[0, 128:192].set(nsa_k_gain[1]).at[0, 256:320].set(nsa_k_gain[2])
    gtok = gtok.at[0, 640:704].set(swa_k_gain).at[0, 704:768].set(swa_k_gain)
    mtok = jnp.zeros((1, TOK_W), f32).at[0, 128:192].set(1.0).at[0, 256:320].set(1.0).at[0, 640:768].set(1.0)
    gcha = (jnp.tile(nsa_q_gain, NSA_HEADS) * (scale * np.float32(np.log2(np.e))))[:, None]
    gchc = (jnp.tile(swa_q_gain, SWA_HEADS) * scale)[:, None]
    blk = np.arange(LANES) // HEAD_DIM
    bd = jnp.asarray(blk[:, None] == blk[None, :], bf16)
    return wtok, wch, gtok, mtok, gcha, gchc, bd


def _adaln_kernel(c_ref, w_ref, b_ref, o_ref):
    cond = jax.nn.silu(c_ref[...]).astype(bf16)
    o_ref[...] = jnp.dot(cond, w_ref[...].astype(bf16), preferred_element_type=f32) + b_ref[...]


def adaln(c_pad, w_ada, b_ada):
    L, D, D6 = w_ada.shape
    return pl.pallas_call(
        _adaln_kernel,
        grid=(L, D6 // D),
        in_specs=[
            pl.BlockSpec((SUBLANES, D), lambda l, j: (0, 0)),
            pl.BlockSpec((None, D, D), lambda l, j: (l, 0, j)),
            pl.BlockSpec((None, 1, D), lambda l, j: (l, 0, j)),
        ],
        out_specs=pl.BlockSpec((None, SUBLANES, D), lambda l, j: (l, 0, j)),
        out_shape=jax.ShapeDtypeStruct((L, SUBLANES, D6), f32),
        compiler_params=_cparams(("arbitrary", "arbitrary")),
        name="adaln",
    )(c_pad, w_ada, b_ada.reshape(L, 1, D6))


def _moe_combine_kernel(x_ref, y0_ref, y1_ref, w_ref, gm_ref, o_ref):
    w = w_ref[...]
    y = y0_ref[...].astype(f32) * w[:, 0:1] + y1_ref[...].astype(f32) * w[:, 1:2]
    o_ref[...] = x_ref[...] + gm_ref[...] * y


def moe_combine(x2, y0, y1, ew_tok, g_m, *, B, S, tm=512):
    N, D = x2.shape
    tpb = S // tm
    row = lambda i: (i, 0)
    return pl.pallas_call(
        _moe_combine_kernel,
        grid=(N // tm,),
        in_specs=[pl.BlockSpec((tm, D), row), pl.BlockSpec((tm, D), row), pl.BlockSpec((tm, D), row),
                  pl.BlockSpec((tm, TOP_K), row), pl.BlockSpec((None, 1, D), lambda i: (i // tpb, 0, 0))],
        out_specs=pl.BlockSpec((tm, D), row),
        out_shape=jax.ShapeDtypeStruct((N, D), f32),
        compiler_params=_cparams(("arbitrary",)),
        name="moe_combine",
    )(x2, y0, y1, ew_tok, g_m)


def kernel(x, c, w_ada, b_ada, g_norm_mix, g_norm_moe, w_in, cmp_pe_k, cmp_w1_k, cmp_w2_k, cmp_pe_v, cmp_w1_v,
           cmp_w2_v, nsa_q_gain, nsa_k_gain, swa_q_gain, swa_k_gain, swa_sinks, w_branch_a, w_branch_b,
           w_branch_c, w_out, w_router, router_bias, w_exp_gate, w_exp_up, w_exp_down):
    B, S, D = x.shape
    N = B * S
    L = w_ada.shape[0]
    slopes = 2.0 ** (-8.0 * jnp.arange(1, SWA_HEADS + NSA_HEADS + 1, dtype=f32) / (SWA_HEADS + NSA_HEADS))
    swa_slope_l = jnp.repeat(slopes[:SWA_HEADS].reshape(SWA_KV, SWA_HEADS // SWA_KV), Q_BLOCK, axis=1)
    nsa_c = nsa_constants(slopes[SWA_HEADS:], S)
    c_pad = jnp.zeros((SUBLANES, D), f32).at[:B].set(c)
    mod_all = adaln(c_pad, w_ada, b_ada)[:, :B]
    wrT = w_router.T
    rb = router_bias[:, None]
    x2 = x.reshape(N, D)
    for l in range(L):
        sh_a, sc_a, g_a, sh_m, sc_m, g_m = [m[:, None, :] for m in jnp.split(mod_all[l], 6, axis=-1)]
        a_mix = g_norm_mix[l][None, None, :] * (1.0 + sc_a)
        a_moe = g_norm_moe[l][None, None, :] * (1.0 + sc_m)
        pw = _prep_proj_weights(w_in[l], nsa_q_gain[l], nsa_k_gain[l], swa_q_gain[l], swa_k_gain[l])
        kva, kb, kvc, qta, qtb, qtc, vtb, gT = proj_in(x2, a_mix, sh_a, *pw, B=B, S=S)
        cw = _prep_cmp_weights(cmp_pe_k[l], cmp_w1_k[l], cmp_w2_k[l], cmp_pe_v[l], cmp_w1_v[l], cmp_w2_v[l],
                               nsa_k_gain[l][0])
        o_a = nsa_branch(qta, kva, gT, cw, nsa_c, B=B, S=S)
        o_b = sb_attention(_perm_rows(kb.reshape(B, S, B_WIDTH), S), _perm_lanes(vtb, S), qtb, B=B, S=S)
        sink_l = jnp.repeat(swa_sinks[l].reshape(SWA_KV, SWA_HEADS // SWA_KV), Q_BLOCK, axis=1)
        o_c = swa_attention(kvc.reshape(B, S, 2 * SWA_KV * HEAD_DIM), qtc, swa_slope_l, sink_l, B=B, S=S)
        x2, h2, eidx, ew = mix_out(
            x2, a_mix, sh_a, g_a, o_a.reshape(N, A_WIDTH), o_b.reshape(N, B_WIDTH), o_c.reshape(N, C_WIDTH),
            w_in[l][:, _O_MG:_O_END].astype(bf16), w_branch_a[l].astype(bf16), w_branch_b[l].astype(bf16),
            w_branch_c[l].astype(bf16), w_out[l].astype(bf16), a_moe, sh_m, wrT, rb, B=B, S=S)
        slot_tok, blk_e, n_used, pos = moe_dispatch(eidx, N)
        y = moe_ffn(blk_e, n_used, h2[slot_tok], w_exp_gate, w_exp_up, w_exp_down, layer=l)
        x2 = moe_combine(x2, y[pos[:, 0]], y[pos[:, 1]], ew.T, g_m, B=B, S=S)
    return x2.reshape(B, S, D)
```
